```python
import math
import jax
import jax.numpy as jnp
from jax import lax
import numpy as np

D_MODEL = 1024
BATCH = 8
SEQ = 2048
DEPTH = 4
DEC_BATCH = 16
DEC_SEQ = 16
PAST_LEN = 1024

CHUNK = 64
Q_BLOCK = 128
EPS = 1e-6
GLA_HEADS = 4
GLA_DK = 64
GLA_DV = 128
GLA_GATE_RANK = 16
GLA_GATE_NORM = 16.0
MLA_HEADS = 8
MLA_Q_LORA = 384
MLA_KV_LORA = 256
MLA_NOPE = 64
MLA_ROPE = 32
MLA_V = 64
MLA_SCALE = 1.0 / math.sqrt(MLA_NOPE + MLA_ROPE)
ROPE_BASE = 10000.0
MIX_WIDTH = GLA_HEADS * GLA_DV + MLA_HEADS * MLA_V
D_FF = 2816
N_MOD = 9
SPLITS = (GLA_HEADS * GLA_DK, GLA_HEADS * GLA_DK, GLA_HEADS * GLA_DV, GLA_GATE_RANK,
          GLA_HEADS * GLA_DV, MLA_Q_LORA, MLA_KV_LORA, MLA_ROPE)
IN_COLS = GLA_HEADS * (2 * GLA_DK + 2 * GLA_DV) + GLA_GATE_RANK + MLA_Q_LORA + MLA_KV_LORA + MLA_ROPE

kernel_name = 'hymba_gla_mla_macaron_stream'


def rmsnorm(x, g):
    xf = x.astype(jnp.float32)
    y = xf * lax.rsqrt(jnp.mean(xf * xf, axis=-1, keepdims=True) + EPS)
    return (y * g.astype(jnp.float32)).astype(x.dtype)


def rope(x, pos):
    half = MLA_ROPE // 2
    inv = ROPE_BASE ** (-jnp.arange(half, dtype=jnp.float32) / half)
    ang = pos.astype(jnp.float32)[:, None] * inv[None, :]
    ang = ang.reshape(ang.shape[:1] + (1,) * (x.ndim - 3) + (half,))
    cos = jnp.cos(ang).astype(x.dtype)
    sin = jnp.sin(ang).astype(x.dtype)
    x1, x2 = x[..., :half], x[..., half:]
    return jnp.concatenate([x1 * cos - x2 * sin, x1 * sin + x2 * cos], axis=-1)


def swiglu(x, w_in, w_out):
    a, b = jnp.split(x @ w_in, 2, axis=-1)
    return (jax.nn.silu(a) * b) @ w_out


def modulate(x, g, shift, scale):
    return rmsnorm(x, g) * (1.0 + scale) + shift


def ada_mod(c, w, b):
    m = jax.nn.silu(c) @ w + b
    return jnp.split(m[:, None, :], N_MOD, axis=-1)


def split_cols(h):
    idx, acc = [], 0
    for w in SPLITS[:-1]:
        acc += w
        idx.append(acc)
    return jnp.split(h, idx, axis=-1)


def gla_block(q, k, v, logg, s0):
    C = q.shape[2]
    b = jnp.cumsum(logg, axis=2)
    causal = jnp.tril(jnp.ones((C, C), dtype=bool))
    diff = b[:, :, :, None, :] - b[:, :, None, :, :]
    decay = jnp.exp(jnp.where(causal[None, None, :, :, None], diff, -jnp.inf))
    att = jnp.einsum('bhtk,bhsk,bhtsk->bhts', q, k, decay)
    o = jnp.einsum('bhts,bhsv->bhtv', att, v) + jnp.einsum('bhtk,bhkv->bhtv', q * jnp.exp(b), s0)
    b_last = b[:, :, -1:, :]
    s1 = jnp.exp(b_last[:, :, 0, :])[..., None] * s0 + jnp.einsum('bhsk,bhsv->bhkv', k * jnp.exp(b_last - b), v)
    return o, s1


def gla_scan(q, k, v, logg, s0):
    B, H, L, _ = q.shape
    n = L // CHUNK

    def to_blocks(a):
        return jnp.moveaxis(a.reshape(B, H, n, CHUNK, a.shape[-1]), 2, 0)

    def step(s, blk):
        o, s = gla_block(blk[0], blk[1], blk[2], blk[3], s)
        return s, o

    s_final, o = lax.scan(step, s0, (to_blocks(q), to_blocks(k), to_blocks(v), to_blocks(logg)))
    o = jnp.moveaxis(o, 0, 2).reshape(B, H, L, v.shape[-1])
    return o, s_final


def gla_mixer(gq, gk, gv, glr, gog, w_gk2, b_gk2, norm_g, s0, blocked):
    B, L, _ = gq.shape

    def heads(a, d):
        return a.reshape(B, L, GLA_HEADS, d).transpose(0, 2, 1, 3).astype(jnp.float32)

    q = heads(gq, GLA_DK) * (GLA_DK ** -0.5)
    k = heads(gk, GLA_DK)
    v = heads(gv, GLA_DV)
    logg = heads(jax.nn.log_sigmoid((glr @ w_gk2 + b_gk2).astype(jnp.float32)) / GLA_GATE_NORM, GLA_DK)
    s0 = s0.astype(jnp.float32)
    if blocked:
        o, s1 = gla_scan(q, k, v, logg, s0)
    else:
        o, s1 = gla_block(q, k, v, logg, s0)
    o = o.transpose(0, 2, 1, 3).astype(gv.dtype)
    o = rmsnorm(o, norm_g) * jax.nn.silu(gog.reshape(B, L, GLA_HEADS, GLA_DV))
    return o.reshape(B, L, GLA_HEADS * GLA_DV), s1


def mla_project(cq, ckv_raw, kpe_raw, qnorm_g, w_uq, kvnorm_g, pos):
    B, L = cq.shape[:2]
    q = (rmsnorm(cq, qnorm_g) @ w_uq).reshape(B, L, MLA_HEADS, MLA_NOPE + MLA_ROPE)
    q_nope = q[..., :MLA_NOPE]
    q_pe = rope(q[..., MLA_NOPE:], pos)
    ckv = rmsnorm(ckv_raw, kvnorm_g)
    kpe = rope(kpe_raw, pos)
    return q_nope, q_pe, ckv, kpe


def mla_attend(q_nope, q_pe, q_pos, k_nope, k_pe, v, k_pos):
    s = jnp.einsum('bqhd,bkhd->bhqk', q_nope, k_nope) + jnp.einsum('bqhr,bkr->bhqk', q_pe, k_pe)
    s = s.astype(jnp.float32) * MLA_SCALE
    mask = (k_pos[None, :] // CHUNK) <= (q_pos[:, None] // CHUNK)
    s = jnp.where(mask[None, None], s, -jnp.inf)
    p = jax.nn.softmax(s, axis=-1).astype(v.dtype)
    return jnp.einsum('bhqk,bkhv->bqhv', p, v)


def mla_blocks(q_nope, q_pe, q_pos, k_nope, k_pe, v, k_pos):
    B, L = q_nope.shape[:2]
    nb = L // Q_BLOCK

    def blk(a):
        return jnp.swapaxes(a.reshape((B, nb, Q_BLOCK) + a.shape[2:]), 0, 1)

    out = lax.map(lambda t: mla_attend(t[0], t[1], t[2], k_nope, k_pe, v, k_pos),
                  (blk(q_nope), blk(q_pe), q_pos.reshape(nb, Q_BLOCK)))
    return jnp.swapaxes(out, 0, 1).reshape((B, L) + out.shape[3:])


def token_mixer(n, pos, p, l, gla_s0=None, past_ckv=None, past_kpe=None):
    B, L, _ = n.shape
    gq, gk, gv, glr, gog, cq, ckv_raw, kpe_raw = split_cols(n @ p['w_in'][l])
    stream = past_ckv is not None
    if gla_s0 is None:
        gla_s0 = jnp.zeros((B, GLA_HEADS, GLA_DK, GLA_DV), jnp.float32)
    o_gla, s_gla = gla_mixer(gq, gk, gv, glr, gog, p['w_gk2'][l], p['b_gk2'][l],
                             p['gla_norm_g'][l], gla_s0, not stream)
    q_nope, q_pe, ckv, kpe = mla_project(cq, ckv_raw, kpe_raw, p['mla_qnorm_g'][l], p['w_uq'][l],
                                         p['mla_kvnorm_g'][l], pos)
    if stream:
        ckv_all = jnp.concatenate([past_ckv.astype(ckv.dtype), ckv], axis=1)
        kpe_all = jnp.concatenate([past_kpe.astype(kpe.dtype), kpe], axis=1)
        k_pos = jnp.arange(ckv_all.shape[1])
    else:
        ckv_all, kpe_all, k_pos = ckv, kpe, pos
    kv = (ckv_all @ p['w_ukv'][l]).reshape(B, ckv_all.shape[1], MLA_HEADS, MLA_NOPE + MLA_V)
    k_nope, v = kv[..., :MLA_NOPE], kv[..., MLA_NOPE:]
    if stream:
        o_mla = mla_attend(q_nope, q_pe, pos, k_nope, kpe_all, v, k_pos)
    else:
        o_mla = mla_blocks(q_nope, q_pe, pos, k_nope, kpe_all, v, k_pos)
    o = jnp.concatenate([o_gla, o_mla.reshape(B, L, MLA_HEADS * MLA_V)], axis=-1) @ p['w_out'][l]
    return o, ckv, kpe, s_gla


def trunk(x, c, pos, p, gla_state=None, past_ckv=None, past_kpe=None):
    ckvs, kpes, states = [], [], []
    for l in range(DEPTH):
        sh1, sc1, g1, sh2, sc2, g2, sh3, sc3, g3 = ada_mod(c, p['w_ada'][l], p['b_ada'][l])
        x = x + 0.5 * (1.0 + g1) * swiglu(modulate(x, p['ln_g'][l, 0], sh1, sc1),
                                          p['w_ffn1_in'][l], p['w_ffn1_out'][l])
        n = modulate(x, p['ln_g'][l, 1], sh2, sc2)
        if past_ckv is None:
            mix, ckv, kpe, s = token_mixer(n, pos, p, l)
        else:
            mix, ckv, kpe, s = token_mixer(n, pos, p, l, gla_state[l], past_ckv[l], past_kpe[l])
        x = x + (1.0 + g2) * mix
        x = x + 0.5 * (1.0 + g3) * swiglu(modulate(x, p['ln_g'][l, 2], sh3, sc3),
                                          p['w_ffn2_in'][l], p['w_ffn2_out'][l])
        ckvs.append(ckv)
        kpes.append(kpe)
        states.append(s)
    y = rmsnorm(x, p['final_g'])
    return y, jnp.stack(ckvs), jnp.stack(kpes), jnp.stack(states).astype(x.dtype)


def setup_inputs(seed: int = 0) -> dict:
    key = jax.random.key(seed)
    ks = jax.random.split(key, 24)

    def nrm(k, shape, scale=1.0):
        return jax.random.normal(k, shape, jnp.float32) * scale

    return {
        'x_prompt': nrm(ks[0], (BATCH, SEQ, D_MODEL)),
        'x_sample': nrm(ks[1], (DEC_BATCH, DEC_SEQ, D_MODEL)),
        'cache_ckv': nrm(ks[2], (DEPTH, DEC_BATCH, PAST_LEN, MLA_KV_LORA)),
        'cache_kpe': nrm(ks[3], (DEPTH, DEC_BATCH, PAST_LEN, MLA_ROPE)),
        'state_gla': nrm(ks[4], (DEPTH, DEC_BATCH, GLA_HEADS, GLA_DK, GLA_DV), 0.5),
        'c_prompt': nrm(ks[5], (BATCH, D_MODEL)),
        'c_sample': nrm(ks[6], (DEC_BATCH, D_MODEL)),
        'ln_g': 1.0 + nrm(ks[7], (DEPTH, 3, D_MODEL), 0.02),
        'w_ada': nrm(ks[8], (DEPTH, D_MODEL, N_MOD * D_MODEL), 0.3 * D_MODEL ** -0.5),
        'b_ada': nrm(ks[9], (DEPTH, N_MOD * D_MODEL), 0.01),
        'w_ffn1_in': nrm(ks[10], (DEPTH, D_MODEL, 2 * D_FF), D_MODEL ** -0.5),
        'w_ffn1_out': nrm(ks[11], (DEPTH, D_FF, D_MODEL), D_FF ** -0.5),
        'w_ffn2_in': nrm(ks[12], (DEPTH, D_MODEL, 2 * D_FF), D_MODEL ** -0.5),
        'w_ffn2_out': nrm(ks[13], (DEPTH, D_FF, D_MODEL), D_FF ** -0.5),
        'w_in': nrm(ks[14], (DEPTH, D_MODEL, IN_COLS), D_MODEL ** -0.5),
        'w_gk2': nrm(ks[15], (DEPTH, GLA_GATE_RANK, GLA_HEADS * GLA_DK), GLA_GATE_RANK ** -0.5),
        'b_gk2': nrm(ks[16], (DEPTH, GLA_HEADS * GLA_DK), 0.01),
        'gla_norm_g': 1.0 + nrm(ks[17], (DEPTH, GLA_DV), 0.02),
        'mla_qnorm_g': 1.0 + nrm(ks[18], (DEPTH, MLA_Q_LORA), 0.02),
        'w_uq': nrm(ks[19], (DEPTH, MLA_Q_LORA, MLA_HEADS * (MLA_NOPE + MLA_ROPE)), MLA_Q_LORA ** -0.5),
        'mla_kvnorm_g': 1.0 + nrm(ks[20], (DEPTH, MLA_KV_LORA), 0.02),
        'w_ukv': nrm(ks[21], (DEPTH, MLA_KV_LORA, MLA_HEADS * (MLA_NOPE + MLA_V)), MLA_KV_LORA ** -0.5),
        'w_out': nrm(ks[22], (DEPTH, MIX_WIDTH, D_MODEL), MIX_WIDTH ** -0.5),
        'final_g': 1.0 + nrm(ks[23], (D_MODEL,), 0.02),
    }


def reference(x_prompt, x_sample, cache_ckv, cache_kpe, state_gla, c_prompt, c_sample,
              ln_g, w_ada, b_ada, w_ffn1_in, w_ffn1_out, w_ffn2_in, w_ffn2_out,
              w_in, w_gk2, b_gk2, gla_norm_g, mla_qnorm_g, w_uq, mla_kvnorm_g, w_ukv,
              w_out, final_g):
    p = {'ln_g': ln_g, 'w_ada': w_ada, 'b_ada': b_ada,
         'w_ffn1_in': w_ffn1_in, 'w_ffn1_out': w_ffn1_out,
         'w_ffn2_in': w_ffn2_in, 'w_ffn2_out': w_ffn2_out,
         'w_in': w_in, 'w_gk2': w_gk2, 'b_gk2': b_gk2, 'gla_norm_g': gla_norm_g,
         'mla_qnorm_g': mla_qnorm_g, 'w_uq': w_uq, 'mla_kvnorm_g': mla_kvnorm_g,
         'w_ukv': w_ukv, 'w_out': w_out, 'final_g': final_g}
    pos_p = jnp.arange(x_prompt.shape[1])
    y_prompt, ckv_p, kpe_p, gla_p = trunk(x_prompt, c_prompt, pos_p, p)
    past = cache_ckv.shape[2]
    pos_s = past + jnp.arange(x_sample.shape[1])
    y_sample, ckv_s, kpe_s, gla_s = trunk(x_sample, c_sample, pos_s, p, state_gla, cache_ckv, cache_kpe)
    return (y_prompt, y_sample, ckv_p, kpe_p, gla_p, ckv_s, kpe_s, gla_s)
```

```python
import functools
import math

import numpy as np
import jax
import jax.numpy as jnp
from jax import lax
from jax.experimental import pallas as pl
from jax.experimental.pallas import tpu as pltpu

F32 = jnp.float32
BF16 = jnp.bfloat16

D_MODEL = 1024
DEPTH = 4
CHUNK = 64
EPS = 1e-6
GLA_HEADS = 4
GLA_DK = 64
GLA_DV = 128
GLA_GATE_RANK = 16
GLA_GATE_NORM = 16.0
GLA_SUB = 16
MLA_HEADS = 8
MLA_Q_LORA = 384
MLA_KV_LORA = 256
MLA_NOPE = 64
MLA_ROPE = 32
MLA_V = 64
MLA_SCALE = 1.0 / math.sqrt(MLA_NOPE + MLA_ROPE)
ROPE_BASE = 10000.0
D_FF = 2816
N_MOD = 9
LANES = 128
HEAD_BLOCK = 128
NEG_BIG = -1e30

GQK = GLA_HEADS * GLA_DK
GVW = GLA_HEADS * GLA_DV
MQK = MLA_HEADS * HEAD_BLOCK
MVW = MLA_HEADS * MLA_V
O_GQ, O_GK, O_GV, O_GOG = 0, GQK, 2 * GQK, 2 * GQK + GVW
O_CQ = O_GOG + GVW
O_CKV = O_CQ + MLA_Q_LORA
O_KPE = O_CKV + MLA_KV_LORA
O_GLR = O_KPE + HEAD_BLOCK
IN_W = O_GLR + LANES
FFN_TF = 256
VMEM_LIMIT = 56 * 1024 * 1024


def _dot(a, b):
    return jnp.dot(a, b, preferred_element_type=F32)


def _dot_nt(a, b):
    return lax.dot_general(a, b, (((1,), (1,)), ((), ())), preferred_element_type=F32)


def _dot_tn(a, b):
    return lax.dot_general(a, b, (((0,), (0,)), ((), ())), preferred_element_type=F32)


def _rms(x, g):
    ms = jnp.mean(x * x, axis=-1, keepdims=True)
    return x * lax.rsqrt(ms + EPS) * g


def _modulate(x, g, sh, sc, nb):
    tm, d = x.shape
    y = _rms(x, g)
    if nb == 1:
        return y * (1.0 + sc[0]) + sh[0]
    y3 = y.reshape(nb, tm // nb, d)
    return (y3 * (1.0 + sc) + sh).reshape(tm, d)


def _gated_residual(x, y, coef, nb):
    tm, d = x.shape
    if nb == 1:
        return x + coef[0] * y
    return x + (coef * y.reshape(nb, tm // nb, d)).reshape(tm, d)


def _const_spec(shape):
    nd = len(shape)
    return pl.BlockSpec(shape, lambda *_: (0,) * nd, pipeline_mode=pl.Buffered(1))


def _params(sem):
    return pltpu.CompilerParams(dimension_semantics=sem, vmem_limit_bytes=VMEM_LIMIT)


def _ada_kernel(c_ref, w_ref, b_ref, o_ref):
    c = c_ref[...]
    sc = (c * jax.nn.sigmoid(c)).astype(BF16)
    o_ref[0, 0] = _dot(sc, w_ref[0].astype(BF16)) + b_ref[0, 0]


def _ada_call(c_all, w_ada, b_ada):
    nrow = c_all.shape[0]
    b4 = b_ada.reshape(DEPTH, N_MOD, 1, D_MODEL)
    return pl.pallas_call(
        _ada_kernel,
        grid=(DEPTH, N_MOD),
        in_specs=[
            pl.BlockSpec((nrow, D_MODEL), lambda l, j: (0, 0)),
            pl.BlockSpec((1, D_MODEL, D_MODEL), lambda l, j: (l, 0, j)),
            pl.BlockSpec((1, 1, 1, D_MODEL), lambda l, j: (l, j, 0, 0)),
        ],
        out_specs=pl.BlockSpec((1, 1, nrow, D_MODEL), lambda l, j: (l, j, 0, 0)),
        out_shape=jax.ShapeDtypeStruct((DEPTH, N_MOD, nrow, D_MODEL), F32),
        compiler_params=_params(("arbitrary", "arbitrary")),
        name="ada_mod",
    )(c_all, w_ada, b4)


def _ffn_kernel(x_ref, mod_ref, g_ref, win_ref, wout_ref, o_ref, h_ref, *, nb):
    x = x_ref[...]
    n = _modulate(x, g_ref[...], mod_ref[0], mod_ref[1], nb).astype(BF16)
    for c in range(D_FF // FFN_TF):
        lo = c * FFN_TF
        a = _dot(n, win_ref[:, lo:lo + FFN_TF])
        b = _dot(n, win_ref[:, D_FF + lo:D_FF + lo + FFN_TF])
        h_ref[:, lo:lo + FFN_TF] = (a * jax.nn.sigmoid(a) * b).astype(BF16)
    y = _dot(h_ref[...], wout_ref[...])
    o_ref[...] = _gated_residual(x, y, 0.5 * (1.0 + mod_ref[2]), nb)


def _ffn_call(x, mod3, g, w_in, w_out, *, tm, nb, tps):
    t = x.shape[0]
    return pl.pallas_call(
        functools.partial(_ffn_kernel, nb=nb),
        grid=(t // tm,),
        in_specs=[
            pl.BlockSpec((tm, D_MODEL), lambda i: (i, 0)),
            pl.BlockSpec((3, nb, 1, D_MODEL), lambda i: (0, i // tps, 0, 0)),
            _const_spec((1, D_MODEL)),
            _const_spec((D_MODEL, 2 * D_FF)),
            _const_spec((D_FF, D_MODEL)),
        ],
        out_specs=pl.BlockSpec((tm, D_MODEL), lambda i: (i, 0)),
        out_shape=jax.ShapeDtypeStruct((t, D_MODEL), F32),
        scratch_shapes=[pltpu.VMEM((tm, D_FF), BF16)],
        compiler_params=_params(("arbitrary",)),
        name="ffn",
    )(x, mod3, g, w_in, w_out)


def _swap_rope_halves(xb, lane):
    fwd = pltpu.roll(xb, LANES - MLA_ROPE // 2, axis=1)
    bwd = pltpu.roll(xb, MLA_ROPE // 2, axis=1)
    return jnp.where(lane < MLA_NOPE + MLA_ROPE // 2, fwd, bwd)


def _mixin_kernel(x_ref, mod_ref, g_ref, win_ref, wgk_ref, bgk_ref, qg_ref, wuq_ref,
                  kvg_ref, wk_ref, wv_ref, cq_ref, sq_ref, ck_ref, sk_ref,
                  gq_ref, gk_ref, gv_ref, gog_ref, lg_ref, qh_ref, kh_ref, vh_ref,
                  ckv_ref, kpe_ref, *, nb):
    x = x_ref[...]
    tm = x.shape[0]
    n = _modulate(x, g_ref[...], mod_ref[0], mod_ref[1], nb).astype(BF16)
    h = _dot(n, win_ref[...])
    gq_ref[...] = h[:, O_GQ:O_GQ + GQK] * (GLA_DK ** -0.5)
    gk_ref[...] = h[:, O_GK:O_GK + GQK]
    gv_ref[...] = h[:, O_GV:O_GV + GVW]
    gog_ref[...] = h[:, O_GOG:O_GOG + GVW]
    z = _dot(h[:, O_GLR:O_GLR + LANES].astype(BF16), wgk_ref[...]) + bgk_ref[...]
    lsig = jnp.minimum(z, 0.0) - jnp.log1p(jnp.exp(-jnp.abs(z)))
    lg_ref[...] = lsig / GLA_GATE_NORM
    lane = lax.broadcasted_iota(jnp.int32, (tm, LANES), 1)
    cqn = _rms(h[:, O_CQ:O_CQ + MLA_Q_LORA], qg_ref[...]).astype(BF16)
    q = _dot(cqn, wuq_ref[...])
    cq, sq = cq_ref[...], sq_ref[...]
    for hd in range(MLA_HEADS):
        qb = q[:, hd * HEAD_BLOCK:(hd + 1) * HEAD_BLOCK]
        qh_ref[:, hd * HEAD_BLOCK:(hd + 1) * HEAD_BLOCK] = (
            qb * cq + _swap_rope_halves(qb, lane) * sq).astype(BF16)
    ckv = _rms(h[:, O_CKV:O_CKV + MLA_KV_LORA], kvg_ref[...])
    ckv_ref[...] = ckv
    kb = h[:, O_KPE:O_KPE + HEAD_BLOCK]
    kpe = kb * ck_ref[...] + _swap_rope_halves(kb, lane) * sk_ref[...]
    kpe_ref[...] = kpe
    ckv16 = ckv.astype(BF16)
    kn = _dot(ckv16, wk_ref[...])
    for hd in range(MLA_HEADS):
        kh_ref[:, hd * HEAD_BLOCK:(hd + 1) * HEAD_BLOCK] = (
            kn[:, hd * HEAD_BLOCK:(hd + 1) * HEAD_BLOCK] + kpe).astype(BF16)
    vh_ref[...] = _dot(ckv16, wv_ref[...]).astype(BF16)


def _mixin_call(x, mod2, g, w_in, w_gk, b_gk, qg, w_uq, kvg, w_k, w_v, tabs, *, tm, nb, tps):
    t = x.shape[0]
    tok = lambda w: pl.BlockSpec((tm, w), lambda i: (i, 0))
    tab = pl.BlockSpec((tm, LANES), lambda i: (i % tps, 0))
    out_w = [(GQK, F32), (GQK, F32), (GVW, F32), (GVW, F32), (GQK, F32),
             (MQK, BF16), (MQK, BF16), (MVW, BF16), (MLA_KV_LORA, F32), (HEAD_BLOCK, F32)]
    return pl.pallas_call(
        functools.partial(_mixin_kernel, nb=nb),
        grid=(t // tm,),
        in_specs=[
            tok(D_MODEL),
            pl.BlockSpec((2, nb, 1, D_MODEL), lambda i: (0, i // tps, 0, 0)),
            _const_spec((1, D_MODEL)),
            _const_spec((D_MODEL, IN_W)),
            _const_spec((LANES, GQK)),
            _const_spec((1, GQK)),
            _const_spec((1, MLA_Q_LORA)),
            _const_spec((MLA_Q_LORA, MQK)),
            _const_spec((1, MLA_KV_LORA)),
            _const_spec((MLA_KV_LORA, MQK)),
            _const_spec((MLA_KV_LORA, MVW)),
            tab, tab, tab, tab,
        ],
        out_specs=[tok(w) for w, _ in out_w],
        out_shape=[jax.ShapeDtypeStruct((t, w), dt) for w, dt in out_w],
        compiler_params=_params(("arbitrary",)),
        name="mixer_in",
    )(x, mod2, g, w_in, w_gk, b_gk, qg, w_uq, kvg, w_k, w_v, *tabs)


def _dup_head(x, hd, lo_half):
    p = hd // 2
    xp = x[:, p * LANES:(p + 1) * LANES]
    xr = pltpu.roll(xp, GLA_DK, axis=1)
    if hd % 2 == 0:
        return jnp.where(lo_half, xp, xr)
    return jnp.where(lo_half, xr, xp)


def _gla_kernel(q_ref, k_ref, v_ref, og_ref, lg_ref, s0_ref, ng_ref, o_ref, sout_ref, st_ref,
                *, chunk, nch):
    j = pl.program_id(1)
    nsub = chunk // GLA_SUB
    tl = chunk * nch

    @pl.when(j == 0)
    def _():
        for hd in range(GLA_HEADS):
            st_ref[hd] = s0_ref[0, hd].T

    lg = lg_ref[...]
    ri = lax.broadcasted_iota(jnp.int32, (tl, tl), 0)
    ci = lax.broadcasted_iota(jnp.int32, (tl, tl), 1)
    tri = jnp.where((ci <= ri) & (ci // chunk == ri // chunk), 1.0, 0.0).astype(BF16)
    p0 = lg.astype(BF16)
    r1 = lg - p0.astype(F32)
    p1 = r1.astype(BF16)
    p2 = (r1 - p1.astype(F32)).astype(BF16)
    b_all = _dot(tri, p0) + _dot(tri, p1) + _dot(tri, p2)

    row = lax.broadcasted_iota(jnp.int32, (chunk, nsub * GLA_DK), 0)
    blk = lax.broadcasted_iota(jnp.int32, (chunk, nsub * GLA_DK), 1) // GLA_DK
    q_sel = (row // GLA_SUB) == blk
    k_sel = row < (blk + 1) * GLA_SUB
    lo_half = lax.broadcasted_iota(jnp.int32, (chunk, LANES), 1) < GLA_DK
    causal = (lax.broadcasted_iota(jnp.int32, (chunk, chunk), 1)
              <= lax.broadcasted_iota(jnp.int32, (chunk, chunk), 0))
    ng = ng_ref[...]

    def widen(a):
        if nsub == 1:
            return a[:, :GLA_DK]
        return jnp.concatenate([a] * (nsub // 2), axis=1)

    for c in range(nch):
        r0 = c * chunk
        b = b_all[r0:r0 + chunk]
        q = q_ref[r0:r0 + chunk, :]
        k = k_ref[r0:r0 + chunk, :]
        r_own = jnp.broadcast_to(b[0:1], (chunk, GQK)) if nsub == 1 else jnp.concatenate(
            [jnp.broadcast_to(b[i * GLA_SUB:i * GLA_SUB + 1], (GLA_SUB, GQK)) for i in range(nsub)],
            axis=0)
        eb = jnp.exp(b)
        qt = q * jnp.exp(b - r_own)
        qe = q * eb
        kl = k * jnp.exp(b[chunk - 1:chunk] - b)
        for hd in range(GLA_HEADS):
            v = v_ref[r0:r0 + chunk, hd * GLA_DV:(hd + 1) * GLA_DV].astype(BF16)
            b2 = widen(_dup_head(b, hd, lo_half))
            ref_w = jnp.broadcast_to(b2[0:1], b2.shape)
            for i in range(1, nsub):
                ref_w = jnp.where(blk == i, jnp.broadcast_to(b2[i * GLA_SUB:i * GLA_SUB + 1], b2.shape),
                                  ref_w)
            kcat = widen(_dup_head(k, hd, lo_half)) * jnp.exp(jnp.where(k_sel, ref_w - b2, NEG_BIG))
            qcat = jnp.where(q_sel, widen(_dup_head(qt, hd, lo_half)), 0.0)
            att = _dot_nt(qcat.astype(BF16), kcat.astype(BF16))
            att = jnp.where(causal, att, 0.0)
            st = st_ref[hd]
            qe_h = _dup_head(qe, hd, lo_half)[:, :GLA_DK]
            o = _dot(att.astype(BF16), v) + _dot_nt(qe_h.astype(BF16), st.astype(BF16))
            kl_h = _dup_head(kl, hd, lo_half)[:, :GLA_DK]
            gl_h = _dup_head(eb[chunk - 8:chunk], hd, lo_half[0:8])[7:8, :GLA_DK]
            st_ref[hd] = st * gl_h + _dot_tn(v, kl_h.astype(BF16))
            og = og_ref[r0:r0 + chunk, hd * GLA_DV:(hd + 1) * GLA_DV]
            o_ref[r0:r0 + chunk, hd * GLA_DV:(hd + 1) * GLA_DV] = (
                _rms(o, ng) * (og * jax.nn.sigmoid(og))).astype(BF16)

    @pl.when(j == pl.num_programs(1) - 1)
    def _():
        for hd in range(GLA_HEADS):
            sout_ref[0, hd] = st_ref[hd].T


def _gla_call(gq, gk, gv, gog, lg, s0, ng, *, nseq, seq_len, tl, chunk):
    t = gq.shape[0]
    tps = seq_len // tl
    tok = lambda w: pl.BlockSpec((tl, w), lambda b, j: (b * tps + j, 0))
    st_spec = pl.BlockSpec((1, GLA_HEADS, GLA_DK, GLA_DV), lambda b, j: (b, 0, 0, 0))
    return pl.pallas_call(
        functools.partial(_gla_kernel, chunk=chunk, nch=tl // chunk),
        grid=(nseq, tps),
        in_specs=[tok(GQK), tok(GQK), tok(GVW), tok(GVW), tok(GQK), st_spec,
                  pl.BlockSpec((1, GLA_DV), lambda b, j: (0, 0))],
        out_specs=[tok(GVW), st_spec],
        out_shape=[jax.ShapeDtypeStruct((t, GVW), BF16),
                   jax.ShapeDtypeStruct((nseq, GLA_HEADS, GLA_DK, GLA_DV), F32)],
        scratch_shapes=[pltpu.VMEM((GLA_HEADS, GLA_DV, GLA_DK), F32)],
        compiler_params=_params(("arbitrary", "arbitrary")),
        name="gla",
    )(gq, gk, gv, gog, lg, s0, ng)


def _softmax_pv(parts, vs):
    m = functools.reduce(jnp.maximum, [jnp.max(s, axis=-1, keepdims=True) for s in parts])
    acc, den = None, None
    for s, v in zip(parts, vs):
        p = jnp.exp(s - m)
        d = jnp.sum(p, axis=-1, keepdims=True)
        o = _dot(p.astype(BF16), v)
        acc = o if acc is None else acc + o
        den = d if den is None else den + d
    return acc / den


def _attn_prompt_kernel(q_ref, k_ref, v_ref, o_ref, *, seq_len, tq):
    lane = lax.broadcasted_iota(jnp.int32, (tq, LANES), 1)
    qi = lax.broadcasted_iota(jnp.int32, (tq, tq), 0) // CHUNK
    ki = lax.broadcasted_iota(jnp.int32, (tq, tq), 1) // CHUNK
    diag_ok = ki <= qi
    for i in range(seq_len // tq):
        r0 = i * tq
        outs = []
        for hh in range(2):
            cs = slice(hh * HEAD_BLOCK, (hh + 1) * HEAD_BLOCK)
            q = q_ref[r0:r0 + tq, cs]
            s_d = jnp.where(diag_ok, _dot_nt(q, k_ref[r0:r0 + tq, cs]), NEG_BIG)
            parts, vs = [s_d], [v_ref[r0:r0 + tq, :]]
            if i > 0:
                parts.append(_dot_nt(q, k_ref[0:r0, cs]))
                vs.append(v_ref[0:r0, :])
            outs.append(_softmax_pv(parts, vs))
        o_ref[r0:r0 + tq, :] = jnp.where(lane < MLA_V, outs[0], outs[1]).astype(BF16)


def _attn_prompt_call(qh, kh, vh, *, nseq, seq_len, tq):
    t = qh.shape[0]
    npair = MLA_HEADS // 2
    return pl.pallas_call(
        functools.partial(_attn_prompt_kernel, seq_len=seq_len, tq=tq),
        grid=(nseq, npair),
        in_specs=[
            pl.BlockSpec((seq_len, 2 * HEAD_BLOCK), lambda b, p: (b, p)),
            pl.BlockSpec((seq_len, 2 * HEAD_BLOCK), lambda b, p: (b, p)),
            pl.BlockSpec((seq_len, 2 * MLA_V), lambda b, p: (b, p)),
        ],
        out_specs=pl.BlockSpec((seq_len, 2 * MLA_V), lambda b, p: (b, p)),
        out_shape=jax.ShapeDtypeStruct((t, MVW), BF16),
        compiler_params=_params(("arbitrary", "arbitrary")),
        name="mla_attn_prompt",
    )(qh, kh, vh)


def _attn_stream_kernel(q_ref, kc_ref, vc_ref, kn_ref, vn_ref, o_ref, *, tq):
    lane = lax.broadcasted_iota(jnp.int32, (tq, LANES), 1)
    outs = []
    for hh in range(2):
        cs = slice(hh * HEAD_BLOCK, (hh + 1) * HEAD_BLOCK)
        q = q_ref[:, cs]
        s_c = _dot_nt(q, kc_ref[0, :, cs])
        s_n = _dot_nt(q, kn_ref[:, cs])
        outs.append(_softmax_pv([s_c, s_n], [vc_ref[0], vn_ref[...]]))
    o_ref[...] = jnp.where(lane < MLA_V, outs[0], outs[1]).astype(BF16)


def _attn_stream_call(qh, khc, vhc, khn, vhn, layer, *, nseq, tq, past):
    t = qh.shape[0]
    npair = MLA_HEADS // 2
    qpos = past + np.arange(tq)
    if past % CHUNK or not ((qpos[None, :] // CHUNK) <= (qpos[:, None] // CHUNK)).all():
        raise NotImplementedError("new frames spanning several chunks")
    kern = functools.partial(_attn_stream_kernel, tq=tq)
    return pl.pallas_call(
        kern,
        grid=(nseq, npair),
        in_specs=[
            pl.BlockSpec((tq, 2 * HEAD_BLOCK), lambda b, p: (b, p)),
            pl.BlockSpec((1, past, 2 * HEAD_BLOCK), lambda b, p: (layer * nseq + b, 0, p)),
            pl.BlockSpec((1, past, 2 * MLA_V), lambda b, p: (layer * nseq + b, 0, p)),
            pl.BlockSpec((tq, 2 * HEAD_BLOCK), lambda b, p: (b, p)),
            pl.BlockSpec((tq, 2 * MLA_V), lambda b, p: (b, p)),
        ],
        out_specs=pl.BlockSpec((tq, 2 * MLA_V), lambda b, p: (b, p)),
        out_shape=jax.ShapeDtypeStruct((t, MVW), BF16),
        compiler_params=_params(("arbitrary", "arbitrary")),
        name="mla_attn_stream",
    )(qh, khc, vhc, khn, vhn)


def _cache_kv_kernel(ckv_ref, kpe_ref, wk_ref, wv_ref, place_ref, kh_ref, vh_ref):
    c16 = ckv_ref[0].astype(BF16)
    kh_ref[0] = (_dot(c16, wk_ref[0]) + _dot(kpe_ref[0].astype(BF16), place_ref[...])).astype(BF16)
    vh_ref[0] = _dot(c16, wv_ref[0]).astype(BF16)


def _cache_kv_call(cache_ckv, cache_kpe, w_k, w_v, place, *, tr):
    depth, nseq, past, _ = cache_ckv.shape
    rows = nseq * past
    ckv = cache_ckv.reshape(depth, rows, MLA_KV_LORA)
    kpe = cache_kpe.reshape(depth, rows, MLA_ROPE)
    kh, vh = pl.pallas_call(
        _cache_kv_kernel,
        grid=(depth, rows // tr),
        in_specs=[
            pl.BlockSpec((1, tr, MLA_KV_LORA), lambda l, i: (l, i, 0)),
            pl.BlockSpec((1, tr, MLA_ROPE), lambda l, i: (l, i, 0)),
            pl.BlockSpec((1, MLA_KV_LORA, MQK), lambda l, i: (l, 0, 0)),
            pl.BlockSpec((1, MLA_KV_LORA, MVW), lambda l, i: (l, 0, 0)),
            pl.BlockSpec((MLA_ROPE, MQK), lambda l, i: (0, 0)),
        ],
        out_specs=[pl.BlockSpec((1, tr, MQK), lambda l, i: (l, i, 0)),
                   pl.BlockSpec((1, tr, MVW), lambda l, i: (l, i, 0))],
        out_shape=[jax.ShapeDtypeStruct((depth, rows, MQK), BF16),
                   jax.ShapeDtypeStruct((depth, rows, MVW), BF16)],
        compiler_params=_params(("arbitrary", "arbitrary")),
        name="mla_cache_kv",
    )(ckv, kpe, w_k, w_v, place)
    return kh.reshape(depth * nseq, past, MQK), vh.reshape(depth * nseq, past, MVW)


def _mixout_kernel(x_ref, og_ref, om_ref, mod_ref, w_ref, o_ref, *, nb):
    y = _dot(og_ref[...], w_ref[0:GVW, :]) + _dot(om_ref[...], w_ref[GVW:GVW + MVW, :])
    o_ref[...] = _gated_residual(x_ref[...], y, 1.0 + mod_ref[0], nb)


def _mixout_call(x, og, om, mod1, w_out, *, tm, nb, tps):
    t = x.shape[0]
    return pl.pallas_call(
        functools.partial(_mixout_kernel, nb=nb),
        grid=(t // tm,),
        in_specs=[
            pl.BlockSpec((tm, D_MODEL), lambda i: (i, 0)),
            pl.BlockSpec((tm, GVW), lambda i: (i, 0)),
            pl.BlockSpec((tm, MVW), lambda i: (i, 0)),
            pl.BlockSpec((1, nb, 1, D_MODEL), lambda i: (0, i // tps, 0, 0)),
            _const_spec((GVW + MVW, D_MODEL)),
        ],
        out_specs=pl.BlockSpec((tm, D_MODEL), lambda i: (i, 0)),
        out_shape=jax.ShapeDtypeStruct((t, D_MODEL), F32),
        compiler_params=_params(("arbitrary",)),
        name="mixer_out",
    )(x, og, om, mod1, w_out)


def _final_norm_kernel(x_ref, g_ref, o_ref):
    o_ref[...] = _rms(x_ref[...], g_ref[...])


def _final_norm_call(x, g, *, tm):
    t = x.shape[0]
    return pl.pallas_call(
        _final_norm_kernel,
        grid=(t // tm,),
        in_specs=[pl.BlockSpec((tm, D_MODEL), lambda i: (i, 0)), _const_spec((1, D_MODEL))],
        out_specs=pl.BlockSpec((tm, D_MODEL), lambda i: (i, 0)),
        out_shape=jax.ShapeDtypeStruct((t, D_MODEL), F32),
        compiler_params=_params(("arbitrary",)),
        name="final_norm",
    )(x, g)


def _prep_weights(w_in, w_gk2, b_gk2, w_uq, w_ukv):
    s = np.cumsum([0, GQK, GQK, GVW, GLA_GATE_RANK, GVW, MLA_Q_LORA, MLA_KV_LORA, MLA_ROPE])
    gq, gk, gv, glr, gog, cq, ckv, kpe = [w_in[:, :, s[i]:s[i + 1]] for i in range(8)]
    z = lambda w: jnp.zeros((DEPTH, D_MODEL, w), w_in.dtype)
    w_in_p = jnp.concatenate(
        [gq, gk, gv, gog, cq, ckv, z(MLA_NOPE), kpe, z(HEAD_BLOCK - MLA_NOPE - MLA_ROPE),
         glr, z(LANES - GLA_GATE_RANK)], axis=-1).astype(BF16)
    w_gk_p = jnp.concatenate(
        [w_gk2, jnp.zeros((DEPTH, LANES - GLA_GATE_RANK, GQK), w_gk2.dtype)], axis=1).astype(BF16)
    b_gk_p = b_gk2.reshape(DEPTH, 1, GQK)
    uq = w_uq.reshape(DEPTH, MLA_Q_LORA, MLA_HEADS, MLA_NOPE + MLA_ROPE)
    uq = jnp.pad(uq, ((0, 0), (0, 0), (0, 0), (0, HEAD_BLOCK - MLA_NOPE - MLA_ROPE)))
    w_uq_p = uq.reshape(DEPTH, MLA_Q_LORA, MQK).astype(BF16)
    ukv = w_ukv.reshape(DEPTH, MLA_KV_LORA, MLA_HEADS, MLA_NOPE + MLA_V)
    uk = jnp.pad(ukv[..., :MLA_NOPE], ((0, 0), (0, 0), (0, 0), (0, HEAD_BLOCK - MLA_NOPE)))
    w_k_p = uk.reshape(DEPTH, MLA_KV_LORA, MQK).astype(BF16)
    w_v_p = ukv[..., MLA_NOPE:].reshape(DEPTH, MLA_KV_LORA, MVW).astype(BF16)
    return w_in_p, w_gk_p, b_gk_p, w_uq_p, w_k_p, w_v_p


def _rope_tables(pos, reps):
    half = MLA_ROPE // 2
    inv = ROPE_BASE ** (-jnp.arange(half, dtype=F32) / half)
    ang = pos.astype(F32)[:, None] * inv[None, :]
    cos, sin = jnp.cos(ang), jnp.sin(ang)
    n = pos.shape[0]
    one, zero = jnp.ones((n, MLA_NOPE), F32), jnp.zeros((n, MLA_NOPE), F32)
    pad = jnp.zeros((n, HEAD_BLOCK - MLA_NOPE - MLA_ROPE), F32)
    c = jnp.concatenate([one, cos, cos, pad], axis=1)
    s = jnp.concatenate([zero, -sin, sin, pad], axis=1)
    tabs = (c * MLA_SCALE, s * MLA_SCALE, c, s)
    return tuple(jnp.tile(a, (reps, 1)) for a in tabs)


def _kpe_placement():
    p = np.zeros((MLA_ROPE, MQK), np.float32)
    for hd in range(MLA_HEADS):
        for r in range(MLA_ROPE):
            p[r, hd * HEAD_BLOCK + MLA_NOPE + r] = 1.0
    return jnp.asarray(p, BF16)


def _trunk(x, mods, wts, tabs, *, nseq, seq_len, tm, nb, gla_tl, gla_chunk, s0, stream=None):
    tps = max(seq_len // tm, 1) if nb == 1 else 1
    ckvs, kpes, states = [], [], []
    for l in range(DEPTH):
        m = mods[l]
        x = _ffn_call(x, m[0:3], wts["ln_g"][l, 0:1], wts["ffn1_in"][l], wts["ffn1_out"][l],
                      tm=tm, nb=nb, tps=tps)
        gq, gk, gv, gog, lg, qh, kh, vh, ckv, kpe = _mixin_call(
            x, m[3:5], wts["ln_g"][l, 1:2], wts["w_in"][l], wts["w_gk"][l], wts["b_gk"][l],
            wts["qg"][l:l + 1], wts["w_uq"][l], wts["kvg"][l:l + 1], wts["w_k"][l], wts["w_v"][l],
            tabs, tm=tm, nb=nb, tps=tps)
        og, st = _gla_call(gq, gk, gv, gog, lg, s0[l], wts["ng"][l:l + 1],
                           nseq=nseq, seq_len=seq_len, tl=gla_tl, chunk=gla_chunk)
        if stream is None:
            om = _attn_prompt_call(qh, kh, vh, nseq=nseq, seq_len=seq_len, tq=min(256, seq_len))
        else:
            khc, vhc, past = stream
            om = _attn_stream_call(qh, khc, vhc, kh, vh, l, nseq=nseq, tq=seq_len, past=past)
        x = _mixout_call(x, og, om, m[5:6], wts["w_out"][l], tm=tm, nb=nb, tps=tps)
        x = _ffn_call(x, m[6:9], wts["ln_g"][l, 2:3], wts["ffn2_in"][l], wts["ffn2_out"][l],
                      tm=tm, nb=nb, tps=tps)
        ckvs.append(ckv)
        kpes.append(kpe[:, MLA_NOPE:MLA_NOPE + MLA_ROPE])
        states.append(st)
    y = _final_norm_call(x, wts["final_g"], tm=tm)
    return y, jnp.stack(ckvs), jnp.stack(kpes), jnp.stack(states)


def kernel(x_prompt, x_sample, cache_ckv, cache_kpe, state_gla, c_prompt, c_sample, ln_g, w_ada, b_ada, w_ffn1_in, w_ffn1_out, w_ffn2_in, w_ffn2_out, w_in, w_gk2, b_gk2, gla_norm_g, mla_qnorm_g, w_uq, mla_kvnorm_g, w_ukv, w_out, final_g):
    bsz, seq, _ = x_prompt.shape
    dbs, dseq, _ = x_sample.shape
    past = cache_ckv.shape[2]

    w_in_p, w_gk_p, b_gk_p, w_uq_p, w_k_p, w_v_p = _prep_weights(w_in, w_gk2, b_gk2, w_uq, w_ukv)
    wts = {
        "ln_g": ln_g, "final_g": final_g.reshape(1, D_MODEL),
        "ffn1_in": w_ffn1_in.astype(BF16), "ffn1_out": w_ffn1_out.astype(BF16),
        "ffn2_in": w_ffn2_in.astype(BF16), "ffn2_out": w_ffn2_out.astype(BF16),
        "w_in": w_in_p, "w_gk": w_gk_p, "b_gk": b_gk_p, "qg": mla_qnorm_g, "w_uq": w_uq_p,
        "kvg": mla_kvnorm_g, "w_k": w_k_p, "w_v": w_v_p, "ng": gla_norm_g,
        "w_out": w_out.astype(BF16),
    }
    mods = _ada_call(jnp.concatenate([c_prompt, c_sample], axis=0), w_ada, b_ada)
    mods_p = mods[:, :, :bsz].reshape(DEPTH, N_MOD, bsz, 1, D_MODEL)
    mods_s = mods[:, :, bsz:].reshape(DEPTH, N_MOD, dbs, 1, D_MODEL)

    tm_p = min(512, seq)
    y_p, ckv_p, kpe_p, gla_p = _trunk(
        x_prompt.reshape(bsz * seq, D_MODEL), mods_p, wts, _rope_tables(jnp.arange(seq), 1),
        nseq=bsz, seq_len=seq, tm=tm_p, nb=1, gla_tl=min(256, seq), gla_chunk=CHUNK,
        s0=jnp.zeros((DEPTH, bsz, GLA_HEADS, GLA_DK, GLA_DV), F32))

    khc, vhc = _cache_kv_call(cache_ckv, cache_kpe, w_k_p, w_v_p, _kpe_placement(), tr=min(1024, past))
    y_s, ckv_s, kpe_s, gla_s = _trunk(
        x_sample.reshape(dbs * dseq, D_MODEL), mods_s, wts,
        _rope_tables(past + jnp.arange(dseq), dbs),
        nseq=dbs, seq_len=dseq, tm=dbs * dseq, nb=dbs, gla_tl=dseq, gla_chunk=dseq,
        s0=state_gla.astype(F32), stream=(khc, vhc, past))

    return (y_p.reshape(bsz, seq, D_MODEL), y_s.reshape(dbs, dseq, D_MODEL),
            ckv_p.reshape(DEPTH, bsz, seq, MLA_KV_LORA), kpe_p.reshape(DEPTH, bsz, seq, MLA_ROPE),
            gla_p,
            ckv_s.reshape(DEPTH, dbs, dseq, MLA_KV_LORA), kpe_s.reshape(DEPTH, dbs, dseq, MLA_ROPE),
            gla_s)
```

```python
import functools
import math

import numpy as np
import jax
import jax.numpy as jnp
from jax import lax
from jax.experimental import pallas as pl
from jax.experimental.pallas import tpu as pltpu

F32 = jnp.float32
BF16 = jnp.bfloat16

D_MODEL = 1024
DEPTH = 4
CHUNK = 64
EPS = 1e-6
GLA_HEADS = 4
GLA_DK = 64
GLA_DV = 128
GLA_GATE_RANK = 16
GLA_GATE_NORM = 16.0
GLA_SUB = 16
MLA_HEADS = 8
MLA_Q_LORA = 384
MLA_KV_LORA = 256
MLA_NOPE = 64
MLA_ROPE = 32
MLA_V = 64
MLA_SCALE = 1.0 / math.sqrt(MLA_NOPE + MLA_ROPE)
LOG2E = math.log2(math.e)
ROPE_BASE = 10000.0
D_FF = 2816
N_MOD = 9
LANES = 128
HEAD_BLOCK = 128
NEG_BIG = -1e30

GQK = GLA_HEADS * GLA_DK
GVW = GLA_HEADS * GLA_DV
MQK = MLA_HEADS * HEAD_BLOCK
MVW = MLA_HEADS * MLA_V
O_GQ, O_GK, O_GV, O_GOG = 0, GQK, 2 * GQK, 2 * GQK + GVW
O_CQ = O_GOG + GVW
O_CKV = O_CQ + MLA_Q_LORA
O_KPE = O_CKV + MLA_KV_LORA
O_GLR = O_KPE + HEAD_BLOCK
IN_W = O_GLR + LANES
FFN_TF = 256
VMEM_LIMIT = 56 * 1024 * 1024


def _dot(a, b):
    return jnp.dot(a, b, preferred_element_type=F32)


def _dot_nt(a, b):
    return lax.dot_general(a, b, (((1,), (1,)), ((), ())), preferred_element_type=F32)


def _dot_tn(a, b):
    return lax.dot_general(a, b, (((0,), (0,)), ((), ())), preferred_element_type=F32)


def _rms(x, g):
    ms = jnp.mean(x * x, axis=-1, keepdims=True)
    return x * lax.rsqrt(ms + EPS) * g


def _modulate(x, g, sh, sc, nb):
    tm, d = x.shape
    y = _rms(x, g)
    if nb == 1:
        return y * (1.0 + sc[0]) + sh[0]
    y3 = y.reshape(nb, tm // nb, d)
    return (y3 * (1.0 + sc) + sh).reshape(tm, d)


def _gated_residual(x, y, coef, nb):
    tm, d = x.shape
    if nb == 1:
        return x + coef[0] * y
    return x + (coef * y.reshape(nb, tm // nb, d)).reshape(tm, d)


def _const_spec(shape):
    nd = len(shape)
    return pl.BlockSpec(shape, lambda *_: (0,) * nd, pipeline_mode=pl.Buffered(1))


def _params(sem):
    return pltpu.CompilerParams(dimension_semantics=sem, vmem_limit_bytes=VMEM_LIMIT)


def _ada_kernel(c_ref, w_ref, b_ref, o_ref):
    c = c_ref[...]
    sc = (c * jax.nn.sigmoid(c)).astype(BF16)
    o_ref[0, 0] = _dot(sc, w_ref[0].astype(BF16)) + b_ref[0, 0]


def _ada_call(c_all, w_ada, b_ada):
    nrow = c_all.shape[0]
    b4 = b_ada.reshape(DEPTH, N_MOD, 1, D_MODEL)
    return pl.pallas_call(
        _ada_kernel,
        grid=(DEPTH, N_MOD),
        in_specs=[
            pl.BlockSpec((nrow, D_MODEL), lambda l, j: (0, 0)),
            pl.BlockSpec((1, D_MODEL, D_MODEL), lambda l, j: (l, 0, j)),
            pl.BlockSpec((1, 1, 1, D_MODEL), lambda l, j: (l, j, 0, 0)),
        ],
        out_specs=pl.BlockSpec((1, 1, nrow, D_MODEL), lambda l, j: (l, j, 0, 0)),
        out_shape=jax.ShapeDtypeStruct((DEPTH, N_MOD, nrow, D_MODEL), F32),
        compiler_params=_params(("arbitrary", "arbitrary")),
        name="ada_mod",
    )(c_all, w_ada, b4)


def _ffn_body(x, sh, sc, gate, g, win_ref, wout_ref, h_ref, nb):
    n = _modulate(x, g, sh, sc, nb).astype(BF16)
    for c in range(D_FF // FFN_TF):
        lo = c * FFN_TF
        a = _dot(n, win_ref[:, lo:lo + FFN_TF])
        b = _dot(n, win_ref[:, D_FF + lo:D_FF + lo + FFN_TF])
        h_ref[:, lo:lo + FFN_TF] = (a * jax.nn.sigmoid(a) * b).astype(BF16)
    y = _dot(h_ref[...], wout_ref[...])
    return _gated_residual(x, y, 0.5 * (1.0 + gate), nb)


def _ffn_kernel(x_ref, mod_ref, g_ref, win_ref, wout_ref, o_ref, h_ref, *, nb):
    o_ref[...] = _ffn_body(x_ref[...], mod_ref[0], mod_ref[1], mod_ref[2], g_ref[...],
                           win_ref, wout_ref, h_ref, nb)


def _mix_ffn_kernel(x_ref, og_ref, om_ref, mod_ref, wmix_ref, g_ref, win_ref, wout_ref, fg_ref,
                    o_ref, h_ref, *, nb, final_norm):
    y = _dot(og_ref[...], wmix_ref[0:GVW, :]) + _dot(om_ref[...], wmix_ref[GVW:GVW + MVW, :])
    x = _gated_residual(x_ref[...], y, 1.0 + mod_ref[0], nb)
    x = _ffn_body(x, mod_ref[1], mod_ref[2], mod_ref[3], g_ref[...], win_ref, wout_ref, h_ref, nb)
    o_ref[...] = _rms(x, fg_ref[...]) if final_norm else x


def _ffn_call(x, mod3, g, w_in, w_out, *, tm, nb, tps):
    t = x.shape[0]
    return pl.pallas_call(
        functools.partial(_ffn_kernel, nb=nb),
        grid=(t // tm,),
        in_specs=[
            pl.BlockSpec((tm, D_MODEL), lambda i: (i, 0)),
            pl.BlockSpec((3, nb, 1, D_MODEL), lambda i: (0, i // tps, 0, 0)),
            _const_spec((1, D_MODEL)),
            _const_spec((D_MODEL, 2 * D_FF)),
            _const_spec((D_FF, D_MODEL)),
        ],
        out_specs=pl.BlockSpec((tm, D_MODEL), lambda i: (i, 0)),
        out_shape=jax.ShapeDtypeStruct((t, D_MODEL), F32),
        scratch_shapes=[pltpu.VMEM((tm, D_FF), BF16)],
        compiler_params=_params(("arbitrary",)),
        name="ffn",
    )(x, mod3, g, w_in, w_out)


def _mix_ffn_call(x, og, om, mod4, w_mix, g, w_in, w_out, fg, *, tm, nb, tps, final_norm):
    t = x.shape[0]
    tok = lambda w: pl.BlockSpec((tm, w), lambda i: (i, 0))
    return pl.pallas_call(
        functools.partial(_mix_ffn_kernel, nb=nb, final_norm=final_norm),
        grid=(t // tm,),
        in_specs=[
            tok(D_MODEL), tok(GVW), tok(MVW),
            pl.BlockSpec((4, nb, 1, D_MODEL), lambda i: (0, i // tps, 0, 0)),
            _const_spec((GVW + MVW, D_MODEL)),
            _const_spec((1, D_MODEL)),
            _const_spec((D_MODEL, 2 * D_FF)),
            _const_spec((D_FF, D_MODEL)),
            _const_spec((1, D_MODEL)),
        ],
        out_specs=tok(D_MODEL),
        out_shape=jax.ShapeDtypeStruct((t, D_MODEL), F32),
        scratch_shapes=[pltpu.VMEM((tm, D_FF), BF16)],
        compiler_params=_params(("arbitrary",)),
        name="mix_ffn",
    )(x, og, om, mod4, w_mix, g, w_in, w_out, fg)


def _swap_rope_halves(xb, lane):
    fwd = pltpu.roll(xb, LANES - MLA_ROPE // 2, axis=1)
    bwd = pltpu.roll(xb, MLA_ROPE // 2, axis=1)
    return jnp.where(lane < MLA_NOPE + MLA_ROPE // 2, fwd, bwd)


def _mixin_kernel(x_ref, mod_ref, g_ref, win_ref, wgk_ref, bgk_ref, qg_ref, wuq_ref,
                  kvg_ref, wk_ref, wv_ref, cq_ref, sq_ref, ck_ref, sk_ref,
                  gq_ref, gk_ref, gv_ref, gog_ref, lg_ref, qh_ref, kh_ref, vh_ref,
                  ckv_ref, kpe_ref, *, nb):
    x = x_ref[...]
    tm = x.shape[0]
    n = _modulate(x, g_ref[...], mod_ref[0], mod_ref[1], nb).astype(BF16)
    h = _dot(n, win_ref[...])
    gq_ref[...] = h[:, O_GQ:O_GQ + GQK] * (GLA_DK ** -0.5)
    gk_ref[...] = h[:, O_GK:O_GK + GQK]
    gv_ref[...] = h[:, O_GV:O_GV + GVW]
    gog_ref[...] = h[:, O_GOG:O_GOG + GVW]
    z = _dot(h[:, O_GLR:O_GLR + LANES].astype(BF16), wgk_ref[...]) + bgk_ref[...]
    lsig = jnp.minimum(z, 0.0) - jnp.log1p(jnp.exp(-jnp.abs(z)))
    lg_ref[...] = lsig / GLA_GATE_NORM
    lane = lax.broadcasted_iota(jnp.int32, (tm, LANES), 1)
    cqn = _rms(h[:, O_CQ:O_CQ + MLA_Q_LORA], qg_ref[...]).astype(BF16)
    q = _dot(cqn, wuq_ref[...])
    cq, sq = cq_ref[...], sq_ref[...]
    for hd in range(MLA_HEADS):
        qb = q[:, hd * HEAD_BLOCK:(hd + 1) * HEAD_BLOCK]
        qh_ref[:, hd * HEAD_BLOCK:(hd + 1) * HEAD_BLOCK] = (
            qb * cq + _swap_rope_halves(qb, lane) * sq).astype(BF16)
    ckv = _rms(h[:, O_CKV:O_CKV + MLA_KV_LORA], kvg_ref[...])
    ckv_ref[...] = ckv
    kb = h[:, O_KPE:O_KPE + HEAD_BLOCK]
    kpe = kb * ck_ref[...] + _swap_rope_halves(kb, lane) * sk_ref[...]
    kpe_ref[...] = kpe
    ckv16 = ckv.astype(BF16)
    kn = _dot(ckv16, wk_ref[...])
    for hd in range(MLA_HEADS):
        kh_ref[:, hd * HEAD_BLOCK:(hd + 1) * HEAD_BLOCK] = (
            kn[:, hd * HEAD_BLOCK:(hd + 1) * HEAD_BLOCK] + kpe).astype(BF16)
    vh_ref[...] = _dot(ckv16, wv_ref[...]).astype(BF16)


def _mixin_call(x, mod2, g, w_in, w_gk, b_gk, qg, w_uq, kvg, w_k, w_v, tabs, *, tm, nb, tps):
    t = x.shape[0]
    tok = lambda w: pl.BlockSpec((tm, w), lambda i: (i, 0))
    tab = pl.BlockSpec((tm, LANES), lambda i: (i % tps, 0))
    out_w = [(GQK, F32), (GQK, F32), (GVW, F32), (GVW, F32), (GQK, F32),
             (MQK, BF16), (MQK, BF16), (MVW, BF16), (MLA_KV_LORA, F32), (HEAD_BLOCK, F32)]
    return pl.pallas_call(
        functools.partial(_mixin_kernel, nb=nb),
        grid=(t // tm,),
        in_specs=[
            tok(D_MODEL),
            pl.BlockSpec((2, nb, 1, D_MODEL), lambda i: (0, i // tps, 0, 0)),
            _const_spec((1, D_MODEL)),
            _const_spec((D_MODEL, IN_W)),
            _const_spec((LANES, GQK)),
            _const_spec((1, GQK)),
            _const_spec((1, MLA_Q_LORA)),
            _const_spec((MLA_Q_LORA, MQK)),
            _const_spec((1, MLA_KV_LORA)),
            _const_spec((MLA_KV_LORA, MQK)),
            _const_spec((MLA_KV_LORA, MVW)),
            tab, tab, tab, tab,
        ],
        out_specs=[tok(w) for w, _ in out_w],
        out_shape=[jax.ShapeDtypeStruct((t, w), dt) for w, dt in out_w],
        compiler_params=_params(("arbitrary",)),
        name="mixer_in",
    )(x, mod2, g, w_in, w_gk, b_gk, qg, w_uq, kvg, w_k, w_v, *tabs)


def _dup_head(x, hd, lo_half):
    p = hd // 2
    xp = x[:, p * LANES:(p + 1) * LANES]
    xr = pltpu.roll(xp, GLA_DK, axis=1)
    if hd % 2 == 0:
        return jnp.where(lo_half, xp, xr)
    return jnp.where(lo_half, xr, xp)


def _gla_kernel(q_ref, k_ref, v_ref, og_ref, lg_ref, s0_ref, ng_ref, o_ref, sout_ref, st_ref,
                *, chunk, nch):
    j = pl.program_id(1)
    nsub = chunk // GLA_SUB
    tl = chunk * nch

    @pl.when(j == 0)
    def _():
        for hd in range(GLA_HEADS):
            st_ref[hd] = s0_ref[0, hd].T

    lg = lg_ref[...]
    ri = lax.broadcasted_iota(jnp.int32, (tl, tl), 0)
    ci = lax.broadcasted_iota(jnp.int32, (tl, tl), 1)
    tri = jnp.where((ci <= ri) & (ci // chunk == ri // chunk), 1.0, 0.0).astype(BF16)
    p0 = lg.astype(BF16)
    r1 = lg - p0.astype(F32)
    p1 = r1.astype(BF16)
    p2 = (r1 - p1.astype(F32)).astype(BF16)
    b_all = _dot(tri, p0) + _dot(tri, p1) + _dot(tri, p2)

    row = lax.broadcasted_iota(jnp.int32, (chunk, nsub * GLA_DK), 0)
    blk = lax.broadcasted_iota(jnp.int32, (chunk, nsub * GLA_DK), 1) // GLA_DK
    q_sel = (row // GLA_SUB) == blk
    k_sel = row < (blk + 1) * GLA_SUB
    lo_half = lax.broadcasted_iota(jnp.int32, (chunk, LANES), 1) < GLA_DK
    causal = (lax.broadcasted_iota(jnp.int32, (chunk, chunk), 1)
              <= lax.broadcasted_iota(jnp.int32, (chunk, chunk), 0))
    ng = ng_ref[...]

    def widen(a):
        if nsub == 1:
            return a[:, :GLA_DK]
        return jnp.concatenate([a] * (nsub // 2), axis=1)

    def v_of(c, hd):
        return v_ref[c * chunk:(c + 1) * chunk, hd * GLA_DV:(hd + 1) * GLA_DV].astype(BF16)

    incr = {}
    for c in range(nch):
        b = b_all[c * chunk:(c + 1) * chunk]
        kl = k_ref[c * chunk:(c + 1) * chunk, :] * jnp.exp(b[chunk - 1:chunk] - b)
        g_end = jnp.exp(b[chunk - 8:chunk])
        for hd in range(GLA_HEADS):
            kl_h = _dup_head(kl, hd, lo_half)[:, :GLA_DK]
            gl_h = _dup_head(g_end, hd, lo_half[0:8])[7:8, :GLA_DK]
            incr[c, hd] = (gl_h, _dot_tn(v_of(c, hd), kl_h.astype(BF16)))
    start = {}
    for hd in range(GLA_HEADS):
        st = st_ref[hd]
        for c in range(nch):
            start[c, hd] = st
            st = st * incr[c, hd][0] + incr[c, hd][1]
        st_ref[hd] = st

    for c in range(nch):
        r0 = c * chunk
        b = b_all[r0:r0 + chunk]
        q = q_ref[r0:r0 + chunk, :]
        k = k_ref[r0:r0 + chunk, :]
        r_own = jnp.broadcast_to(b[0:1], (chunk, GQK)) if nsub == 1 else jnp.concatenate(
            [jnp.broadcast_to(b[i * GLA_SUB:i * GLA_SUB + 1], (GLA_SUB, GQK)) for i in range(nsub)],
            axis=0)
        qt = q * jnp.exp(b - r_own)
        qe = q * jnp.exp(b)
        for hd in range(GLA_HEADS):
            v = v_of(c, hd)
            b2 = widen(_dup_head(b, hd, lo_half))
            ref_w = jnp.broadcast_to(b2[0:1], b2.shape)
            for i in range(1, nsub):
                ref_w = jnp.where(blk == i, jnp.broadcast_to(b2[i * GLA_SUB:i * GLA_SUB + 1], b2.shape),
                                  ref_w)
            kcat = widen(_dup_head(k, hd, lo_half)) * jnp.exp(jnp.where(k_sel, ref_w - b2, NEG_BIG))
            qcat = jnp.where(q_sel, widen(_dup_head(qt, hd, lo_half)), 0.0)
            att = _dot_nt(qcat.astype(BF16), kcat.astype(BF16))
            att = jnp.where(causal, att, 0.0)
            qe_h = _dup_head(qe, hd, lo_half)[:, :GLA_DK]
            o = _dot(att.astype(BF16), v) + _dot_nt(qe_h.astype(BF16), start[c, hd].astype(BF16))
            og = og_ref[r0:r0 + chunk, hd * GLA_DV:(hd + 1) * GLA_DV]
            o_ref[r0:r0 + chunk, hd * GLA_DV:(hd + 1) * GLA_DV] = (
                _rms(o, ng) * (og * jax.nn.sigmoid(og))).astype(BF16)

    @pl.when(j == pl.num_programs(1) - 1)
    def _():
        for hd in range(GLA_HEADS):
            sout_ref[0, hd] = st_ref[hd].T


def _gla_call(gq, gk, gv, gog, lg, s0, ng, *, nseq, seq_len, tl, chunk):
    t = gq.shape[0]
    tps = seq_len // tl
    tok = lambda w: pl.BlockSpec((tl, w), lambda b, j: (b * tps + j, 0))
    st_spec = pl.BlockSpec((1, GLA_HEADS, GLA_DK, GLA_DV), lambda b, j: (b, 0, 0, 0))
    return pl.pallas_call(
        functools.partial(_gla_kernel, chunk=chunk, nch=tl // chunk),
        grid=(nseq, tps),
        in_specs=[tok(GQK), tok(GQK), tok(GVW), tok(GVW), tok(GQK), st_spec,
                  pl.BlockSpec((1, GLA_DV), lambda b, j: (0, 0))],
        out_specs=[tok(GVW), st_spec],
        out_shape=[jax.ShapeDtypeStruct((t, GVW), BF16),
                   jax.ShapeDtypeStruct((nseq, GLA_HEADS, GLA_DK, GLA_DV), F32)],
        scratch_shapes=[pltpu.VMEM((GLA_HEADS, GLA_DV, GLA_DK), F32)],
        compiler_params=_params(("arbitrary", "arbitrary")),
        name="gla",
    )(gq, gk, gv, gog, lg, s0, ng)


def _softmax_pv(parts, vs):
    m = functools.reduce(jnp.maximum, [jnp.max(s, axis=-1, keepdims=True) for s in parts])
    acc, den = None, None
    for s, v in zip(parts, vs):
        p = jnp.exp2(s - m)
        d = jnp.sum(p, axis=-1, keepdims=True)
        o = _dot(p.astype(BF16), v)
        acc = o if acc is None else acc + o
        den = d if den is None else den + d
    return acc / den


def _attn_prompt_kernel(q_ref, k_ref, v_ref, o_ref, *, seq_len, tq):
    lane = lax.broadcasted_iota(jnp.int32, (tq, LANES), 1)
    qi = lax.broadcasted_iota(jnp.int32, (tq, tq), 0) // CHUNK
    ki = lax.broadcasted_iota(jnp.int32, (tq, tq), 1) // CHUNK
    diag_ok = ki <= qi
    for i in range(seq_len // tq):
        r0 = i * tq
        outs = []
        for hh in range(2):
            cs = slice(hh * HEAD_BLOCK, (hh + 1) * HEAD_BLOCK)
            q = q_ref[r0:r0 + tq, cs]
            s_d = jnp.where(diag_ok, _dot_nt(q, k_ref[r0:r0 + tq, cs]), NEG_BIG)
            parts, vs = [s_d], [v_ref[r0:r0 + tq, :]]
            if i > 0:
                parts.append(_dot_nt(q, k_ref[0:r0, cs]))
                vs.append(v_ref[0:r0, :])
            outs.append(_softmax_pv(parts, vs))
        o_ref[r0:r0 + tq, :] = jnp.where(lane < MLA_V, outs[0], outs[1]).astype(BF16)


def _attn_prompt_call(qh, kh, vh, *, nseq, seq_len, tq):
    t = qh.shape[0]
    npair = MLA_HEADS // 2
    return pl.pallas_call(
        functools.partial(_attn_prompt_kernel, seq_len=seq_len, tq=tq),
        grid=(nseq, npair),
        in_specs=[
            pl.BlockSpec((seq_len, 2 * HEAD_BLOCK), lambda b, p: (b, p)),
            pl.BlockSpec((seq_len, 2 * HEAD_BLOCK), lambda b, p: (b, p)),
            pl.BlockSpec((seq_len, 2 * MLA_V), lambda b, p: (b, p)),
        ],
        out_specs=pl.BlockSpec((seq_len, 2 * MLA_V), lambda b, p: (b, p)),
        out_shape=jax.ShapeDtypeStruct((t, MVW), BF16),
        compiler_params=_params(("arbitrary", "arbitrary")),
        name="mla_attn_prompt",
    )(qh, kh, vh)


def _attn_stream_kernel(q_ref, ckc_ref, kpc_ref, ckn_ref, kpn_ref, wk_ref, wv_ref, place_ref,
                        o_ref, *, tq):
    q = q_ref[...]
    ckc = ckc_ref[0, 0].astype(BF16)
    kpc = _dot(kpc_ref[0, 0].astype(BF16), place_ref[...]).astype(BF16)
    ckn = ckn_ref[...].astype(BF16)
    kpn = kpn_ref[...].astype(BF16)
    blocks = [slice(hd * HEAD_BLOCK, (hd + 1) * HEAD_BLOCK) for hd in range(MLA_HEADS)]
    q_abs = jnp.concatenate([_dot_nt(q[:, bs], wk_ref[:, bs]) for bs in blocks], axis=0).astype(BF16)
    q_blk = jnp.concatenate([q[:, bs] for bs in blocks], axis=0)
    s_c = _dot_nt(q_abs, ckc) + _dot_nt(q_blk, kpc)
    s_n = _dot_nt(q_abs, ckn) + _dot_nt(q_blk, kpn)
    lat = _softmax_pv([s_c, s_n], [ckc, ckn]).astype(BF16)
    lane = lax.broadcasted_iota(jnp.int32, (tq, LANES), 1)
    for p in range(MLA_HEADS // 2):
        wv = wv_ref[:, p * LANES:(p + 1) * LANES]
        even = _dot(lat[(2 * p) * tq:(2 * p + 1) * tq], wv)
        odd = _dot(lat[(2 * p + 1) * tq:(2 * p + 2) * tq], wv)
        o_ref[:, p * LANES:(p + 1) * LANES] = jnp.where(lane < MLA_V, even, odd).astype(BF16)


def _attn_stream_call(qh, cache_ckv, cache_kpe, ckv_new, kpe_new, w_k, w_v, place, layer,
                      *, nseq, tq, past):
    t = qh.shape[0]
    qpos = past + np.arange(tq)
    if past % CHUNK or not ((qpos[None, :] // CHUNK) <= (qpos[:, None] // CHUNK)).all():
        raise NotImplementedError("new frames spanning several chunks")
    return pl.pallas_call(
        functools.partial(_attn_stream_kernel, tq=tq),
        grid=(nseq,),
        in_specs=[
            pl.BlockSpec((tq, MQK), lambda b: (b, 0)),
            pl.BlockSpec((1, 1, past, MLA_KV_LORA), lambda b: (layer, b, 0, 0)),
            pl.BlockSpec((1, 1, past, MLA_ROPE), lambda b: (layer, b, 0, 0)),
            pl.BlockSpec((tq, MLA_KV_LORA), lambda b: (b, 0)),
            pl.BlockSpec((tq, HEAD_BLOCK), lambda b: (b, 0)),
            _const_spec((MLA_KV_LORA, MQK)),
            _const_spec((MLA_KV_LORA, MVW)),
            _const_spec((MLA_ROPE, HEAD_BLOCK)),
        ],
        out_specs=pl.BlockSpec((tq, MVW), lambda b: (b, 0)),
        out_shape=jax.ShapeDtypeStruct((t, MVW), BF16),
        compiler_params=_params(("arbitrary",)),
        name="mla_attn_stream",
    )(qh, cache_ckv, cache_kpe, ckv_new, kpe_new, w_k, w_v, place)


def _prep_weights(w_in, w_gk2, b_gk2, w_uq, w_ukv):
    s = np.cumsum([0, GQK, GQK, GVW, GLA_GATE_RANK, GVW, MLA_Q_LORA, MLA_KV_LORA, MLA_ROPE])
    gq, gk, gv, glr, gog, cq, ckv, kpe = [w_in[:, :, s[i]:s[i + 1]] for i in range(8)]
    z = lambda w: jnp.zeros((DEPTH, D_MODEL, w), w_in.dtype)
    w_in_p = jnp.concatenate(
        [gq, gk, gv, gog, cq, ckv, z(MLA_NOPE), kpe, z(HEAD_BLOCK - MLA_NOPE - MLA_ROPE),
         glr, z(LANES - GLA_GATE_RANK)], axis=-1).astype(BF16)
    w_gk_p = jnp.concatenate(
        [w_gk2, jnp.zeros((DEPTH, LANES - GLA_GATE_RANK, GQK), w_gk2.dtype)], axis=1).astype(BF16)
    b_gk_p = b_gk2.reshape(DEPTH, 1, GQK)
    uq = w_uq.reshape(DEPTH, MLA_Q_LORA, MLA_HEADS, MLA_NOPE + MLA_ROPE)
    uq = jnp.pad(uq, ((0, 0), (0, 0), (0, 0), (0, HEAD_BLOCK - MLA_NOPE - MLA_ROPE)))
    w_uq_p = uq.reshape(DEPTH, MLA_Q_LORA, MQK).astype(BF16)
    ukv = w_ukv.reshape(DEPTH, MLA_KV_LORA, MLA_HEADS, MLA_NOPE + MLA_V)
    uk = jnp.pad(ukv[..., :MLA_NOPE], ((0, 0), (0, 0), (0, 0), (0, HEAD_BLOCK - MLA_NOPE)))
    w_k_p = uk.reshape(DEPTH, MLA_KV_LORA, MQK).astype(BF16)
    w_v_p = ukv[..., MLA_NOPE:].reshape(DEPTH, MLA_KV_LORA, MVW).astype(BF16)
    return w_in_p, w_gk_p, b_gk_p, w_uq_p, w_k_p, w_v_p


def _rope_tables(pos, reps):
    half = MLA_ROPE // 2
    inv = ROPE_BASE ** (-jnp.arange(half, dtype=F32) / half)
    ang = pos.astype(F32)[:, None] * inv[None, :]
    cos, sin = jnp.cos(ang), jnp.sin(ang)
    n = pos.shape[0]
    one, zero = jnp.ones((n, MLA_NOPE), F32), jnp.zeros((n, MLA_NOPE), F32)
    pad = jnp.zeros((n, HEAD_BLOCK - MLA_NOPE - MLA_ROPE), F32)
    c = jnp.concatenate([one, cos, cos, pad], axis=1)
    s = jnp.concatenate([zero, -sin, sin, pad], axis=1)
    q_scale = MLA_SCALE * LOG2E
    tabs = (c * q_scale, s * q_scale, c, s)
    return tuple(jnp.tile(a, (reps, 1)) for a in tabs)


def _kpe_placement():
    p = np.zeros((MLA_ROPE, HEAD_BLOCK), np.float32)
    p[np.arange(MLA_ROPE), MLA_NOPE + np.arange(MLA_ROPE)] = 1.0
    return jnp.asarray(p, BF16)


def _trunk(x, mods, wts, tabs, *, nseq, seq_len, tm, nb, gla_tl, gla_chunk, s0, stream=None):
    tps = max(seq_len // tm, 1) if nb == 1 else 1
    ckvs, kpes, states = [], [], []
    for l in range(DEPTH):
        m = mods[l]
        x = _ffn_call(x, m[0:3], wts["ln_g"][l, 0:1], wts["ffn1_in"][l], wts["ffn1_out"][l],
                      tm=tm, nb=nb, tps=tps)
        gq, gk, gv, gog, lg, qh, kh, vh, ckv, kpe = _mixin_call(
            x, m[3:5], wts["ln_g"][l, 1:2], wts["w_in"][l], wts["w_gk"][l], wts["b_gk"][l],
            wts["qg"][l:l + 1], wts["w_uq"][l], wts["kvg"][l:l + 1], wts["w_k"][l], wts["w_v"][l],
            tabs, tm=tm, nb=nb, tps=tps)
        og, st = _gla_call(gq, gk, gv, gog, lg, s0[l], wts["ng"][l:l + 1],
                           nseq=nseq, seq_len=seq_len, tl=gla_tl, chunk=gla_chunk)
        if stream is None:
            om = _attn_prompt_call(qh, kh, vh, nseq=nseq, seq_len=seq_len, tq=min(256, seq_len))
        else:
            cache_ckv, cache_kpe, place = stream
            om = _attn_stream_call(qh, cache_ckv, cache_kpe, ckv, kpe, wts["w_k"][l], wts["w_v"][l],
                                   place, l, nseq=nseq, tq=seq_len, past=cache_ckv.shape[2])
        x = _mix_ffn_call(x, og, om, m[5:9], wts["w_out"][l], wts["ln_g"][l, 2:3], wts["ffn2_in"][l],
                          wts["ffn2_out"][l], wts["final_g"], tm=tm, nb=nb, tps=tps,
                          final_norm=(l == DEPTH - 1))
        ckvs.append(ckv)
        kpes.append(kpe[:, MLA_NOPE:MLA_NOPE + MLA_ROPE])
        states.append(st)
    return x, jnp.stack(ckvs), jnp.stack(kpes), jnp.stack(states)


def kernel(x_prompt, x_sample, cache_ckv, cache_kpe, state_gla, c_prompt, c_sample, ln_g, w_ada, b_ada, w_ffn1_in, w_ffn1_out, w_ffn2_in, w_ffn2_out, w_in, w_gk2, b_gk2, gla_norm_g, mla_qnorm_g, w_uq, mla_kvnorm_g, w_ukv, w_out, final_g):
    bsz, seq, _ = x_prompt.shape
    dbs, dseq, _ = x_sample.shape
    past = cache_ckv.shape[2]

    w_in_p, w_gk_p, b_gk_p, w_uq_p, w_k_p, w_v_p = _prep_weights(w_in, w_gk2, b_gk2, w_uq, w_ukv)
    wts = {
        "ln_g": ln_g, "final_g": final_g.reshape(1, D_MODEL),
        "ffn1_in": w_ffn1_in.astype(BF16), "ffn1_out": w_ffn1_out.astype(BF16),
        "ffn2_in": w_ffn2_in.astype(BF16), "ffn2_out": w_ffn2_out.astype(BF16),
        "w_in": w_in_p, "w_gk": w_gk_p, "b_gk": b_gk_p, "qg": mla_qnorm_g, "w_uq": w_uq_p,
        "kvg": mla_kvnorm_g, "w_k": w_k_p, "w_v": w_v_p, "ng": gla_norm_g,
        "w_out": w_out.astype(BF16),
    }
    mods = _ada_call(jnp.concatenate([c_prompt, c_sample], axis=0), w_ada, b_ada)
    mods_p = mods[:, :, :bsz].reshape(DEPTH, N_MOD, bsz, 1, D_MODEL)
    mods_s = mods[:, :, bsz:].reshape(DEPTH, N_MOD, dbs, 1, D_MODEL)

    tm_p = min(512, seq)
    y_p, ckv_p, kpe_p, gla_p = _trunk(
        x_prompt.reshape(bsz * seq, D_MODEL), mods_p, wts, _rope_tables(jnp.arange(seq), 1),
        nseq=bsz, seq_len=seq, tm=tm_p, nb=1, gla_tl=min(256, seq), gla_chunk=CHUNK,
        s0=jnp.zeros((DEPTH, bsz, GLA_HEADS, GLA_DK, GLA_DV), F32))

    y_s, ckv_s, kpe_s, gla_s = _trunk(
        x_sample.reshape(dbs * dseq, D_MODEL), mods_s, wts,
        _rope_tables(past + jnp.arange(dseq), dbs),
        nseq=dbs, seq_len=dseq, tm=dbs * dseq, nb=dbs, gla_tl=dseq, gla_chunk=dseq,
        s0=state_gla.astype(F32), stream=(cache_ckv, cache_kpe, _kpe_placement()))

    return (y_p.reshape(bsz, seq, D_MODEL), y_s.reshape(dbs, dseq, D_MODEL),
            ckv_p.reshape(DEPTH, bsz, seq, MLA_KV_LORA), kpe_p.reshape(DEPTH, bsz, seq, MLA_ROPE),
            gla_p,
            ckv_s.reshape(DEPTH, dbs, dseq, MLA_KV_LORA), kpe_s.reshape(DEPTH, dbs, dseq, MLA_ROPE),
            gla_s)
```

```python
import functools
import math

import numpy as np
import jax
import jax.numpy as jnp
from jax import lax
from jax.experimental import pallas as pl
from jax.experimental.pallas import tpu as pltpu

F32 = jnp.float32
BF16 = jnp.bfloat16

D_MODEL = 1024
DEPTH = 4
CHUNK = 64
EPS = 1e-6
GLA_HEADS = 4
GLA_DK = 64
GLA_DV = 128
GLA_GATE_RANK = 16
GLA_GATE_NORM = 16.0
GLA_SUB = 16
MLA_HEADS = 8
MLA_Q_LORA = 384
MLA_KV_LORA = 256
MLA_NOPE = 64
MLA_ROPE = 32
MLA_V = 64
MLA_SCALE = 1.0 / math.sqrt(MLA_NOPE + MLA_ROPE)
LOG2E = math.log2(math.e)
ROPE_BASE = 10000.0
D_FF = 2816
N_MOD = 9
LANES = 128
HEAD_BLOCK = 128
NEG_BIG = -1e30
SCORE_SAFE = 60.0
NRM_ROWS = 8

GQK = GLA_HEADS * GLA_DK
GVW = GLA_HEADS * GLA_DV
MQK = MLA_HEADS * HEAD_BLOCK
MVW = MLA_HEADS * MLA_V
O_GQ, O_GK, O_GV = 0, GQK, 2 * GQK
O_GLR = O_GV + GVW
O_KPE_COL = O_GLR + GLA_GATE_RANK + GVW + MLA_Q_LORA + MLA_KV_LORA
O_KPE_BLK = (O_KPE_COL // LANES) * LANES
IN_COLS = O_KPE_COL + MLA_ROPE
IN_W = -(-IN_COLS // LANES) * LANES
FFN_TF = 256
VMEM_LIMIT = 56 * 1024 * 1024


def _dot(a, b):
    return jnp.dot(a, b, preferred_element_type=F32)


def _dot_nt(a, b):
    return lax.dot_general(a, b, (((1,), (1,)), ((), ())), preferred_element_type=F32)


def _dot_tn(a, b):
    return lax.dot_general(a, b, (((0,), (0,)), ((), ())), preferred_element_type=F32)


def _rms(x, g):
    ms = jnp.mean(x * x, axis=-1, keepdims=True)
    return x * lax.rsqrt(ms + EPS) * g


def _modulate(x, g, sh, sc, nb):
    tm, d = x.shape
    y = _rms(x, g)
    if nb == 1:
        return y * (1.0 + sc[0]) + sh[0]
    y3 = y.reshape(nb, tm // nb, d)
    return (y3 * (1.0 + sc) + sh).reshape(tm, d)


def _gated_residual(x, y, coef, nb):
    tm, d = x.shape
    if nb == 1:
        return x + coef[0] * y
    return x + (coef * y.reshape(nb, tm // nb, d)).reshape(tm, d)


def _const_spec(shape):
    nd = len(shape)
    return pl.BlockSpec(shape, lambda *_: (0,) * nd, pipeline_mode=pl.Buffered(1))


def _params(sem):
    return pltpu.CompilerParams(dimension_semantics=sem, vmem_limit_bytes=VMEM_LIMIT)


def _ada_kernel(c_ref, w_ref, b_ref, o_ref):
    c = c_ref[...]
    sc = (c * jax.nn.sigmoid(c)).astype(BF16)
    o_ref[0, 0] = _dot(sc, w_ref[0].astype(BF16)) + b_ref[0, 0]


def _ada_call(c_all, w_ada, b_ada):
    nrow = c_all.shape[0]
    b4 = b_ada.reshape(DEPTH, N_MOD, 1, D_MODEL)
    return pl.pallas_call(
        _ada_kernel,
        grid=(DEPTH, N_MOD),
        in_specs=[
            pl.BlockSpec((nrow, D_MODEL), lambda l, j: (0, 0)),
            pl.BlockSpec((1, D_MODEL, D_MODEL), lambda l, j: (l, 0, j)),
            pl.BlockSpec((1, 1, 1, D_MODEL), lambda l, j: (l, j, 0, 0)),
        ],
        out_specs=pl.BlockSpec((1, 1, nrow, D_MODEL), lambda l, j: (l, j, 0, 0)),
        out_shape=jax.ShapeDtypeStruct((DEPTH, N_MOD, nrow, D_MODEL), F32),
        compiler_params=_params(("arbitrary", "arbitrary")),
        name="ada_mod",
    )(c_all, w_ada, b4)


def _ffn_body(x, sh, sc, gate, g, win_ref, wout_ref, h_ref, nb):
    n = _modulate(x, g, sh, sc, nb).astype(BF16)
    for c in range(D_FF // FFN_TF):
        lo = c * FFN_TF
        a = _dot(n, win_ref[:, lo:lo + FFN_TF])
        b = _dot(n, win_ref[:, D_FF + lo:D_FF + lo + FFN_TF])
        h_ref[:, lo:lo + FFN_TF] = (a * jax.nn.sigmoid(a) * b).astype(BF16)
    y = _dot(h_ref[...], wout_ref[...])
    return _gated_residual(x, y, 0.5 * (1.0 + gate), nb)


def _ffn_kernel(x_ref, mod_ref, g_ref, win_ref, wout_ref, o_ref, h_ref, *, nb):
    o_ref[...] = _ffn_body(x_ref[...], mod_ref[0], mod_ref[1], mod_ref[2], g_ref[...],
                           win_ref, wout_ref, h_ref, nb)


def _mix_ffn_kernel(x_ref, og_ref, om_ref, mod_ref, wmix_ref, g_ref, win_ref, wout_ref, fg_ref,
                    o_ref, h_ref, *, nb, final_norm):
    y = _dot(og_ref[...], wmix_ref[0:GVW, :]) + _dot(om_ref[...], wmix_ref[GVW:GVW + MVW, :])
    x = _gated_residual(x_ref[...], y, 1.0 + mod_ref[0], nb)
    x = _ffn_body(x, mod_ref[1], mod_ref[2], mod_ref[3], g_ref[...], win_ref, wout_ref, h_ref, nb)
    o_ref[...] = _rms(x, fg_ref[...]) if final_norm else x


def _ffn_call(x, mod3, g, w_in, w_out, *, tm, nb, tps):
    t = x.shape[0]
    return pl.pallas_call(
        functools.partial(_ffn_kernel, nb=nb),
        grid=(t // tm,),
        in_specs=[
            pl.BlockSpec((tm, D_MODEL), lambda i: (i, 0)),
            pl.BlockSpec((3, nb, 1, D_MODEL), lambda i: (0, i // tps, 0, 0)),
            _const_spec((1, D_MODEL)),
            _const_spec((D_MODEL, 2 * D_FF)),
            _const_spec((D_FF, D_MODEL)),
        ],
        out_specs=pl.BlockSpec((tm, D_MODEL), lambda i: (i, 0)),
        out_shape=jax.ShapeDtypeStruct((t, D_MODEL), F32),
        scratch_shapes=[pltpu.VMEM((tm, D_FF), BF16)],
        compiler_params=_params(("arbitrary",)),
        name="ffn",
    )(x, mod3, g, w_in, w_out)


def _mix_ffn_call(x, og, om, mod4, w_mix, g, w_in, w_out, fg, *, tm, nb, tps, final_norm):
    t = x.shape[0]
    tok = lambda w: pl.BlockSpec((tm, w), lambda i: (i, 0))
    return pl.pallas_call(
        functools.partial(_mix_ffn_kernel, nb=nb, final_norm=final_norm),
        grid=(t // tm,),
        in_specs=[
            tok(D_MODEL), tok(GVW), tok(MVW),
            pl.BlockSpec((4, nb, 1, D_MODEL), lambda i: (0, i // tps, 0, 0)),
            _const_spec((GVW + MVW, D_MODEL)),
            _const_spec((1, D_MODEL)),
            _const_spec((D_MODEL, 2 * D_FF)),
            _const_spec((D_FF, D_MODEL)),
            _const_spec((1, D_MODEL)),
        ],
        out_specs=tok(D_MODEL),
        out_shape=jax.ShapeDtypeStruct((t, D_MODEL), F32),
        scratch_shapes=[pltpu.VMEM((tm, D_FF), BF16)],
        compiler_params=_params(("arbitrary",)),
        name="mix_ffn",
    )(x, og, om, mod4, w_mix, g, w_in, w_out, fg)


def _swap_rope_halves(xb, lane):
    fwd = pltpu.roll(xb, LANES - MLA_ROPE // 2, axis=1)
    bwd = pltpu.roll(xb, MLA_ROPE // 2, axis=1)
    return jnp.where(lane < MLA_NOPE + MLA_ROPE // 2, fwd, bwd)


N_SHIFT = (GVW + MLA_Q_LORA + MLA_KV_LORA) // LANES


def _head_sq_norms(a, ones_ref):
    sq = _dot((a * a).astype(BF16), ones_ref[...])
    return jnp.max(sq, axis=0, keepdims=True)


def _mixin_kernel(x_ref, mod_ref, g_ref, win_ref, wgk_ref, bgk_ref, qg_ref, wuq_ref,
                  kvg_ref, wk_ref, wv_ref, ones_ref, cq_ref, sq_ref, ck_ref, sk_ref,
                  gq_ref, gk_ref, gv_ref, gog_ref, lg_ref, qh_ref, kh_ref, vh_ref,
                  ckv_ref, kpe_ref, nrm_ref, wsh_ref, *, nb):
    @pl.when(pl.program_id(0) == 0)
    def _():
        keep = LANES - GLA_GATE_RANK
        wl = lax.broadcasted_iota(jnp.int32, (D_MODEL, LANES), 1)
        prev = pltpu.roll(win_ref[:, O_GLR:O_GLR + LANES].astype(F32), keep, axis=1)
        for j in range(N_SHIFT):
            c0 = O_GLR + LANES * (j + 1)
            nxt = pltpu.roll(win_ref[:, c0:c0 + LANES].astype(F32), keep, axis=1)
            wsh_ref[:, LANES * j:LANES * (j + 1)] = jnp.where(wl < keep, prev, nxt).astype(BF16)
            prev = nxt

    x = x_ref[...]
    tm = x.shape[0]
    n = _modulate(x, g_ref[...], mod_ref[0], mod_ref[1], nb).astype(BF16)
    lane = lax.broadcasted_iota(jnp.int32, (tm, LANES), 1)
    h = _dot(n, win_ref[:, 0:O_GLR + LANES])
    gq_ref[...] = h[:, O_GQ:O_GQ + GQK] * (GLA_DK ** -0.5)
    gk_ref[...] = h[:, O_GK:O_GK + GQK]
    gv_ref[...] = h[:, O_GV:O_GV + GVW]
    z = _dot(h[:, O_GLR:O_GLR + LANES].astype(BF16), wgk_ref[...]) + bgk_ref[...]
    lsig = jnp.minimum(z, 0.0) - jnp.log1p(jnp.exp(-jnp.abs(z)))
    lg_ref[...] = lsig / GLA_GATE_NORM
    h2 = _dot(n, wsh_ref[...])
    gog_ref[...] = h2[:, 0:GVW]
    cqn = _rms(h2[:, GVW:GVW + MLA_Q_LORA], qg_ref[...]).astype(BF16)
    q = _dot(cqn, wuq_ref[...])
    cq, sq = cq_ref[...], sq_ref[...]
    qr = jnp.concatenate(
        [q[:, hd * HEAD_BLOCK:(hd + 1) * HEAD_BLOCK] * cq
         + _swap_rope_halves(q[:, hd * HEAD_BLOCK:(hd + 1) * HEAD_BLOCK], lane) * sq
         for hd in range(MLA_HEADS)], axis=1)
    qh_ref[...] = qr.astype(BF16)
    ckv = _rms(h2[:, GVW + MLA_Q_LORA:], kvg_ref[...])
    ckv_ref[...] = ckv
    kpe_lane0 = O_KPE_COL - O_KPE_BLK
    kb = pltpu.roll(_dot(n, win_ref[:, O_KPE_BLK:O_KPE_BLK + LANES]), MLA_NOPE - kpe_lane0, axis=1)
    kb = jnp.where((lane >= MLA_NOPE) & (lane < MLA_NOPE + MLA_ROPE), kb, 0.0)
    kpe = kb * ck_ref[...] + _swap_rope_halves(kb, lane) * sk_ref[...]
    kpe_ref[...] = pltpu.roll(kpe, LANES - MLA_NOPE, axis=1)[:, :MLA_ROPE]
    ckv16 = ckv.astype(BF16)
    kn = _dot(ckv16, wk_ref[...])
    kr = jnp.concatenate(
        [kn[:, hd * HEAD_BLOCK:(hd + 1) * HEAD_BLOCK] + kpe for hd in range(MLA_HEADS)], axis=1)
    kh_ref[...] = kr.astype(BF16)
    vh_ref[...] = _dot(ckv16, wv_ref[...]).astype(BF16)
    nrm_ref[0] = jnp.concatenate(
        [_head_sq_norms(qr, ones_ref), _head_sq_norms(kr, ones_ref),
         jnp.zeros((NRM_ROWS - 2, LANES), F32)], axis=0)


def _mixin_call(x, mod2, g, w_in, w_gk, b_gk, qg, w_uq, kvg, w_k, w_v, tabs, *, tm, nb, tps):
    t = x.shape[0]
    tok = lambda w: pl.BlockSpec((tm, w), lambda i: (i, 0))
    tab = pl.BlockSpec((tm, LANES), lambda i: (i % tps, 0))
    out_w = [(GQK, F32), (GQK, F32), (GVW, F32), (GVW, F32), (GQK, F32),
             (MQK, BF16), (MQK, BF16), (MVW, BF16), (MLA_KV_LORA, F32), (MLA_ROPE, F32)]
    nrm_spec = pl.BlockSpec((1, NRM_ROWS, LANES), lambda i: (i, 0, 0))
    nrm_shape = jax.ShapeDtypeStruct((t // tm, NRM_ROWS, LANES), F32)
    head_ones = np.zeros((MQK, LANES), np.float32)
    head_ones[np.arange(MQK), np.arange(MQK) // HEAD_BLOCK] = 1.0
    return pl.pallas_call(
        functools.partial(_mixin_kernel, nb=nb),
        grid=(t // tm,),
        in_specs=[
            tok(D_MODEL),
            pl.BlockSpec((2, nb, 1, D_MODEL), lambda i: (0, i // tps, 0, 0)),
            _const_spec((1, D_MODEL)),
            _const_spec((D_MODEL, IN_W)),
            _const_spec((LANES, GQK)),
            _const_spec((1, GQK)),
            _const_spec((1, MLA_Q_LORA)),
            _const_spec((MLA_Q_LORA, MQK)),
            _const_spec((1, MLA_KV_LORA)),
            _const_spec((MLA_KV_LORA, MQK)),
            _const_spec((MLA_KV_LORA, MVW)),
            _const_spec((MQK, LANES)),
            tab, tab, tab, tab,
        ],
        out_specs=[tok(w) for w, _ in out_w] + [nrm_spec],
        out_shape=[jax.ShapeDtypeStruct((t, w), dt) for w, dt in out_w] + [nrm_shape],
        scratch_shapes=[pltpu.VMEM((D_MODEL, N_SHIFT * LANES), BF16)],
        compiler_params=_params(("arbitrary",)),
        name="mixer_in",
    )(x, mod2, g, w_in, w_gk, b_gk, qg, w_uq, kvg, w_k, w_v, jnp.asarray(head_ones, BF16), *tabs)


def _dup_head(x, hd, lo_half):
    p = hd // 2
    xp = x[:, p * LANES:(p + 1) * LANES]
    xr = pltpu.roll(xp, GLA_DK, axis=1)
    if hd % 2 == 0:
        return jnp.where(lo_half, xp, xr)
    return jnp.where(lo_half, xr, xp)


def _gla_kernel(q_ref, k_ref, v_ref, og_ref, lg_ref, s0_ref, ng_ref, o_ref, sout_ref, st_ref,
                *, chunk, nch):
    j = pl.program_id(1)
    nsub = chunk // GLA_SUB
    tl = chunk * nch

    @pl.when(j == 0)
    def _():
        for hd in range(GLA_HEADS):
            st_ref[hd] = s0_ref[0, hd].T

    lg = lg_ref[...]
    ri = lax.broadcasted_iota(jnp.int32, (tl, tl), 0)
    ci = lax.broadcasted_iota(jnp.int32, (tl, tl), 1)
    tri = jnp.where((ci <= ri) & (ci // chunk == ri // chunk), 1.0, 0.0).astype(BF16)
    p0 = lg.astype(BF16)
    r1 = lg - p0.astype(F32)
    p1 = r1.astype(BF16)
    p2 = (r1 - p1.astype(F32)).astype(BF16)
    b_all = _dot(tri, p0) + _dot(tri, p1) + _dot(tri, p2)

    row = lax.broadcasted_iota(jnp.int32, (chunk, nsub * GLA_DK), 0)
    blk = lax.broadcasted_iota(jnp.int32, (chunk, nsub * GLA_DK), 1) // GLA_DK
    q_sel = (row // GLA_SUB) == blk
    k_sel = row < (blk + 1) * GLA_SUB
    lo_half = lax.broadcasted_iota(jnp.int32, (chunk, LANES), 1) < GLA_DK
    causal = (lax.broadcasted_iota(jnp.int32, (chunk, chunk), 1)
              <= lax.broadcasted_iota(jnp.int32, (chunk, chunk), 0))
    ng = ng_ref[...]

    def widen(a):
        if nsub == 1:
            return a[:, :GLA_DK]
        return jnp.concatenate([a] * (nsub // 2), axis=1)

    def v_of(c, hd):
        return v_ref[c * chunk:(c + 1) * chunk, hd * GLA_DV:(hd + 1) * GLA_DV].astype(BF16)

    incr = {}
    for c in range(nch):
        b = b_all[c * chunk:(c + 1) * chunk]
        kl = k_ref[c * chunk:(c + 1) * chunk, :] * jnp.exp(b[chunk - 1:chunk] - b)
        g_end = jnp.exp(b[chunk - 8:chunk])
        for hd in range(GLA_HEADS):
            kl_h = _dup_head(kl, hd, lo_half)[:, :GLA_DK]
            gl_h = _dup_head(g_end, hd, lo_half[0:8])[7:8, :GLA_DK]
            incr[c, hd] = (gl_h, _dot_tn(v_of(c, hd), kl_h.astype(BF16)))
    start = {}
    for hd in range(GLA_HEADS):
        st = st_ref[hd]
        for c in range(nch):
            start[c, hd] = st
            st = st * incr[c, hd][0] + incr[c, hd][1]
        st_ref[hd] = st

    for c in range(nch):
        r0 = c * chunk
        b = b_all[r0:r0 + chunk]
        q = q_ref[r0:r0 + chunk, :]
        k = k_ref[r0:r0 + chunk, :]
        r_own = jnp.broadcast_to(b[0:1], (chunk, GQK)) if nsub == 1 else jnp.concatenate(
            [jnp.broadcast_to(b[i * GLA_SUB:i * GLA_SUB + 1], (GLA_SUB, GQK)) for i in range(nsub)],
            axis=0)
        qt = q * jnp.exp(b - r_own)
        qe = q * jnp.exp(b)
        for hd in range(GLA_HEADS):
            v = v_of(c, hd)
            b2 = widen(_dup_head(b, hd, lo_half))
            ref_w = jnp.broadcast_to(b2[0:1], b2.shape)
            for i in range(1, nsub):
                ref_w = jnp.where(blk == i, jnp.broadcast_to(b2[i * GLA_SUB:i * GLA_SUB + 1], b2.shape),
                                  ref_w)
            kcat = widen(_dup_head(k, hd, lo_half)) * jnp.exp(jnp.where(k_sel, ref_w - b2, NEG_BIG))
            qcat = jnp.where(q_sel, widen(_dup_head(qt, hd, lo_half)), 0.0)
            att = _dot_nt(qcat.astype(BF16), kcat.astype(BF16))
            att = jnp.where(causal, att, 0.0)
            qe_h = _dup_head(qe, hd, lo_half)[:, :GLA_DK]
            o = _dot(att.astype(BF16), v) + _dot_nt(qe_h.astype(BF16), start[c, hd].astype(BF16))
            og = og_ref[r0:r0 + chunk, hd * GLA_DV:(hd + 1) * GLA_DV]
            o_ref[r0:r0 + chunk, hd * GLA_DV:(hd + 1) * GLA_DV] = (
                _rms(o, ng) * (og * jax.nn.sigmoid(og))).astype(BF16)

    @pl.when(j == pl.num_programs(1) - 1)
    def _():
        for hd in range(GLA_HEADS):
            sout_ref[0, hd] = st_ref[hd].T


def _gla_call(gq, gk, gv, gog, lg, s0, ng, *, nseq, seq_len, tl, chunk):
    t = gq.shape[0]
    tps = seq_len // tl
    tok = lambda w: pl.BlockSpec((tl, w), lambda b, j: (b * tps + j, 0))
    st_spec = pl.BlockSpec((1, GLA_HEADS, GLA_DK, GLA_DV), lambda b, j: (b, 0, 0, 0))
    return pl.pallas_call(
        functools.partial(_gla_kernel, chunk=chunk, nch=tl // chunk),
        grid=(nseq, tps),
        in_specs=[tok(GQK), tok(GQK), tok(GVW), tok(GVW), tok(GQK), st_spec,
                  pl.BlockSpec((1, GLA_DV), lambda b, j: (0, 0))],
        out_specs=[tok(GVW), st_spec],
        out_shape=[jax.ShapeDtypeStruct((t, GVW), BF16),
                   jax.ShapeDtypeStruct((nseq, GLA_HEADS, GLA_DK, GLA_DV), F32)],
        scratch_shapes=[pltpu.VMEM((GLA_HEADS, GLA_DV, GLA_DK), F32)],
        compiler_params=_params(("arbitrary", "arbitrary")),
        name="gla",
    )(gq, gk, gv, gog, lg, s0, ng)


def _softmax_pv(parts, vs, shift_by_max=True):
    if shift_by_max:
        m = functools.reduce(jnp.maximum, [jnp.max(s, axis=-1, keepdims=True) for s in parts])
    acc, den = None, None
    for s, v in zip(parts, vs):
        p = jnp.exp2(s - m) if shift_by_max else jnp.exp2(s)
        d = jnp.sum(p, axis=-1, keepdims=True)
        o = _dot(p.astype(BF16), v)
        acc = o if acc is None else acc + o
        den = d if den is None else den + d
    return acc / den


def _attn_prompt_kernel(q_ref, k_ref, v_ref, nrm_ref, o_ref, *, seq_len, tq):
    pair = pl.program_id(1)
    lane = lax.broadcasted_iota(jnp.int32, (tq, LANES), 1)
    qi = lax.broadcasted_iota(jnp.int32, (tq, tq), 0) // CHUNK
    ki = lax.broadcasted_iota(jnp.int32, (tq, tq), 1) // CHUNK
    diag_ok = ki <= qi

    def attend(shift_by_max):
        for i in range(seq_len // tq):
            r0 = i * tq
            outs = []
            for hh in range(2):
                cs = slice(hh * HEAD_BLOCK, (hh + 1) * HEAD_BLOCK)
                q = q_ref[r0:r0 + tq, cs]
                s_d = jnp.where(diag_ok, _dot_nt(q, k_ref[r0:r0 + tq, cs]), NEG_BIG)
                parts, vs = [s_d], [v_ref[r0:r0 + tq, :]]
                if i > 0:
                    parts.append(_dot_nt(q, k_ref[0:r0, cs]))
                    vs.append(v_ref[0:r0, :])
                outs.append(_softmax_pv(parts, vs, shift_by_max))
            o_ref[r0:r0 + tq, :] = jnp.where(lane < MLA_V, outs[0], outs[1]).astype(BF16)

    nrm = jnp.max(nrm_ref[...], axis=0)
    sq_bound = nrm[0:1] * nrm[1:2]
    head_lane = lax.broadcasted_iota(jnp.int32, (1, LANES), 1) // 2
    safe = jnp.max(jnp.where(head_lane == pair, sq_bound, 0.0)) <= SCORE_SAFE * SCORE_SAFE
    pl.when(safe)(functools.partial(attend, False))
    pl.when(jnp.logical_not(safe))(functools.partial(attend, True))


def _attn_prompt_call(qh, kh, vh, nrm, *, nseq, seq_len, tq):
    t = qh.shape[0]
    npair = MLA_HEADS // 2
    tps = nrm.shape[0] // nseq
    return pl.pallas_call(
        functools.partial(_attn_prompt_kernel, seq_len=seq_len, tq=tq),
        grid=(nseq, npair),
        in_specs=[
            pl.BlockSpec((seq_len, 2 * HEAD_BLOCK), lambda b, p: (b, p)),
            pl.BlockSpec((seq_len, 2 * HEAD_BLOCK), lambda b, p: (b, p)),
            pl.BlockSpec((seq_len, 2 * MLA_V), lambda b, p: (b, p)),
            pl.BlockSpec((tps, NRM_ROWS, LANES), lambda b, p: (b, 0, 0)),
        ],
        out_specs=pl.BlockSpec((seq_len, 2 * MLA_V), lambda b, p: (b, p)),
        out_shape=jax.ShapeDtypeStruct((t, MVW), BF16),
        compiler_params=_params(("arbitrary", "arbitrary")),
        name="mla_attn_prompt",
    )(qh, kh, vh, nrm)


def _attn_stream_kernel(q_ref, ckc_ref, kpc_ref, ckn_ref, kpn_ref, wk_ref, wv_ref, place_ref,
                        o_ref, *, tq):
    q = q_ref[...]
    ckc = ckc_ref[0, 0].astype(BF16)
    kpc = _dot(kpc_ref[0, 0].astype(BF16), place_ref[...]).astype(BF16)
    ckn = ckn_ref[...].astype(BF16)
    kpn = _dot(kpn_ref[...].astype(BF16), place_ref[...]).astype(BF16)
    blocks = [slice(hd * HEAD_BLOCK, (hd + 1) * HEAD_BLOCK) for hd in range(MLA_HEADS)]
    q_abs = jnp.concatenate([_dot_nt(q[:, bs], wk_ref[:, bs]) for bs in blocks], axis=0).astype(BF16)
    q_blk = jnp.concatenate([q[:, bs] for bs in blocks], axis=0)
    s_c = _dot_nt(q_abs, ckc) + _dot_nt(q_blk, kpc)
    s_n = _dot_nt(q_abs, ckn) + _dot_nt(q_blk, kpn)
    lat = _softmax_pv([s_c, s_n], [ckc, ckn]).astype(BF16)
    lane = lax.broadcasted_iota(jnp.int32, (tq, LANES), 1)
    for p in range(MLA_HEADS // 2):
        wv = wv_ref[:, p * LANES:(p + 1) * LANES]
        even = _dot(lat[(2 * p) * tq:(2 * p + 1) * tq], wv)
        odd = _dot(lat[(2 * p + 1) * tq:(2 * p + 2) * tq], wv)
        o_ref[:, p * LANES:(p + 1) * LANES] = jnp.where(lane < MLA_V, even, odd).astype(BF16)


def _attn_stream_call(qh, cache_ckv, cache_kpe, ckv_new, kpe_new, w_k, w_v, place, layer,
                      *, nseq, tq, past):
    t = qh.shape[0]
    qpos = past + np.arange(tq)
    if past % CHUNK or not ((qpos[None, :] // CHUNK) <= (qpos[:, None] // CHUNK)).all():
        raise NotImplementedError("new frames spanning several chunks")
    return pl.pallas_call(
        functools.partial(_attn_stream_kernel, tq=tq),
        grid=(nseq,),
        in_specs=[
            pl.BlockSpec((tq, MQK), lambda b: (b, 0)),
            pl.BlockSpec((1, 1, past, MLA_KV_LORA), lambda b: (layer, b, 0, 0)),
            pl.BlockSpec((1, 1, past, MLA_ROPE), lambda b: (layer, b, 0, 0)),
            pl.BlockSpec((tq, MLA_KV_LORA), lambda b: (b, 0)),
            pl.BlockSpec((tq, MLA_ROPE), lambda b: (b, 0)),
            _const_spec((MLA_KV_LORA, MQK)),
            _const_spec((MLA_KV_LORA, MVW)),
            _const_spec((MLA_ROPE, HEAD_BLOCK)),
        ],
        out_specs=pl.BlockSpec((tq, MVW), lambda b: (b, 0)),
        out_shape=jax.ShapeDtypeStruct((t, MVW), BF16),
        compiler_params=_params(("arbitrary",)),
        name="mla_attn_stream",
    )(qh, cache_ckv, cache_kpe, ckv_new, kpe_new, w_k, w_v, place)


def _prep_weights(w_in, w_gk2, b_gk2, w_uq, w_ukv):
    w_in_p = jnp.pad(w_in.astype(BF16), ((0, 0), (0, 0), (0, IN_W - IN_COLS)))
    w_gk_p = jnp.pad(w_gk2.astype(BF16), ((0, 0), (0, LANES - GLA_GATE_RANK), (0, 0)))
    b_gk_p = b_gk2.reshape(DEPTH, 1, GQK)
    uq = w_uq.astype(BF16).reshape(DEPTH, MLA_Q_LORA, MLA_HEADS, MLA_NOPE + MLA_ROPE)
    uq = jnp.pad(uq, ((0, 0), (0, 0), (0, 0), (0, HEAD_BLOCK - MLA_NOPE - MLA_ROPE)))
    w_uq_p = uq.reshape(DEPTH, MLA_Q_LORA, MQK)
    ukv = w_ukv.astype(BF16).reshape(DEPTH, MLA_KV_LORA, MLA_HEADS, MLA_NOPE + MLA_V)
    uk = jnp.pad(ukv[..., :MLA_NOPE], ((0, 0), (0, 0), (0, 0), (0, HEAD_BLOCK - MLA_NOPE)))
    w_k_p = uk.reshape(DEPTH, MLA_KV_LORA, MQK)
    w_v_p = ukv[..., MLA_NOPE:].reshape(DEPTH, MLA_KV_LORA, MVW)
    return w_in_p, w_gk_p, b_gk_p, w_uq_p, w_k_p, w_v_p


def _rope_tables(pos, reps):
    half = MLA_ROPE // 2
    inv = ROPE_BASE ** (-jnp.arange(half, dtype=F32) / half)
    ang = pos.astype(F32)[:, None] * inv[None, :]
    cos, sin = jnp.cos(ang), jnp.sin(ang)
    n = pos.shape[0]
    one, zero = jnp.ones((n, MLA_NOPE), F32), jnp.zeros((n, MLA_NOPE), F32)
    pad = jnp.zeros((n, HEAD_BLOCK - MLA_NOPE - MLA_ROPE), F32)
    c = jnp.concatenate([one, cos, cos, pad], axis=1)
    s = jnp.concatenate([zero, -sin, sin, pad], axis=1)
    q_scale = MLA_SCALE * LOG2E
    tabs = (c * q_scale, s * q_scale, c, s)
    return tuple(jnp.tile(a, (reps, 1)) for a in tabs)


def _kpe_placement():
    p = np.zeros((MLA_ROPE, HEAD_BLOCK), np.float32)
    p[np.arange(MLA_ROPE), MLA_NOPE + np.arange(MLA_ROPE)] = 1.0
    return jnp.asarray(p, BF16)


def _trunk(x, mods, wts, tabs, *, nseq, seq_len, tm, nb, gla_tl, gla_chunk, s0, stream=None):
    tps = max(seq_len // tm, 1) if nb == 1 else 1
    ckvs, kpes, states = [], [], []
    for l in range(DEPTH):
        m = mods[l]
        x = _ffn_call(x, m[0:3], wts["ln_g"][l, 0:1], wts["ffn1_in"][l], wts["ffn1_out"][l],
                      tm=tm, nb=nb, tps=tps)
        gq, gk, gv, gog, lg, qh, kh, vh, ckv, kpe, nrm = _mixin_call(
            x, m[3:5], wts["ln_g"][l, 1:2], wts["w_in"][l], wts["w_gk"][l], wts["b_gk"][l],
            wts["qg"][l:l + 1], wts["w_uq"][l], wts["kvg"][l:l + 1], wts["w_k"][l], wts["w_v"][l],
            tabs, tm=tm, nb=nb, tps=tps)
        og, st = _gla_call(gq, gk, gv, gog, lg, s0[l], wts["ng"][l:l + 1],
                           nseq=nseq, seq_len=seq_len, tl=gla_tl, chunk=gla_chunk)
        if stream is None:
            om = _attn_prompt_call(qh, kh, vh, nrm, nseq=nseq, seq_len=seq_len, tq=min(256, seq_len))
        else:
            cache_ckv, cache_kpe, place = stream
            om = _attn_stream_call(qh, cache_ckv, cache_kpe, ckv, kpe, wts["w_k"][l], wts["w_v"][l],
                                   place, l, nseq=nseq, tq=seq_len, past=cache_ckv.shape[2])
        x = _mix_ffn_call(x, og, om, m[5:9], wts["w_out"][l], wts["ln_g"][l, 2:3], wts["ffn2_in"][l],
                          wts["ffn2_out"][l], wts["final_g"], tm=tm, nb=nb, tps=tps,
                          final_norm=(l == DEPTH - 1))
        ckvs.append(ckv)
        kpes.append(kpe)
        states.append(st)
    return x, jnp.stack(ckvs), jnp.stack(kpes), jnp.stack(states)


def kernel(x_prompt, x_sample, cache_ckv, cache_kpe, state_gla, c_prompt, c_sample, ln_g, w_ada, b_ada, w_ffn1_in, w_ffn1_out, w_ffn2_in, w_ffn2_out, w_in, w_gk2, b_gk2, gla_norm_g, mla_qnorm_g, w_uq, mla_kvnorm_g, w_ukv, w_out, final_g):
    bsz, seq, _ = x_prompt.shape
    dbs, dseq, _ = x_sample.shape
    past = cache_ckv.shape[2]

    w_in_p, w_gk_p, b_gk_p, w_uq_p, w_k_p, w_v_p = _prep_weights(w_in, w_gk2, b_gk2, w_uq, w_ukv)
    wts = {
        "ln_g": ln_g, "final_g": final_g.reshape(1, D_MODEL),
        "ffn1_in": w_ffn1_in.astype(BF16), "ffn1_out": w_ffn1_out.astype(BF16),
        "ffn2_in": w_ffn2_in.astype(BF16), "ffn2_out": w_ffn2_out.astype(BF16),
        "w_in": w_in_p, "w_gk": w_gk_p, "b_gk": b_gk_p, "qg": mla_qnorm_g, "w_uq": w_uq_p,
        "kvg": mla_kvnorm_g, "w_k": w_k_p, "w_v": w_v_p, "ng": gla_norm_g,
        "w_out": w_out.astype(BF16),
    }
    mods = _ada_call(jnp.concatenate([c_prompt, c_sample], axis=0), w_ada, b_ada)
    mods_p = mods[:, :, :bsz].reshape(DEPTH, N_MOD, bsz, 1, D_MODEL)
    mods_s = mods[:, :, bsz:].reshape(DEPTH, N_MOD, dbs, 1, D_MODEL)

    tm_p = min(512, seq)
    y_p, ckv_p, kpe_p, gla_p = _trunk(
        x_prompt.reshape(bsz * seq, D_MODEL), mods_p, wts, _rope_tables(jnp.arange(seq), 1),
        nseq=bsz, seq_len=seq, tm=tm_p, nb=1, gla_tl=min(256, seq), gla_chunk=CHUNK,
        s0=jnp.zeros((DEPTH, bsz, GLA_HEADS, GLA_DK, GLA_DV), F32))

    y_s, ckv_s, kpe_s, gla_s = _trunk(
        x_sample.reshape(dbs * dseq, D_MODEL), mods_s, wts,
        _rope_tables(past + jnp.arange(dseq), dbs),
        nseq=dbs, seq_len=dseq, tm=dbs * dseq, nb=dbs, gla_tl=dseq, gla_chunk=dseq,
        s0=state_gla.astype(F32), stream=(cache_ckv, cache_kpe, _kpe_placement()))

    return (y_p.reshape(bsz, seq, D_MODEL), y_s.reshape(dbs, dseq, D_MODEL),
            ckv_p.reshape(DEPTH, bsz, seq, MLA_KV_LORA), kpe_p.reshape(DEPTH, bsz, seq, MLA_ROPE),
            gla_p,
            ckv_s.reshape(DEPTH, dbs, dseq, MLA_KV_LORA), kpe_s.reshape(DEPTH, dbs, dseq, MLA_ROPE),
            gla_s)
```

```python
import functools
import math

import numpy as np
import jax
import jax.numpy as jnp
from jax import lax
from jax.experimental import pallas as pl
from jax.experimental.pallas import tpu as pltpu

F32 = jnp.float32
BF16 = jnp.bfloat16

D_MODEL = 1024
DEPTH = 4
CHUNK = 64
EPS = 1e-6
GLA_HEADS = 4
GLA_DK = 64
GLA_DV = 128
GLA_GATE_RANK = 16
GLA_GATE_NORM = 16.0
GLA_SUB = 16
EXP_CLAMP = 80.0
MLA_HEADS = 8
MLA_Q_LORA = 384
MLA_KV_LORA = 256
MLA_NOPE = 64
MLA_ROPE = 32
MLA_V = 64
MLA_SCALE = 1.0 / math.sqrt(MLA_NOPE + MLA_ROPE)
LOG2E = math.log2(math.e)
ROPE_BASE = 10000.0
D_FF = 2816
N_MOD = 9
LANES = 128
HEAD_BLOCK = 128
NEG_BIG = -1e30
SCORE_SAFE = 60.0
NRM_ROWS = 8

GQK = GLA_HEADS * GLA_DK
GVW = GLA_HEADS * GLA_DV
MQK = MLA_HEADS * HEAD_BLOCK
MVW = MLA_HEADS * MLA_V
O_GQ, O_GK, O_GV = 0, GQK, 2 * GQK
O_GLR = O_GV + GVW
O_KPE_COL = O_GLR + GLA_GATE_RANK + GVW + MLA_Q_LORA + MLA_KV_LORA
O_KPE_BLK = (O_KPE_COL // LANES) * LANES
IN_COLS = O_KPE_COL + MLA_ROPE
IN_W = -(-IN_COLS // LANES) * LANES
FFN_TF = 256
FFN_TM = 1024
GLA_TL = 512
CUMSUM_ROWS = 256
VMEM_LIMIT = 56 * 1024 * 1024


def _dot(a, b):
    return jnp.dot(a, b, preferred_element_type=F32)


def _dot_nt(a, b):
    return lax.dot_general(a, b, (((1,), (1,)), ((), ())), preferred_element_type=F32)


def _dot_tn(a, b):
    return lax.dot_general(a, b, (((0,), (0,)), ((), ())), preferred_element_type=F32)


def _rms(x, g):
    ms = jnp.mean(x * x, axis=-1, keepdims=True)
    return x * lax.rsqrt(ms + EPS) * g


def _modulate(x, g, sh, sc, nb):
    tm, d = x.shape
    y = _rms(x, g)
    if nb == 1:
        return y * (1.0 + sc[0]) + sh[0]
    y3 = y.reshape(nb, tm // nb, d)
    return (y3 * (1.0 + sc) + sh).reshape(tm, d)


def _gated_residual(x, y, coef, nb):
    tm, d = x.shape
    if nb == 1:
        return x + coef[0] * y
    return x + (coef * y.reshape(nb, tm // nb, d)).reshape(tm, d)


def _const_spec(shape):
    nd = len(shape)
    return pl.BlockSpec(shape, lambda *_: (0,) * nd, pipeline_mode=pl.Buffered(1))


def _layer_spec(shape, *lead):
    nd = len(shape)
    return pl.BlockSpec((None,) * len(lead) + tuple(shape), lambda *_: tuple(lead) + (0,) * nd,
                        pipeline_mode=pl.Buffered(1))


def _mod_spec(nmod, nb, tps, seq0, layer, blk):
    return pl.BlockSpec((None, nmod, nb, 1, D_MODEL), lambda i: (layer, blk, seq0 + i // tps, 0, 0))


def _params(sem):
    return pltpu.CompilerParams(dimension_semantics=sem, vmem_limit_bytes=VMEM_LIMIT)


def _ada_kernel(c_ref, w_ref, b_ref, o_ref):
    c = c_ref[...]
    sc = (c * jax.nn.sigmoid(c)).astype(BF16)
    o_ref[0, 0] = _dot(sc, w_ref[0].astype(BF16)) + b_ref[0, 0]


def _ada_call(c_all, w_ada, b_ada):
    nrow = c_all.shape[0]
    b4 = b_ada.reshape(DEPTH, N_MOD, 1, D_MODEL)
    return pl.pallas_call(
        _ada_kernel,
        grid=(DEPTH, N_MOD),
        in_specs=[
            pl.BlockSpec((nrow, D_MODEL), lambda l, j: (0, 0)),
            pl.BlockSpec((1, D_MODEL, D_MODEL), lambda l, j: (l, 0, j)),
            pl.BlockSpec((1, 1, 1, D_MODEL), lambda l, j: (l, j, 0, 0)),
        ],
        out_specs=pl.BlockSpec((1, 1, nrow, D_MODEL), lambda l, j: (l, j, 0, 0)),
        out_shape=jax.ShapeDtypeStruct((DEPTH, N_MOD, nrow, D_MODEL), F32),
        compiler_params=_params(("arbitrary", "arbitrary")),
        name="ada_mod",
    )(c_all, w_ada, b4)


def _ffn_body(x, sh, sc, gate, g, win_ref, wout_ref, h_ref, nb):
    n = _modulate(x, g, sh, sc, nb).astype(BF16)
    for c in range(D_FF // FFN_TF):
        lo = c * FFN_TF
        a = _dot(n, win_ref[:, lo:lo + FFN_TF])
        b = _dot(n, win_ref[:, D_FF + lo:D_FF + lo + FFN_TF])
        h_ref[:, lo:lo + FFN_TF] = (a * jax.nn.sigmoid(a) * b).astype(BF16)
    y = _dot(h_ref[...], wout_ref[...])
    return _gated_residual(x, y, 0.5 * (1.0 + gate), nb)


def _ffn_kernel(x_ref, mod_ref, g_ref, win_ref, wout_ref, o_ref, h_ref, *, nb):
    o_ref[...] = _ffn_body(x_ref[...], mod_ref[0], mod_ref[1], mod_ref[2], g_ref[...],
                           win_ref, wout_ref, h_ref, nb)


def _mix_ffn_kernel(x_ref, og_ref, om_ref, gmix_ref, mod_ref, wmix_ref, g_ref, win_ref, wout_ref,
                    fg_ref, o_ref, h_ref, *, nb, final_norm):
    y = _dot(og_ref[...], wmix_ref[0:GVW, :]) + _dot(om_ref[...], wmix_ref[GVW:GVW + MVW, :])
    x = _gated_residual(x_ref[...], y, 1.0 + gmix_ref[2], nb)
    x = _ffn_body(x, mod_ref[0], mod_ref[1], mod_ref[2], g_ref[...], win_ref, wout_ref, h_ref, nb)
    o_ref[...] = _rms(x, fg_ref[...]) if final_norm else x


def _ffn_call(x, mods, ln_g, w_in, w_out, layer, *, tm, nb, tps, seq0):
    t = x.shape[0]
    return pl.pallas_call(
        functools.partial(_ffn_kernel, nb=nb),
        grid=(t // tm,),
        in_specs=[
            pl.BlockSpec((tm, D_MODEL), lambda i: (i, 0)),
            _mod_spec(3, nb, tps, seq0, layer, 0),
            _layer_spec((1, D_MODEL), layer, 0),
            _layer_spec((D_MODEL, 2 * D_FF), layer),
            _layer_spec((D_FF, D_MODEL), layer),
        ],
        out_specs=pl.BlockSpec((tm, D_MODEL), lambda i: (i, 0)),
        out_shape=jax.ShapeDtypeStruct((t, D_MODEL), F32),
        scratch_shapes=[pltpu.VMEM((tm, D_FF), BF16)],
        compiler_params=_params(("arbitrary",)),
        name="ffn",
    )(x, mods, ln_g, w_in, w_out)


def _mix_ffn_call(x, og, om, mods, w_mix, ln_g, w_in, w_out, fg, layer, *, tm, nb, tps, seq0, final_norm):
    t = x.shape[0]
    tok = lambda w: pl.BlockSpec((tm, w), lambda i: (i, 0))
    return pl.pallas_call(
        functools.partial(_mix_ffn_kernel, nb=nb, final_norm=final_norm),
        grid=(t // tm,),
        in_specs=[
            tok(D_MODEL), tok(GVW), tok(MVW),
            _mod_spec(3, nb, tps, seq0, layer, 1),
            _mod_spec(3, nb, tps, seq0, layer, 2),
            _layer_spec((GVW + MVW, D_MODEL), layer),
            _layer_spec((1, D_MODEL), layer, 2),
            _layer_spec((D_MODEL, 2 * D_FF), layer),
            _layer_spec((D_FF, D_MODEL), layer),
            _const_spec((1, D_MODEL)),
        ],
        out_specs=tok(D_MODEL),
        out_shape=jax.ShapeDtypeStruct((t, D_MODEL), F32),
        scratch_shapes=[pltpu.VMEM((tm, D_FF), BF16)],
        compiler_params=_params(("arbitrary",)),
        name="mix_ffn",
    )(x, og, om, mods, mods, w_mix, ln_g, w_in, w_out, fg)


def _swap_rope_halves(xb, lane):
    fwd = pltpu.roll(xb, LANES - MLA_ROPE // 2, axis=1)
    bwd = pltpu.roll(xb, MLA_ROPE // 2, axis=1)
    return jnp.where(lane < MLA_NOPE + MLA_ROPE // 2, fwd, bwd)


N_SHIFT = (GVW + MLA_Q_LORA + MLA_KV_LORA) // LANES


def _head_sq_norms(a, ones_ref):
    sq = _dot((a * a).astype(BF16), ones_ref[...])
    return jnp.max(sq, axis=0, keepdims=True)


def _mixin_kernel(x_ref, mod_ref, g_ref, win_ref, wgk_ref, bgk_ref, qg_ref, wuq_ref,
                  kvg_ref, wk_ref, wv_ref, ones_ref, cq_ref, sq_ref, ck_ref, sk_ref,
                  gq_ref, gk_ref, gv_ref, gog_ref, lg_ref, qh_ref, kh_ref, vh_ref,
                  ckv_ref, kpe_ref, nrm_ref, wsh_ref, *, nb):
    @pl.when(pl.program_id(0) == 0)
    def _():
        keep = LANES - GLA_GATE_RANK
        wl = lax.broadcasted_iota(jnp.int32, (D_MODEL, LANES), 1)
        prev = pltpu.roll(win_ref[:, O_GLR:O_GLR + LANES].astype(F32), keep, axis=1)
        for j in range(N_SHIFT):
            c0 = O_GLR + LANES * (j + 1)
            nxt = pltpu.roll(win_ref[:, c0:c0 + LANES].astype(F32), keep, axis=1)
            wsh_ref[:, LANES * j:LANES * (j + 1)] = jnp.where(wl < keep, prev, nxt).astype(BF16)
            prev = nxt

    x = x_ref[...]
    tm = x.shape[0]
    n = _modulate(x, g_ref[...], mod_ref[0], mod_ref[1], nb).astype(BF16)
    lane = lax.broadcasted_iota(jnp.int32, (tm, LANES), 1)
    h = _dot(n, win_ref[:, 0:O_GLR + LANES])
    gq_ref[...] = h[:, O_GQ:O_GQ + GQK] * (GLA_DK ** -0.5)
    gk_ref[...] = h[:, O_GK:O_GK + GQK]
    gv_ref[...] = h[:, O_GV:O_GV + GVW]
    z = _dot(h[:, O_GLR:O_GLR + LANES].astype(BF16), wgk_ref[...]) + bgk_ref[...]
    lsig = jnp.minimum(z, 0.0) - jnp.log1p(jnp.exp(-jnp.abs(z)))
    lg_ref[...] = lsig / GLA_GATE_NORM
    h2 = _dot(n, wsh_ref[...])
    gog_ref[...] = h2[:, 0:GVW]
    cqn = _rms(h2[:, GVW:GVW + MLA_Q_LORA], qg_ref[...]).astype(BF16)
    q = _dot(cqn, wuq_ref[...])
    cq, sq = cq_ref[...], sq_ref[...]
    qr = jnp.concatenate(
        [q[:, hd * HEAD_BLOCK:(hd + 1) * HEAD_BLOCK] * cq
         + _swap_rope_halves(q[:, hd * HEAD_BLOCK:(hd + 1) * HEAD_BLOCK], lane) * sq
         for hd in range(MLA_HEADS)], axis=1)
    qh_ref[...] = qr.astype(BF16)
    ckv = _rms(h2[:, GVW + MLA_Q_LORA:], kvg_ref[...])
    ckv_ref[...] = ckv
    kpe_lane0 = O_KPE_COL - O_KPE_BLK
    kb = pltpu.roll(_dot(n, win_ref[:, O_KPE_BLK:O_KPE_BLK + LANES]), MLA_NOPE - kpe_lane0, axis=1)
    kb = jnp.where((lane >= MLA_NOPE) & (lane < MLA_NOPE + MLA_ROPE), kb, 0.0)
    kpe = kb * ck_ref[...] + _swap_rope_halves(kb, lane) * sk_ref[...]
    kpe_ref[...] = pltpu.roll(kpe, LANES - MLA_NOPE, axis=1)[:, :MLA_ROPE]
    ckv16 = ckv.astype(BF16)
    kn = _dot(ckv16, wk_ref[...])
    kr = jnp.concatenate(
        [kn[:, hd * HEAD_BLOCK:(hd + 1) * HEAD_BLOCK] + kpe for hd in range(MLA_HEADS)], axis=1)
    kh_ref[...] = kr.astype(BF16)
    vh_ref[...] = _dot(ckv16, wv_ref[...]).astype(BF16)
    nrm_ref[0] = jnp.concatenate(
        [_head_sq_norms(qr, ones_ref), _head_sq_norms(kr, ones_ref),
         jnp.zeros((NRM_ROWS - 2, LANES), F32)], axis=0)


def _mixin_call(x, mods, ln_g, w_in, w_gk, b_gk, qg, w_uq, kvg, w_k, w_v, tabs, layer, *, tm, nb, tps, seq0):
    t = x.shape[0]
    tok = lambda w: pl.BlockSpec((tm, w), lambda i: (i, 0))
    tab = pl.BlockSpec((tm, LANES), lambda i: (i % tps, 0))
    out_w = [(GQK, F32), (GQK, F32), (GVW, F32), (GVW, F32), (GQK, F32),
             (MQK, BF16), (MQK, BF16), (MVW, BF16), (MLA_KV_LORA, F32), (MLA_ROPE, F32)]
    nrm_spec = pl.BlockSpec((1, NRM_ROWS, LANES), lambda i: (i, 0, 0))
    nrm_shape = jax.ShapeDtypeStruct((t // tm, NRM_ROWS, LANES), F32)
    head_ones = np.zeros((MQK, LANES), np.float32)
    head_ones[np.arange(MQK), np.arange(MQK) // HEAD_BLOCK] = 1.0
    return pl.pallas_call(
        functools.partial(_mixin_kernel, nb=nb),
        grid=(t // tm,),
        in_specs=[
            tok(D_MODEL),
            _mod_spec(3, nb, tps, seq0, layer, 1),
            _layer_spec((1, D_MODEL), layer, 1),
            _layer_spec((D_MODEL, IN_W), layer),
            _layer_spec((LANES, GQK), layer),
            _layer_spec((1, GQK), layer),
            _layer_spec((1, MLA_Q_LORA), layer),
            _layer_spec((MLA_Q_LORA, MQK), layer),
            _layer_spec((1, MLA_KV_LORA), layer),
            _layer_spec((MLA_KV_LORA, MQK), layer),
            _layer_spec((MLA_KV_LORA, MVW), layer),
            _const_spec((MQK, LANES)),
            tab, tab, tab, tab,
        ],
        out_specs=[tok(w) for w, _ in out_w] + [nrm_spec],
        out_shape=[jax.ShapeDtypeStruct((t, w), dt) for w, dt in out_w] + [nrm_shape],
        scratch_shapes=[pltpu.VMEM((D_MODEL, N_SHIFT * LANES), BF16)],
        compiler_params=_params(("arbitrary",)),
        name="mixer_in",
    )(x, mods, ln_g, w_in, w_gk, b_gk, qg, w_uq, kvg, w_k, w_v, jnp.asarray(head_ones, BF16), *tabs)


def _gla_kernel(q_ref, k_ref, v_ref, og_ref, lg_ref, s0_ref, ng_ref, o_ref, sout_ref, st_ref,
                *, chunk, nch):
    j = pl.program_id(1)
    nsub = chunk // GLA_SUB
    tl = chunk * nch
    lane_head = lax.broadcasted_iota(jnp.int32, (chunk, GQK), 1) // GLA_DK

    def stack_heads(a):
        return jnp.concatenate(
            [jnp.where(lane_head == hd, a, 0.0) for hd in range(GLA_HEADS)], axis=0).astype(BF16)

    @pl.when(j == 0)
    def _():
        for hd in range(GLA_HEADS):
            s0 = s0_ref[0, hd]
            rows = [jnp.zeros((GLA_DK, GLA_DV), F32)] * GLA_HEADS
            rows[hd] = s0
            st_ref[hd * GLA_DV:(hd + 1) * GLA_DV, :] = jnp.concatenate(rows, axis=0).T

    grp = min(tl, CUMSUM_ROWS)
    ri = lax.broadcasted_iota(jnp.int32, (grp, grp), 0)
    ci = lax.broadcasted_iota(jnp.int32, (grp, grp), 1)
    tri = jnp.where((ci <= ri) & (ci // chunk == ri // chunk), 1.0, 0.0).astype(BF16)
    b_parts = []
    for g in range(tl // grp):
        lg = lg_ref[g * grp:(g + 1) * grp, :]
        p0 = lg.astype(BF16)
        r1 = lg - p0.astype(F32)
        p1 = r1.astype(BF16)
        p2 = (r1 - p1.astype(F32)).astype(BF16)
        b_parts.append(_dot(tri, p0) + _dot(tri, p1) + _dot(tri, p2))
    b_all = b_parts[0] if len(b_parts) == 1 else jnp.concatenate(b_parts, axis=0)

    rt = lax.broadcasted_iota(jnp.int32, (GLA_HEADS * chunk, nsub * chunk), 0) % chunk
    cc = lax.broadcasted_iota(jnp.int32, (GLA_HEADS * chunk, nsub * chunk), 1)
    keep = (cc // chunk == rt // GLA_SUB) & (cc % chunk <= rt)
    ng = ng_ref[...]

    def v_of(c):
        return v_ref[c * chunk:(c + 1) * chunk, :].astype(BF16)

    start = []
    st = st_ref[...]
    for c in range(nch):
        b = b_all[c * chunk:(c + 1) * chunk]
        kl = k_ref[c * chunk:(c + 1) * chunk, :] * jnp.exp(b[chunk - 1:chunk] - b)
        g_end = jnp.exp(b[chunk - 8:chunk])[7:8]
        start.append(st)
        st = st * g_end + _dot_tn(v_of(c), kl.astype(BF16))
    st_ref[...] = st

    for c in range(nch):
        r0 = c * chunk
        b = b_all[r0:r0 + chunk]
        q = q_ref[r0:r0 + chunk, :]
        k = k_ref[r0:r0 + chunk, :]
        r_own = jnp.broadcast_to(b[0:1], (chunk, GQK)) if nsub == 1 else jnp.concatenate(
            [jnp.broadcast_to(b[i * GLA_SUB:i * GLA_SUB + 1], (GLA_SUB, GQK)) for i in range(nsub)],
            axis=0)
        qt = q * jnp.exp(b - r_own)
        qe = q * jnp.exp(b)
        k_rel = [(k * jnp.exp(jnp.minimum(b[i * GLA_SUB:i * GLA_SUB + 1] - b, EXP_CLAMP))).astype(BF16)
                 for i in range(nsub)]
        att = _dot_nt(stack_heads(qt), jnp.concatenate(k_rel, axis=0))
        att = jnp.where(keep, att, 0.0).astype(BF16)
        v = v_of(c)
        o_all = (_dot(att, jnp.concatenate([v] * nsub, axis=0))
                 + _dot_nt(stack_heads(qe), start[c].astype(BF16)))
        for hd in range(GLA_HEADS):
            cols = slice(hd * GLA_DV, (hd + 1) * GLA_DV)
            o = o_all[hd * chunk:(hd + 1) * chunk, cols]
            og = og_ref[r0:r0 + chunk, cols]
            o_ref[r0:r0 + chunk, cols] = (_rms(o, ng) * (og * jax.nn.sigmoid(og))).astype(BF16)

    @pl.when(j == pl.num_programs(1) - 1)
    def _():
        for hd in range(GLA_HEADS):
            sout_ref[0, hd] = st_ref[hd * GLA_DV:(hd + 1) * GLA_DV, :].T[hd * GLA_DK:(hd + 1) * GLA_DK, :]


def _gla_call(gq, gk, gv, gog, lg, s0, ng, layer, *, nseq, seq_len, tl, chunk):
    t = gq.shape[0]
    tps = seq_len // tl
    tok = lambda w: pl.BlockSpec((tl, w), lambda b, j: (b * tps + j, 0))
    st_spec = pl.BlockSpec((1, GLA_HEADS, GLA_DK, GLA_DV), lambda b, j: (b, 0, 0, 0))
    s0_spec = pl.BlockSpec((None, 1, GLA_HEADS, GLA_DK, GLA_DV), lambda b, j: (layer, b, 0, 0, 0))
    return pl.pallas_call(
        functools.partial(_gla_kernel, chunk=chunk, nch=tl // chunk),
        grid=(nseq, tps),
        in_specs=[tok(GQK), tok(GQK), tok(GVW), tok(GVW), tok(GQK), s0_spec,
                  _layer_spec((1, GLA_DV), layer)],
        out_specs=[tok(GVW), st_spec],
        out_shape=[jax.ShapeDtypeStruct((t, GVW), BF16),
                   jax.ShapeDtypeStruct((nseq, GLA_HEADS, GLA_DK, GLA_DV), F32)],
        scratch_shapes=[pltpu.VMEM((GVW, GQK), F32)],
        compiler_params=_params(("arbitrary", "arbitrary")),
        name="gla",
    )(gq, gk, gv, gog, lg, s0, ng)


def _softmax_pv(parts, vs, shift_by_max=True):
    if shift_by_max:
        m = functools.reduce(jnp.maximum, [jnp.max(s, axis=-1, keepdims=True) for s in parts])
    acc, den = None, None
    for s, v in zip(parts, vs):
        p = jnp.exp2(s - m) if shift_by_max else jnp.exp2(s)
        d = jnp.sum(p, axis=-1, keepdims=True)
        o = _dot(p.astype(BF16), v)
        acc = o if acc is None else acc + o
        den = d if den is None else den + d
    return acc / den


def _attn_prompt_kernel(q_ref, k_ref, v_ref, nrm_ref, o_ref, *, seq_len, tq):
    pair = pl.program_id(1)
    lane = lax.broadcasted_iota(jnp.int32, (tq, LANES), 1)
    qi = lax.broadcasted_iota(jnp.int32, (tq, tq), 0) // CHUNK
    ki = lax.broadcasted_iota(jnp.int32, (tq, tq), 1) // CHUNK
    diag_ok = ki <= qi

    def attend(shift_by_max):
        for i in range(seq_len // tq):
            r0 = i * tq
            outs = []
            for hh in range(2):
                cs = slice(hh * HEAD_BLOCK, (hh + 1) * HEAD_BLOCK)
                q = q_ref[r0:r0 + tq, cs]
                s_d = jnp.where(diag_ok, _dot_nt(q, k_ref[r0:r0 + tq, cs]), NEG_BIG)
                parts, vs = [s_d], [v_ref[r0:r0 + tq, :]]
                if i > 0:
                    parts.append(_dot_nt(q, k_ref[0:r0, cs]))
                    vs.append(v_ref[0:r0, :])
                outs.append(_softmax_pv(parts, vs, shift_by_max))
            o_ref[r0:r0 + tq, :] = jnp.where(lane < MLA_V, outs[0], outs[1]).astype(BF16)

    nrm = jnp.max(nrm_ref[...], axis=0)
    sq_bound = nrm[0:1] * nrm[1:2]
    head_lane = lax.broadcasted_iota(jnp.int32, (1, LANES), 1) // 2
    safe = jnp.max(jnp.where(head_lane == pair, sq_bound, 0.0)) <= SCORE_SAFE * SCORE_SAFE
    pl.when(safe)(functools.partial(attend, False))
    pl.when(jnp.logical_not(safe))(functools.partial(attend, True))


def _attn_prompt_call(qh, kh, vh, nrm, *, nseq, seq_len, tq):
    t = qh.shape[0]
    npair = MLA_HEADS // 2
    tps = nrm.shape[0] // nseq
    return pl.pallas_call(
        functools.partial(_attn_prompt_kernel, seq_len=seq_len, tq=tq),
        grid=(nseq, npair),
        in_specs=[
            pl.BlockSpec((seq_len, 2 * HEAD_BLOCK), lambda b, p: (b, p)),
            pl.BlockSpec((seq_len, 2 * HEAD_BLOCK), lambda b, p: (b, p)),
            pl.BlockSpec((seq_len, 2 * MLA_V), lambda b, p: (b, p)),
            pl.BlockSpec((tps, NRM_ROWS, LANES), lambda b, p: (b, 0, 0)),
        ],
        out_specs=pl.BlockSpec((seq_len, 2 * MLA_V), lambda b, p: (b, p)),
        out_shape=jax.ShapeDtypeStruct((t, MVW), BF16),
        compiler_params=_params(("arbitrary", "arbitrary")),
        name="mla_attn_prompt",
    )(qh, kh, vh, nrm)


def _attn_stream_kernel(q_ref, ckc_ref, kpc_ref, ckn_ref, kpn_ref, wk_ref, wv_ref, place_ref,
                        o_ref, *, tq):
    q = q_ref[...]
    ckc = ckc_ref[0, 0].astype(BF16)
    kpc = _dot(kpc_ref[0, 0].astype(BF16), place_ref[...]).astype(BF16)
    ckn = ckn_ref[...].astype(BF16)
    kpn = _dot(kpn_ref[...].astype(BF16), place_ref[...]).astype(BF16)
    blocks = [slice(hd * HEAD_BLOCK, (hd + 1) * HEAD_BLOCK) for hd in range(MLA_HEADS)]
    q_abs = jnp.concatenate([_dot_nt(q[:, bs], wk_ref[:, bs]) for bs in blocks], axis=0).astype(BF16)
    q_blk = jnp.concatenate([q[:, bs] for bs in blocks], axis=0)
    s_c = _dot_nt(q_abs, ckc) + _dot_nt(q_blk, kpc)
    s_n = _dot_nt(q_abs, ckn) + _dot_nt(q_blk, kpn)
    lat = _softmax_pv([s_c, s_n], [ckc, ckn]).astype(BF16)
    lane = lax.broadcasted_iota(jnp.int32, (tq, LANES), 1)
    for p in range(MLA_HEADS // 2):
        wv = wv_ref[:, p * LANES:(p + 1) * LANES]
        even = _dot(lat[(2 * p) * tq:(2 * p + 1) * tq], wv)
        odd = _dot(lat[(2 * p + 1) * tq:(2 * p + 2) * tq], wv)
        o_ref[:, p * LANES:(p + 1) * LANES] = jnp.where(lane < MLA_V, even, odd).astype(BF16)


def _attn_stream_call(qh, cache_ckv, cache_kpe, ckv_new, kpe_new, w_k, w_v, place, layer,
                      *, nseq, tq, past):
    t = qh.shape[0]
    qpos = past + np.arange(tq)
    if past % CHUNK or not ((qpos[None, :] // CHUNK) <= (qpos[:, None] // CHUNK)).all():
        raise NotImplementedError("new frames spanning several chunks")
    return pl.pallas_call(
        functools.partial(_attn_stream_kernel, tq=tq),
        grid=(nseq,),
        in_specs=[
            pl.BlockSpec((tq, MQK), lambda b: (b, 0)),
            pl.BlockSpec((1, 1, past, MLA_KV_LORA), lambda b: (layer, b, 0, 0)),
            pl.BlockSpec((1, 1, past, MLA_ROPE), lambda b: (layer, b, 0, 0)),
            pl.BlockSpec((tq, MLA_KV_LORA), lambda b: (b, 0)),
            pl.BlockSpec((tq, MLA_ROPE), lambda b: (b, 0)),
            _layer_spec((MLA_KV_LORA, MQK), layer),
            _layer_spec((MLA_KV_LORA, MVW), layer),
            _const_spec((MLA_ROPE, HEAD_BLOCK)),
        ],
        out_specs=pl.BlockSpec((tq, MVW), lambda b: (b, 0)),
        out_shape=jax.ShapeDtypeStruct((t, MVW), BF16),
        compiler_params=_params(("arbitrary",)),
        name="mla_attn_stream",
    )(qh, cache_ckv, cache_kpe, ckv_new, kpe_new, w_k, w_v, place)


def _prep_weights(w_in, w_gk2, b_gk2, w_uq, w_ukv):
    w_in_p = jnp.pad(w_in.astype(BF16), ((0, 0), (0, 0), (0, IN_W - IN_COLS)))
    w_gk_p = jnp.pad(w_gk2.astype(BF16), ((0, 0), (0, LANES - GLA_GATE_RANK), (0, 0)))
    b_gk_p = b_gk2.reshape(DEPTH, 1, GQK)
    uq = w_uq.astype(BF16).reshape(DEPTH, MLA_Q_LORA, MLA_HEADS, MLA_NOPE + MLA_ROPE)
    uq = jnp.pad(uq, ((0, 0), (0, 0), (0, 0), (0, HEAD_BLOCK - MLA_NOPE - MLA_ROPE)))
    w_uq_p = uq.reshape(DEPTH, MLA_Q_LORA, MQK)
    ukv = w_ukv.astype(BF16).reshape(DEPTH, MLA_KV_LORA, MLA_HEADS, MLA_NOPE + MLA_V)
    uk = jnp.pad(ukv[..., :MLA_NOPE], ((0, 0), (0, 0), (0, 0), (0, HEAD_BLOCK - MLA_NOPE)))
    w_k_p = uk.reshape(DEPTH, MLA_KV_LORA, MQK)
    w_v_p = ukv[..., MLA_NOPE:].reshape(DEPTH, MLA_KV_LORA, MVW)
    return w_in_p, w_gk_p, b_gk_p, w_uq_p, w_k_p, w_v_p


def _rope_tables(pos, reps):
    half = MLA_ROPE // 2
    inv = ROPE_BASE ** (-jnp.arange(half, dtype=F32) / half)
    ang = pos.astype(F32)[:, None] * inv[None, :]
    cos, sin = jnp.cos(ang), jnp.sin(ang)
    n = pos.shape[0]
    one, zero = jnp.ones((n, MLA_NOPE), F32), jnp.zeros((n, MLA_NOPE), F32)
    pad = jnp.zeros((n, HEAD_BLOCK - MLA_NOPE - MLA_ROPE), F32)
    c = jnp.concatenate([one, cos, cos, pad], axis=1)
    s = jnp.concatenate([zero, -sin, sin, pad], axis=1)
    q_scale = MLA_SCALE * LOG2E
    tabs = (c * q_scale, s * q_scale, c, s)
    return tuple(jnp.tile(a, (reps, 1)) for a in tabs)


def _kpe_placement():
    p = np.zeros((MLA_ROPE, HEAD_BLOCK), np.float32)
    p[np.arange(MLA_ROPE), MLA_NOPE + np.arange(MLA_ROPE)] = 1.0
    return jnp.asarray(p, BF16)


def _trunk(x, mods, seq0, wts, tabs, *, nseq, seq_len, tm, tm_ffn, nb, gla_tl, gla_chunk, s0, stream=None):
    tps = max(seq_len // tm, 1) if nb == 1 else 1
    tile = dict(tm=tm, nb=nb, tps=tps, seq0=seq0)
    tile_ffn = dict(tile, tm=tm_ffn, tps=max(seq_len // tm_ffn, 1)) if nb == 1 else tile
    ckvs, kpes, states = [], [], []
    for l in range(DEPTH):
        x = _ffn_call(x, mods, wts["ln_g"], wts["ffn1_in"], wts["ffn1_out"], l, **tile_ffn)
        gq, gk, gv, gog, lg, qh, kh, vh, ckv, kpe, nrm = _mixin_call(
            x, mods, wts["ln_g"], wts["w_in"], wts["w_gk"], wts["b_gk"], wts["qg"], wts["w_uq"],
            wts["kvg"], wts["w_k"], wts["w_v"], tabs, l, **tile)
        og, st = _gla_call(gq, gk, gv, gog, lg, s0, wts["ng"], l,
                           nseq=nseq, seq_len=seq_len, tl=gla_tl, chunk=gla_chunk)
        if stream is None:
            om = _attn_prompt_call(qh, kh, vh, nrm, nseq=nseq, seq_len=seq_len, tq=min(256, seq_len))
        else:
            cache_ckv, cache_kpe, place = stream
            om = _attn_stream_call(qh, cache_ckv, cache_kpe, ckv, kpe, wts["w_k"], wts["w_v"],
                                   place, l, nseq=nseq, tq=seq_len, past=cache_ckv.shape[2])
        x = _mix_ffn_call(x, og, om, mods, wts["w_out"], wts["ln_g"], wts["ffn2_in"],
                          wts["ffn2_out"], wts["final_g"], l, final_norm=(l == DEPTH - 1), **tile_ffn)
        ckvs.append(ckv)
        kpes.append(kpe)
        states.append(st)
    return x, jnp.stack(ckvs), jnp.stack(kpes), jnp.stack(states)


def kernel(x_prompt, x_sample, cache_ckv, cache_kpe, state_gla, c_prompt, c_sample, ln_g, w_ada, b_ada, w_ffn1_in, w_ffn1_out, w_ffn2_in, w_ffn2_out, w_in, w_gk2, b_gk2, gla_norm_g, mla_qnorm_g, w_uq, mla_kvnorm_g, w_ukv, w_out, final_g):
    bsz, seq, _ = x_prompt.shape
    dbs, dseq, _ = x_sample.shape
    past = cache_ckv.shape[2]

    w_in_p, w_gk_p, b_gk_p, w_uq_p, w_k_p, w_v_p = _prep_weights(w_in, w_gk2, b_gk2, w_uq, w_ukv)
    wts = {
        "ln_g": ln_g.reshape(DEPTH, 3, 1, D_MODEL), "final_g": final_g.reshape(1, D_MODEL),
        "ffn1_in": w_ffn1_in.astype(BF16), "ffn1_out": w_ffn1_out.astype(BF16),
        "ffn2_in": w_ffn2_in.astype(BF16), "ffn2_out": w_ffn2_out.astype(BF16),
        "w_in": w_in_p, "w_gk": w_gk_p, "b_gk": b_gk_p, "w_uq": w_uq_p, "w_k": w_k_p, "w_v": w_v_p,
        "qg": mla_qnorm_g.reshape(DEPTH, 1, MLA_Q_LORA),
        "kvg": mla_kvnorm_g.reshape(DEPTH, 1, MLA_KV_LORA),
        "ng": gla_norm_g.reshape(DEPTH, 1, GLA_DV),
        "w_out": w_out.astype(BF16),
    }
    mods = _ada_call(jnp.concatenate([c_sample, c_prompt], axis=0), w_ada, b_ada)
    mods = mods.reshape(DEPTH, N_MOD, dbs + bsz, 1, D_MODEL)

    tm_p = min(512, seq)
    y_p, ckv_p, kpe_p, gla_p = _trunk(
        x_prompt.reshape(bsz * seq, D_MODEL), mods, dbs, wts, _rope_tables(jnp.arange(seq), 1),
        nseq=bsz, seq_len=seq, tm=tm_p, tm_ffn=min(FFN_TM, seq), nb=1, gla_tl=min(GLA_TL, seq), gla_chunk=CHUNK,
        s0=jnp.zeros((DEPTH, bsz, GLA_HEADS, GLA_DK, GLA_DV), F32))

    y_s, ckv_s, kpe_s, gla_s = _trunk(
        x_sample.reshape(dbs * dseq, D_MODEL), mods, 0, wts,
        _rope_tables(past + jnp.arange(dseq), dbs),
        nseq=dbs, seq_len=dseq, tm=dbs * dseq, tm_ffn=dbs * dseq, nb=dbs, gla_tl=dseq, gla_chunk=dseq,
        s0=state_gla.astype(F32), stream=(cache_ckv, cache_kpe, _kpe_placement()))

    return (y_p.reshape(bsz, seq, D_MODEL), y_s.reshape(dbs, dseq, D_MODEL),
            ckv_p.reshape(DEPTH, bsz, seq, MLA_KV_LORA), kpe_p.reshape(DEPTH, bsz, seq, MLA_ROPE),
            gla_p,
            ckv_s.reshape(DEPTH, dbs, dseq, MLA_KV_LORA), kpe_s.reshape(DEPTH, dbs, dseq, MLA_ROPE),
            gla_s)
```

```python
import functools
import math

import numpy as np
import jax
import jax.numpy as jnp
from jax import lax
from jax.experimental import pallas as pl
from jax.experimental.pallas import tpu as pltpu

F32 = jnp.float32
BF16 = jnp.bfloat16

D_MODEL = 1024
DEPTH = 4
CHUNK = 64
EPS = 1e-6
GLA_HEADS = 4
GLA_DK = 64
GLA_DV = 128
GLA_GATE_RANK = 16
GLA_GATE_NORM = 16.0
GLA_SUB = 16
EXP_CLAMP = 80.0
MLA_HEADS = 8
MLA_Q_LORA = 384
MLA_KV_LORA = 256
MLA_NOPE = 64
MLA_ROPE = 32
MLA_V = 64
MLA_SCALE = 1.0 / math.sqrt(MLA_NOPE + MLA_ROPE)
LOG2E = math.log2(math.e)
ROPE_BASE = 10000.0
D_FF = 2816
N_MOD = 9
LANES = 128
HEAD_BLOCK = 128
NEG_BIG = -1e30
SCORE_SAFE = 60.0
NRM_ROWS = 8

GQK = GLA_HEADS * GLA_DK
GVW = GLA_HEADS * GLA_DV
MQK = MLA_HEADS * HEAD_BLOCK
MVW = MLA_HEADS * MLA_V
O_GQ, O_GK, O_GV = 0, GQK, 2 * GQK
O_GLR = O_GV + GVW
O_KPE_COL = O_GLR + GLA_GATE_RANK + GVW + MLA_Q_LORA + MLA_KV_LORA
O_KPE_BLK = (O_KPE_COL // LANES) * LANES
IN_COLS = O_KPE_COL + MLA_ROPE
IN_W = -(-IN_COLS // LANES) * LANES
FFN_TF = 256
FFN_TM = 1024
GLA_TL = 512
CUMSUM_ROWS = 256
VMEM_LIMIT = 56 * 1024 * 1024


def _dot(a, b):
    return jnp.dot(a, b, preferred_element_type=F32)


def _dot_nt(a, b):
    return lax.dot_general(a, b, (((1,), (1,)), ((), ())), preferred_element_type=F32)


def _dot_tn(a, b):
    return lax.dot_general(a, b, (((0,), (0,)), ((), ())), preferred_element_type=F32)


def _rms(x, g):
    ms = jnp.mean(x * x, axis=-1, keepdims=True)
    return x * lax.rsqrt(ms + EPS) * g


def _modulate(x, g, sh, sc, nb):
    tm, d = x.shape
    y = _rms(x, g)
    if nb == 1:
        return y * (1.0 + sc[0]) + sh[0]
    y3 = y.reshape(nb, tm // nb, d)
    return (y3 * (1.0 + sc) + sh).reshape(tm, d)


def _gated_residual(x, y, coef, nb):
    tm, d = x.shape
    if nb == 1:
        return x + coef[0] * y
    return x + (coef * y.reshape(nb, tm // nb, d)).reshape(tm, d)


def _const_spec(shape):
    nd = len(shape)
    return pl.BlockSpec(shape, lambda *_: (0,) * nd, pipeline_mode=pl.Buffered(1))


def _layer_spec(shape, *lead):
    nd = len(shape)
    return pl.BlockSpec((None,) * len(lead) + tuple(shape), lambda *_: tuple(lead) + (0,) * nd,
                        pipeline_mode=pl.Buffered(1))


def _mod_spec(nmod, nb, tps, seq0, layer, blk):
    return pl.BlockSpec((None, nmod, nb, 1, D_MODEL), lambda i: (layer, blk, seq0 + i // tps, 0, 0))


def _params(sem):
    return pltpu.CompilerParams(dimension_semantics=sem, vmem_limit_bytes=VMEM_LIMIT)


def _ada_kernel(c_ref, w_ref, b_ref, o_ref):
    c = c_ref[...]
    sc = (c * jax.nn.sigmoid(c)).astype(BF16)
    o_ref[0, 0] = _dot(sc, w_ref[0].astype(BF16)) + b_ref[0, 0]


def _ada_call(c_all, w_ada, b_ada):
    nrow = c_all.shape[0]
    b4 = b_ada.reshape(DEPTH, N_MOD, 1, D_MODEL)
    return pl.pallas_call(
        _ada_kernel,
        grid=(DEPTH, N_MOD),
        in_specs=[
            pl.BlockSpec((nrow, D_MODEL), lambda l, j: (0, 0)),
            pl.BlockSpec((1, D_MODEL, D_MODEL), lambda l, j: (l, 0, j)),
            pl.BlockSpec((1, 1, 1, D_MODEL), lambda l, j: (l, j, 0, 0)),
        ],
        out_specs=pl.BlockSpec((1, 1, nrow, D_MODEL), lambda l, j: (l, j, 0, 0)),
        out_shape=jax.ShapeDtypeStruct((DEPTH, N_MOD, nrow, D_MODEL), F32),
        compiler_params=_params(("arbitrary", "arbitrary")),
        name="ada_mod",
    )(c_all, w_ada, b4)


def _ffn_body(x, sh, sc, gate, g, wa_ref, wb_ref, wo_ref, h_ref, nb):
    n = _modulate(x, g, sh, sc, nb).astype(BF16)
    for c in range(D_FF // FFN_TF):
        lo = c * FFN_TF
        a = _dot(n, wa_ref[:, lo:lo + FFN_TF])
        b = _dot(n, wb_ref[:, lo:lo + FFN_TF])
        h_ref[:, lo:lo + FFN_TF] = (a * jax.nn.sigmoid(a) * b).astype(BF16)
    y = _dot(h_ref[...], wo_ref[...])
    return _gated_residual(x, y, 0.5 * (1.0 + gate), nb)


def _mix_residual(x, og_ref, om_ref, gate, wmix_ref, nb):
    y = _dot(og_ref[...], wmix_ref[0:GVW, :]) + _dot(om_ref[...], wmix_ref[GVW:GVW + MVW, :])
    return _gated_residual(x, y, 1.0 + gate, nb)


def _ffn_kernel(x_ref, mod_ref, g_ref, wa_ref, wb_ref, wo_ref, o_ref, h_ref, *, nb):
    o_ref[...] = _ffn_body(x_ref[...], mod_ref[0], mod_ref[1], mod_ref[2], g_ref[...],
                           wa_ref, wb_ref, wo_ref, h_ref, nb)


def _mix_ffn_kernel(x_ref, og_ref, om_ref, gmix_ref, mod_ref, wmix_ref, g_ref, wa_ref, wb_ref, wo_ref,
                    fg_ref, o_ref, h_ref, *, nb, final_norm):
    x = _mix_residual(x_ref[...], og_ref, om_ref, gmix_ref[2], wmix_ref, nb)
    x = _ffn_body(x, mod_ref[0], mod_ref[1], mod_ref[2], g_ref[...], wa_ref, wb_ref, wo_ref, h_ref, nb)
    o_ref[...] = _rms(x, fg_ref[...]) if final_norm else x


def _ffn_weight_specs():
    return [_const_spec((D_MODEL, D_FF)), _const_spec((D_MODEL, D_FF)), _const_spec((D_FF, D_MODEL))]


def _ffn_call(x, mods, ln_g, w16, layer, *, tm, nb, tps, seq0):
    t = x.shape[0]
    return pl.pallas_call(
        functools.partial(_ffn_kernel, nb=nb),
        grid=(t // tm,),
        in_specs=[
            pl.BlockSpec((tm, D_MODEL), lambda i: (i, 0)),
            _mod_spec(3, nb, tps, seq0, layer, 0),
            _layer_spec((1, D_MODEL), layer, 0),
        ] + _ffn_weight_specs(),
        out_specs=pl.BlockSpec((tm, D_MODEL), lambda i: (i, 0)),
        out_shape=jax.ShapeDtypeStruct((t, D_MODEL), F32),
        scratch_shapes=[pltpu.VMEM((tm, D_FF), BF16)],
        compiler_params=_params(("arbitrary",)),
        name="ffn",
    )(x, mods, ln_g, *w16)


def _mix_ffn_call(x, og, om, mods, w_mix, ln_g, w16, fg, layer, *, tm, nb, tps, seq0, final_norm):
    t = x.shape[0]
    tok = lambda w: pl.BlockSpec((tm, w), lambda i: (i, 0))
    return pl.pallas_call(
        functools.partial(_mix_ffn_kernel, nb=nb, final_norm=final_norm),
        grid=(t // tm,),
        in_specs=[
            tok(D_MODEL), tok(GVW), tok(MVW),
            _mod_spec(3, nb, tps, seq0, layer, 1),
            _mod_spec(3, nb, tps, seq0, layer, 2),
            _layer_spec((GVW + MVW, D_MODEL), layer),
            _layer_spec((1, D_MODEL), layer, 2),
        ] + _ffn_weight_specs() + [_const_spec((1, D_MODEL))],
        out_specs=tok(D_MODEL),
        out_shape=jax.ShapeDtypeStruct((t, D_MODEL), F32),
        scratch_shapes=[pltpu.VMEM((tm, D_FF), BF16)],
        compiler_params=_params(("arbitrary",)),
        name="mix_ffn",
    )(x, og, om, mods, mods, w_mix, ln_g, *w16, fg)


def _ffn_stream_kernel(*refs, nb, has_mix, final_norm):
    if has_mix:
        x_ref, og_ref, om_ref, gmix_ref, mod_ref, wmix_ref, *refs = refs
    else:
        x_ref, mod_ref, *refs = refs
    (g_ref, wa_ref, wb_ref, wo_ref, fg_ref,
     y_ref, wa16_ref, wb16_ref, wo16_ref, xs_ref, n_ref, acc_ref) = refs
    c = pl.program_id(0)

    @pl.when(c == 0)
    def _():
        x = x_ref[...]
        if has_mix:
            x = _mix_residual(x, og_ref, om_ref, gmix_ref[2], wmix_ref, nb)
        xs_ref[...] = x
        n_ref[...] = _modulate(x, g_ref[...], mod_ref[0], mod_ref[1], nb).astype(BF16)
        acc_ref[...] = jnp.zeros_like(acc_ref)

    wa, wb, wo = wa_ref[...].astype(BF16), wb_ref[...].astype(BF16), wo_ref[...].astype(BF16)
    wa16_ref[...] = wa
    wb16_ref[...] = wb
    wo16_ref[...] = wo
    n = n_ref[...]
    a = _dot(n, wa)
    b = _dot(n, wb)
    acc_ref[...] += _dot((a * jax.nn.sigmoid(a) * b).astype(BF16), wo)

    @pl.when(c == pl.num_programs(0) - 1)
    def _():
        x = _gated_residual(xs_ref[...], acc_ref[...], 0.5 * (1.0 + mod_ref[2]), nb)
        y_ref[...] = _rms(x, fg_ref[...]) if final_norm else x


def _ffn_stream_call(x, mix, mods, ln_g, w_in, w_out, fg, layer, *, nb, ffn_idx, final_norm):
    tm = x.shape[0]
    nchunk = D_FF // FFN_TF
    fixed = lambda shape: pl.BlockSpec(shape, lambda c: (0,) * len(shape))
    mod = lambda blk: pl.BlockSpec((None, 3, nb, 1, D_MODEL), lambda c: (layer, blk, 0, 0, 0))
    if mix is None:
        args, specs = [x, mods], [fixed((tm, D_MODEL)), mod(ffn_idx)]
    else:
        og, om, w_mix = mix
        args = [x, og, om, mods, mods, w_mix]
        specs = [fixed((tm, D_MODEL)), fixed((tm, GVW)), fixed((tm, MVW)), mod(1), mod(ffn_idx),
                 _layer_spec((GVW + MVW, D_MODEL), layer)]
    args += [ln_g, w_in, w_in, w_out, fg]
    specs += [
        _layer_spec((1, D_MODEL), layer, ffn_idx),
        pl.BlockSpec((None, D_MODEL, FFN_TF), lambda c: (layer, 0, c)),
        pl.BlockSpec((None, D_MODEL, FFN_TF), lambda c: (layer, 0, nchunk + c)),
        pl.BlockSpec((None, FFN_TF, D_MODEL), lambda c: (layer, c, 0)),
        fixed((1, D_MODEL)),
    ]
    y, wa, wb, wo = pl.pallas_call(
        functools.partial(_ffn_stream_kernel, nb=nb, has_mix=mix is not None, final_norm=final_norm),
        grid=(nchunk,),
        in_specs=specs,
        out_specs=[fixed((tm, D_MODEL)),
                   pl.BlockSpec((D_MODEL, FFN_TF), lambda c: (0, c)),
                   pl.BlockSpec((D_MODEL, FFN_TF), lambda c: (0, c)),
                   pl.BlockSpec((FFN_TF, D_MODEL), lambda c: (c, 0))],
        out_shape=[jax.ShapeDtypeStruct((tm, D_MODEL), F32),
                   jax.ShapeDtypeStruct((D_MODEL, D_FF), BF16),
                   jax.ShapeDtypeStruct((D_MODEL, D_FF), BF16),
                   jax.ShapeDtypeStruct((D_FF, D_MODEL), BF16)],
        scratch_shapes=[pltpu.VMEM((tm, D_MODEL), F32), pltpu.VMEM((tm, D_MODEL), BF16),
                        pltpu.VMEM((tm, D_MODEL), F32)],
        compiler_params=_params(("arbitrary",)),
        name="ffn_stream",
    )(*args)
    return y, (wa, wb, wo)


def _swap_rope_halves(xb, lane):
    fwd = pltpu.roll(xb, LANES - MLA_ROPE // 2, axis=1)
    bwd = pltpu.roll(xb, MLA_ROPE // 2, axis=1)
    return jnp.where(lane < MLA_NOPE + MLA_ROPE // 2, fwd, bwd)


N_SHIFT = (GVW + MLA_Q_LORA + MLA_KV_LORA) // LANES


def _head_sq_norms(a, ones_ref):
    sq = _dot((a * a).astype(BF16), ones_ref[...])
    return jnp.max(sq, axis=0, keepdims=True)


def _mixin_kernel(x_ref, mod_ref, g_ref, win_ref, wgk_ref, bgk_ref, qg_ref, wuq_ref,
                  kvg_ref, wk_ref, wv_ref, ones_ref, cq_ref, sq_ref, ck_ref, sk_ref,
                  gq_ref, gk_ref, gv_ref, gog_ref, lg_ref, qh_ref, kh_ref, vh_ref,
                  ckv_ref, kpe_ref, nrm_ref, wsh_ref, *, nb):
    @pl.when(pl.program_id(0) == 0)
    def _():
        keep = LANES - GLA_GATE_RANK
        wl = lax.broadcasted_iota(jnp.int32, (D_MODEL, LANES), 1)
        prev = pltpu.roll(win_ref[:, O_GLR:O_GLR + LANES].astype(F32), keep, axis=1)
        for j in range(N_SHIFT):
            c0 = O_GLR + LANES * (j + 1)
            nxt = pltpu.roll(win_ref[:, c0:c0 + LANES].astype(F32), keep, axis=1)
            wsh_ref[:, LANES * j:LANES * (j + 1)] = jnp.where(wl < keep, prev, nxt).astype(BF16)
            prev = nxt

    x = x_ref[...]
    tm = x.shape[0]
    n = _modulate(x, g_ref[...], mod_ref[0], mod_ref[1], nb).astype(BF16)
    lane = lax.broadcasted_iota(jnp.int32, (tm, LANES), 1)
    h = _dot(n, win_ref[:, 0:O_GLR + LANES])
    gq_ref[...] = h[:, O_GQ:O_GQ + GQK] * (GLA_DK ** -0.5)
    gk_ref[...] = h[:, O_GK:O_GK + GQK]
    gv_ref[...] = h[:, O_GV:O_GV + GVW]
    z = _dot(h[:, O_GLR:O_GLR + LANES].astype(BF16), wgk_ref[...]) + bgk_ref[...]
    lsig = jnp.minimum(z, 0.0) - jnp.log1p(jnp.exp(-jnp.abs(z)))
    lg_ref[...] = lsig / GLA_GATE_NORM
    h2 = _dot(n, wsh_ref[...])
    gog_ref[...] = h2[:, 0:GVW]
    cqn = _rms(h2[:, GVW:GVW + MLA_Q_LORA], qg_ref[...]).astype(BF16)
    q = _dot(cqn, wuq_ref[...])
    cq, sq = cq_ref[...], sq_ref[...]
    qr = jnp.concatenate(
        [q[:, hd * HEAD_BLOCK:(hd + 1) * HEAD_BLOCK] * cq
         + _swap_rope_halves(q[:, hd * HEAD_BLOCK:(hd + 1) * HEAD_BLOCK], lane) * sq
         for hd in range(MLA_HEADS)], axis=1)
    qh_ref[...] = qr.astype(BF16)
    ckv = _rms(h2[:, GVW + MLA_Q_LORA:], kvg_ref[...])
    ckv_ref[...] = ckv
    kpe_lane0 = O_KPE_COL - O_KPE_BLK
    kb = pltpu.roll(_dot(n, win_ref[:, O_KPE_BLK:O_KPE_BLK + LANES]), MLA_NOPE - kpe_lane0, axis=1)
    kb = jnp.where((lane >= MLA_NOPE) & (lane < MLA_NOPE + MLA_ROPE), kb, 0.0)
    kpe = kb * ck_ref[...] + _swap_rope_halves(kb, lane) * sk_ref[...]
    kpe_ref[...] = pltpu.roll(kpe, LANES - MLA_NOPE, axis=1)[:, :MLA_ROPE]
    ckv16 = ckv.astype(BF16)
    kn = _dot(ckv16, wk_ref[...])
    kr = jnp.concatenate(
        [kn[:, hd * HEAD_BLOCK:(hd + 1) * HEAD_BLOCK] + kpe for hd in range(MLA_HEADS)], axis=1)
    kh_ref[...] = kr.astype(BF16)
    vh_ref[...] = _dot(ckv16, wv_ref[...]).astype(BF16)
    nrm_ref[0] = jnp.concatenate(
        [_head_sq_norms(qr, ones_ref), _head_sq_norms(kr, ones_ref),
         jnp.zeros((NRM_ROWS - 2, LANES), F32)], axis=0)


def _mixin_call(x, mods, ln_g, w_in, w_gk, b_gk, qg, w_uq, kvg, w_k, w_v, tabs, layer, *, tm, nb, tps, seq0):
    t = x.shape[0]
    tok = lambda w: pl.BlockSpec((tm, w), lambda i: (i, 0))
    tab = pl.BlockSpec((tm, LANES), lambda i: (i % tps, 0))
    out_w = [(GQK, F32), (GQK, F32), (GVW, F32), (GVW, F32), (GQK, F32),
             (MQK, BF16), (MQK, BF16), (MVW, BF16), (MLA_KV_LORA, F32), (MLA_ROPE, F32)]
    nrm_spec = pl.BlockSpec((1, NRM_ROWS, LANES), lambda i: (i, 0, 0))
    nrm_shape = jax.ShapeDtypeStruct((t // tm, NRM_ROWS, LANES), F32)
    head_ones = np.zeros((MQK, LANES), np.float32)
    head_ones[np.arange(MQK), np.arange(MQK) // HEAD_BLOCK] = 1.0
    return pl.pallas_call(
        functools.partial(_mixin_kernel, nb=nb),
        grid=(t // tm,),
        in_specs=[
            tok(D_MODEL),
            _mod_spec(3, nb, tps, seq0, layer, 1),
            _layer_spec((1, D_MODEL), layer, 1),
            _layer_spec((D_MODEL, IN_W), layer),
            _layer_spec((LANES, GQK), layer),
            _layer_spec((1, GQK), layer),
            _layer_spec((1, MLA_Q_LORA), layer),
            _layer_spec((MLA_Q_LORA, MQK), layer),
            _layer_spec((1, MLA_KV_LORA), layer),
            _layer_spec((MLA_KV_LORA, MQK), layer),
            _layer_spec((MLA_KV_LORA, MVW), layer),
            _const_spec((MQK, LANES)),
            tab, tab, tab, tab,
        ],
        out_specs=[tok(w) for w, _ in out_w] + [nrm_spec],
        out_shape=[jax.ShapeDtypeStruct((t, w), dt) for w, dt in out_w] + [nrm_shape],
        scratch_shapes=[pltpu.VMEM((D_MODEL, N_SHIFT * LANES), BF16)],
        compiler_params=_params(("arbitrary",)),
        name="mixer_in",
    )(x, mods, ln_g, w_in, w_gk, b_gk, qg, w_uq, kvg, w_k, w_v, jnp.asarray(head_ones, BF16), *tabs)


def _gla_kernel(q_ref, k_ref, v_ref, og_ref, lg_ref, s0_ref, ng_ref, o_ref, sout_ref, st_ref,
                *, chunk, nch):
    j = pl.program_id(1)
    nsub = chunk // GLA_SUB
    tl = chunk * nch
    lane_head = lax.broadcasted_iota(jnp.int32, (chunk, GQK), 1) // GLA_DK

    def stack_heads(a):
        return jnp.concatenate(
            [jnp.where(lane_head == hd, a, 0.0) for hd in range(GLA_HEADS)], axis=0).astype(BF16)

    @pl.when(j == 0)
    def _():
        for hd in range(GLA_HEADS):
            s0 = s0_ref[0, hd]
            rows = [jnp.zeros((GLA_DK, GLA_DV), F32)] * GLA_HEADS
            rows[hd] = s0
            st_ref[hd * GLA_DV:(hd + 1) * GLA_DV, :] = jnp.concatenate(rows, axis=0).T

    grp = min(tl, CUMSUM_ROWS)
    ri = lax.broadcasted_iota(jnp.int32, (grp, grp), 0)
    ci = lax.broadcasted_iota(jnp.int32, (grp, grp), 1)
    tri = jnp.where((ci <= ri) & (ci // chunk == ri // chunk), 1.0, 0.0).astype(BF16)
    b_parts = []
    for g in range(tl // grp):
        lg = lg_ref[g * grp:(g + 1) * grp, :]
        p0 = lg.astype(BF16)
        r1 = lg - p0.astype(F32)
        p1 = r1.astype(BF16)
        p2 = (r1 - p1.astype(F32)).astype(BF16)
        b_parts.append(_dot(tri, p0) + _dot(tri, p1) + _dot(tri, p2))
    b_all = b_parts[0] if len(b_parts) == 1 else jnp.concatenate(b_parts, axis=0)

    rt = lax.broadcasted_iota(jnp.int32, (GLA_HEADS * chunk, nsub * chunk), 0) % chunk
    cc = lax.broadcasted_iota(jnp.int32, (GLA_HEADS * chunk, nsub * chunk), 1)
    keep = (cc // chunk == rt // GLA_SUB) & (cc % chunk <= rt)
    ng = ng_ref[...]

    def v_of(c):
        return v_ref[c * chunk:(c + 1) * chunk, :].astype(BF16)

    start = []
    st = st_ref[...]
    for c in range(nch):
        b = b_all[c * chunk:(c + 1) * chunk]
        kl = k_ref[c * chunk:(c + 1) * chunk, :] * jnp.exp(b[chunk - 1:chunk] - b)
        g_end = jnp.exp(b[chunk - 8:chunk])[7:8]
        start.append(st)
        st = st * g_end + _dot_tn(v_of(c), kl.astype(BF16))
    st_ref[...] = st

    for c in range(nch):
        r0 = c * chunk
        b = b_all[r0:r0 + chunk]
        q = q_ref[r0:r0 + chunk, :]
        k = k_ref[r0:r0 + chunk, :]
        r_own = jnp.broadcast_to(b[0:1], (chunk, GQK)) if nsub == 1 else jnp.concatenate(
            [jnp.broadcast_to(b[i * GLA_SUB:i * GLA_SUB + 1], (GLA_SUB, GQK)) for i in range(nsub)],
            axis=0)
        qt = q * jnp.exp(b - r_own)
        qe = q * jnp.exp(b)
        k_rel = [(k * jnp.exp(jnp.minimum(b[i * GLA_SUB:i * GLA_SUB + 1] - b, EXP_CLAMP))).astype(BF16)
                 for i in range(nsub)]
        att = _dot_nt(stack_heads(qt), jnp.concatenate(k_rel, axis=0))
        att = jnp.where(keep, att, 0.0).astype(BF16)
        v = v_of(c)
        o_all = (_dot(att, jnp.concatenate([v] * nsub, axis=0))
                 + _dot_nt(stack_heads(qe), start[c].astype(BF16)))
        for hd in range(GLA_HEADS):
            cols = slice(hd * GLA_DV, (hd + 1) * GLA_DV)
            o = o_all[hd * chunk:(hd + 1) * chunk, cols]
            og = og_ref[r0:r0 + chunk, cols]
            o_ref[r0:r0 + chunk, cols] = (_rms(o, ng) * (og * jax.nn.sigmoid(og))).astype(BF16)

    @pl.when(j == pl.num_programs(1) - 1)
    def _():
        for hd in range(GLA_HEADS):
            sout_ref[0, hd] = st_ref[hd * GLA_DV:(hd + 1) * GLA_DV, :].T[hd * GLA_DK:(hd + 1) * GLA_DK, :]


def _gla_call(gq, gk, gv, gog, lg, s0, ng, layer, *, nseq, seq_len, tl, chunk):
    t = gq.shape[0]
    tps = seq_len // tl
    tok = lambda w: pl.BlockSpec((tl, w), lambda b, j: (b * tps + j, 0))
    st_spec = pl.BlockSpec((1, GLA_HEADS, GLA_DK, GLA_DV), lambda b, j: (b, 0, 0, 0))
    s0_spec = pl.BlockSpec((None, 1, GLA_HEADS, GLA_DK, GLA_DV), lambda b, j: (layer, b, 0, 0, 0))
    return pl.pallas_call(
        functools.partial(_gla_kernel, chunk=chunk, nch=tl // chunk),
        grid=(nseq, tps),
        in_specs=[tok(GQK), tok(GQK), tok(GVW), tok(GVW), tok(GQK), s0_spec,
                  _layer_spec((1, GLA_DV), layer)],
        out_specs=[tok(GVW), st_spec],
        out_shape=[jax.ShapeDtypeStruct((t, GVW), BF16),
                   jax.ShapeDtypeStruct((nseq, GLA_HEADS, GLA_DK, GLA_DV), F32)],
        scratch_shapes=[pltpu.VMEM((GVW, GQK), F32)],
        compiler_params=_params(("arbitrary", "arbitrary")),
        name="gla",
    )(gq, gk, gv, gog, lg, s0, ng)


def _softmax_pv(parts, vs, shift_by_max=True):
    if shift_by_max:
        m = functools.reduce(jnp.maximum, [jnp.max(s, axis=-1, keepdims=True) for s in parts])
    acc, den = None, None
    for s, v in zip(parts, vs):
        p = jnp.exp2(s - m) if shift_by_max else jnp.exp2(s)
        d = jnp.sum(p, axis=-1, keepdims=True)
        o = _dot(p.astype(BF16), v)
        acc = o if acc is None else acc + o
        den = d if den is None else den + d
    return acc / den


def _attn_prompt_kernel(q_ref, k_ref, v_ref, nrm_ref, o_ref, *, seq_len, tq):
    pair = pl.program_id(1)
    lane = lax.broadcasted_iota(jnp.int32, (tq, LANES), 1)
    qi = lax.broadcasted_iota(jnp.int32, (tq, tq), 0) // CHUNK
    ki = lax.broadcasted_iota(jnp.int32, (tq, tq), 1) // CHUNK
    diag_ok = ki <= qi

    def attend(shift_by_max):
        for i in range(seq_len // tq):
            r0 = i * tq
            outs = []
            for hh in range(2):
                cs = slice(hh * HEAD_BLOCK, (hh + 1) * HEAD_BLOCK)
                q = q_ref[r0:r0 + tq, cs]
                s_d = jnp.where(diag_ok, _dot_nt(q, k_ref[r0:r0 + tq, cs]), NEG_BIG)
                parts, vs = [s_d], [v_ref[r0:r0 + tq, :]]
                if i > 0:
                    parts.append(_dot_nt(q, k_ref[0:r0, cs]))
                    vs.append(v_ref[0:r0, :])
                outs.append(_softmax_pv(parts, vs, shift_by_max))
            o_ref[r0:r0 + tq, :] = jnp.where(lane < MLA_V, outs[0], outs[1]).astype(BF16)

    nrm = jnp.max(nrm_ref[...], axis=0)
    sq_bound = nrm[0:1] * nrm[1:2]
    head_lane = lax.broadcasted_iota(jnp.int32, (1, LANES), 1) // 2
    safe = jnp.max(jnp.where(head_lane == pair, sq_bound, 0.0)) <= SCORE_SAFE * SCORE_SAFE
    pl.when(safe)(functools.partial(attend, False))
    pl.when(jnp.logical_not(safe))(functools.partial(attend, True))


def _attn_prompt_call(qh, kh, vh, nrm, *, nseq, seq_len, tq):
    t = qh.shape[0]
    npair = MLA_HEADS // 2
    tps = nrm.shape[0] // nseq
    return pl.pallas_call(
        functools.partial(_attn_prompt_kernel, seq_len=seq_len, tq=tq),
        grid=(nseq, npair),
        in_specs=[
            pl.BlockSpec((seq_len, 2 * HEAD_BLOCK), lambda b, p: (b, p)),
            pl.BlockSpec((seq_len, 2 * HEAD_BLOCK), lambda b, p: (b, p)),
            pl.BlockSpec((seq_len, 2 * MLA_V), lambda b, p: (b, p)),
            pl.BlockSpec((tps, NRM_ROWS, LANES), lambda b, p: (b, 0, 0)),
        ],
        out_specs=pl.BlockSpec((seq_len, 2 * MLA_V), lambda b, p: (b, p)),
        out_shape=jax.ShapeDtypeStruct((t, MVW), BF16),
        compiler_params=_params(("arbitrary", "arbitrary")),
        name="mla_attn_prompt",
    )(qh, kh, vh, nrm)


def _attn_stream_kernel(q_ref, ckc_ref, kpc_ref, ckn_ref, kpn_ref, wk_ref, wv_ref, place_ref,
                        o_ref, *, tq):
    q = q_ref[...]
    place_t = place_ref[...]
    ckc = ckc_ref[0, 0].astype(BF16)
    kpc_t = _dot(place_t, kpc_ref[0, 0].astype(BF16)).astype(BF16)
    ckn = ckn_ref[...].astype(BF16)
    kpn = _dot_nt(kpn_ref[...].astype(BF16), place_t).astype(BF16)
    blocks = [slice(hd * HEAD_BLOCK, (hd + 1) * HEAD_BLOCK) for hd in range(MLA_HEADS)]
    q_abs = jnp.concatenate([_dot_nt(q[:, bs], wk_ref[:, bs]) for bs in blocks], axis=0).astype(BF16)
    q_blk = jnp.concatenate([q[:, bs] for bs in blocks], axis=0)
    s_c = _dot_nt(q_abs, ckc) + _dot(q_blk, kpc_t)
    s_n = _dot_nt(q_abs, ckn) + _dot_nt(q_blk, kpn)
    lat = _softmax_pv([s_c, s_n], [ckc, ckn]).astype(BF16)
    lane = lax.broadcasted_iota(jnp.int32, (tq, LANES), 1)
    for p in range(MLA_HEADS // 2):
        wv = wv_ref[:, p * LANES:(p + 1) * LANES]
        even = _dot(lat[(2 * p) * tq:(2 * p + 1) * tq], wv)
        odd = _dot(lat[(2 * p + 1) * tq:(2 * p + 2) * tq], wv)
        o_ref[:, p * LANES:(p + 1) * LANES] = jnp.where(lane < MLA_V, even, odd).astype(BF16)


def _attn_stream_call(qh, cache_ckv, cache_kpe, ckv_new, kpe_new, w_k, w_v, place, layer,
                      *, nseq, tq, past):
    t = qh.shape[0]
    qpos = past + np.arange(tq)
    if past % CHUNK or not ((qpos[None, :] // CHUNK) <= (qpos[:, None] // CHUNK)).all():
        raise NotImplementedError("new frames spanning several chunks")
    return pl.pallas_call(
        functools.partial(_attn_stream_kernel, tq=tq),
        grid=(nseq,),
        in_specs=[
            pl.BlockSpec((tq, MQK), lambda b: (b, 0)),
            pl.BlockSpec((1, 1, past, MLA_KV_LORA), lambda b: (layer, b, 0, 0)),
            pl.BlockSpec((1, 1, MLA_ROPE, past), lambda b: (layer, b, 0, 0)),
            pl.BlockSpec((tq, MLA_KV_LORA), lambda b: (b, 0)),
            pl.BlockSpec((tq, MLA_ROPE), lambda b: (b, 0)),
            _layer_spec((MLA_KV_LORA, MQK), layer),
            _layer_spec((MLA_KV_LORA, MVW), layer),
            _const_spec((HEAD_BLOCK, MLA_ROPE)),
        ],
        out_specs=pl.BlockSpec((tq, MVW), lambda b: (b, 0)),
        out_shape=jax.ShapeDtypeStruct((t, MVW), BF16),
        compiler_params=_params(("arbitrary",)),
        name="mla_attn_stream",
    )(qh, cache_ckv, cache_kpe, ckv_new, kpe_new, w_k, w_v, place)


def _prep_weights(w_in, w_gk2, b_gk2, w_uq, w_ukv):
    w_in_p = jnp.pad(w_in.astype(BF16), ((0, 0), (0, 0), (0, IN_W - IN_COLS)))
    w_gk_p = jnp.pad(w_gk2.astype(BF16), ((0, 0), (0, LANES - GLA_GATE_RANK), (0, 0)))
    b_gk_p = b_gk2.reshape(DEPTH, 1, GQK)
    uq = w_uq.astype(BF16).reshape(DEPTH, MLA_Q_LORA, MLA_HEADS, MLA_NOPE + MLA_ROPE)
    uq = jnp.pad(uq, ((0, 0), (0, 0), (0, 0), (0, HEAD_BLOCK - MLA_NOPE - MLA_ROPE)))
    w_uq_p = uq.reshape(DEPTH, MLA_Q_LORA, MQK)
    ukv = w_ukv.astype(BF16).reshape(DEPTH, MLA_KV_LORA, MLA_HEADS, MLA_NOPE + MLA_V)
    uk = jnp.pad(ukv[..., :MLA_NOPE], ((0, 0), (0, 0), (0, 0), (0, HEAD_BLOCK - MLA_NOPE)))
    w_k_p = uk.reshape(DEPTH, MLA_KV_LORA, MQK)
    w_v_p = ukv[..., MLA_NOPE:].reshape(DEPTH, MLA_KV_LORA, MVW)
    return w_in_p, w_gk_p, b_gk_p, w_uq_p, w_k_p, w_v_p


def _rope_tables(pos, reps):
    half = MLA_ROPE // 2
    inv = ROPE_BASE ** (-jnp.arange(half, dtype=F32) / half)
    ang = pos.astype(F32)[:, None] * inv[None, :]
    cos, sin = jnp.cos(ang), jnp.sin(ang)
    n = pos.shape[0]
    one, zero = jnp.ones((n, MLA_NOPE), F32), jnp.zeros((n, MLA_NOPE), F32)
    pad = jnp.zeros((n, HEAD_BLOCK - MLA_NOPE - MLA_ROPE), F32)
    c = jnp.concatenate([one, cos, cos, pad], axis=1)
    s = jnp.concatenate([zero, -sin, sin, pad], axis=1)
    q_scale = MLA_SCALE * LOG2E
    tabs = (c * q_scale, s * q_scale, c, s)
    return tuple(jnp.tile(a, (reps, 1)) for a in tabs)


def _kpe_placement():
    p = np.zeros((HEAD_BLOCK, MLA_ROPE), np.float32)
    p[MLA_NOPE + np.arange(MLA_ROPE), np.arange(MLA_ROPE)] = 1.0
    return jnp.asarray(p, BF16)


def _trunk(x, mods, seq0, wts, tabs, *, nseq, seq_len, tm, tm_ffn, nb, gla_tl, gla_chunk, s0,
           stream=None, ffn16=None):
    tps = max(seq_len // tm, 1) if nb == 1 else 1
    tile = dict(tm=tm, nb=nb, tps=tps, seq0=seq0)
    tile_ffn = dict(tile, tm=tm_ffn, tps=max(seq_len // tm_ffn, 1)) if nb == 1 else tile
    ckvs, kpes, states, cast = [], [], [], []
    for l in range(DEPTH):
        if stream is None:
            x = _ffn_call(x, mods, wts["ln_g"], ffn16[l][0], l, **tile_ffn)
        else:
            x, w1 = _ffn_stream_call(x, None, mods, wts["ln_g"], wts["ffn1_in"], wts["ffn1_out"],
                                     wts["final_g"], l, nb=nb, ffn_idx=0, final_norm=False)
        gq, gk, gv, gog, lg, qh, kh, vh, ckv, kpe, nrm = _mixin_call(
            x, mods, wts["ln_g"], wts["w_in"], wts["w_gk"], wts["b_gk"], wts["qg"], wts["w_uq"],
            wts["kvg"], wts["w_k"], wts["w_v"], tabs, l, **tile)
        og, st = _gla_call(gq, gk, gv, gog, lg, s0, wts["ng"], l,
                           nseq=nseq, seq_len=seq_len, tl=gla_tl, chunk=gla_chunk)
        last = l == DEPTH - 1
        if stream is None:
            om = _attn_prompt_call(qh, kh, vh, nrm, nseq=nseq, seq_len=seq_len, tq=min(256, seq_len))
            x = _mix_ffn_call(x, og, om, mods, wts["w_out"], wts["ln_g"], ffn16[l][1], wts["final_g"],
                              l, final_norm=last, **tile_ffn)
        else:
            cache_ckv, cache_kpe_t, place_t = stream
            om = _attn_stream_call(qh, cache_ckv, cache_kpe_t, ckv, kpe, wts["w_k"], wts["w_v"],
                                   place_t, l, nseq=nseq, tq=seq_len, past=cache_ckv.shape[2])
            x, w2 = _ffn_stream_call(x, (og, om, wts["w_out"]), mods, wts["ln_g"], wts["ffn2_in"],
                                     wts["ffn2_out"], wts["final_g"], l, nb=nb, ffn_idx=2,
                                     final_norm=last)
            cast.append((w1, w2))
        ckvs.append(ckv)
        kpes.append(kpe)
        states.append(st)
    return x, jnp.stack(ckvs), jnp.stack(kpes), jnp.stack(states), cast


def kernel(x_prompt, x_sample, cache_ckv, cache_kpe, state_gla, c_prompt, c_sample, ln_g, w_ada, b_ada, w_ffn1_in, w_ffn1_out, w_ffn2_in, w_ffn2_out, w_in, w_gk2, b_gk2, gla_norm_g, mla_qnorm_g, w_uq, mla_kvnorm_g, w_ukv, w_out, final_g):
    bsz, seq, _ = x_prompt.shape
    dbs, dseq, _ = x_sample.shape
    past = cache_ckv.shape[2]

    w_in_p, w_gk_p, b_gk_p, w_uq_p, w_k_p, w_v_p = _prep_weights(w_in, w_gk2, b_gk2, w_uq, w_ukv)
    wts = {
        "ln_g": ln_g.reshape(DEPTH, 3, 1, D_MODEL), "final_g": final_g.reshape(1, D_MODEL),
        "ffn1_in": w_ffn1_in, "ffn1_out": w_ffn1_out, "ffn2_in": w_ffn2_in, "ffn2_out": w_ffn2_out,
        "w_in": w_in_p, "w_gk": w_gk_p, "b_gk": b_gk_p, "w_uq": w_uq_p, "w_k": w_k_p, "w_v": w_v_p,
        "qg": mla_qnorm_g.reshape(DEPTH, 1, MLA_Q_LORA),
        "kvg": mla_kvnorm_g.reshape(DEPTH, 1, MLA_KV_LORA),
        "ng": gla_norm_g.reshape(DEPTH, 1, GLA_DV),
        "w_out": w_out.astype(BF16),
    }
    mods = _ada_call(jnp.concatenate([c_sample, c_prompt], axis=0), w_ada, b_ada)
    mods = mods.reshape(DEPTH, N_MOD, dbs + bsz, 1, D_MODEL)

    y_s, ckv_s, kpe_s, gla_s, ffn16 = _trunk(
        x_sample.reshape(dbs * dseq, D_MODEL), mods, 0, wts,
        _rope_tables(past + jnp.arange(dseq), dbs),
        nseq=dbs, seq_len=dseq, tm=dbs * dseq, tm_ffn=dbs * dseq, nb=dbs, gla_tl=dseq, gla_chunk=dseq,
        s0=state_gla.astype(F32),
        stream=(cache_ckv, jnp.swapaxes(cache_kpe, 2, 3), _kpe_placement()))

    tm_p = min(512, seq)
    y_p, ckv_p, kpe_p, gla_p, _ = _trunk(
        x_prompt.reshape(bsz * seq, D_MODEL), mods, dbs, wts, _rope_tables(jnp.arange(seq), 1),
        nseq=bsz, seq_len=seq, tm=tm_p, tm_ffn=min(FFN_TM, seq), nb=1, gla_tl=min(GLA_TL, seq), gla_chunk=CHUNK,
        s0=jnp.zeros((DEPTH, bsz, GLA_HEADS, GLA_DK, GLA_DV), F32), ffn16=ffn16)

    return (y_p.reshape(bsz, seq, D_MODEL), y_s.reshape(dbs, dseq, D_MODEL),
            ckv_p.reshape(DEPTH, bsz, seq, MLA_KV_LORA), kpe_p.reshape(DEPTH, bsz, seq, MLA_ROPE),
            gla_p,
            ckv_s.reshape(DEPTH, dbs, dseq, MLA_KV_LORA), kpe_s.reshape(DEPTH, dbs, dseq, MLA_ROPE),
            gla_s)
```

```python
import functools
import math

import numpy as np
import jax
import jax.numpy as jnp
from jax import lax
from jax.experimental import pallas as pl
from jax.experimental.pallas import tpu as pltpu

F32 = jnp.float32
BF16 = jnp.bfloat16

D_MODEL = 1024
DEPTH = 4
CHUNK = 64
EPS = 1e-6
GLA_HEADS = 4
GLA_DK = 64
GLA_DV = 128
GLA_GATE_RANK = 16
GLA_GATE_NORM = 16.0
GLA_SUB = 16
EXP_CLAMP = 80.0
MLA_HEADS = 8
MLA_Q_LORA = 384
MLA_KV_LORA = 256
MLA_NOPE = 64
MLA_ROPE = 32
MLA_V = 64
MLA_SCALE = 1.0 / math.sqrt(MLA_NOPE + MLA_ROPE)
LOG2E = math.log2(math.e)
ROPE_BASE = 10000.0
D_FF = 2816
N_MOD = 9
LANES = 128
HEAD_BLOCK = 128
NEG_BIG = -1e30
SCORE_SAFE = 60.0
NRM_ROWS = 8

GQK = GLA_HEADS * GLA_DK
GVW = GLA_HEADS * GLA_DV
MQK = MLA_HEADS * HEAD_BLOCK
MVW = MLA_HEADS * MLA_V
O_GQ, O_GK, O_GV = 0, GQK, 2 * GQK
O_GLR = O_GV + GVW
O_KPE_COL = O_GLR + GLA_GATE_RANK + GVW + MLA_Q_LORA + MLA_KV_LORA
O_KPE_BLK = (O_KPE_COL // LANES) * LANES
IN_COLS = O_KPE_COL + MLA_ROPE
IN_W = -(-IN_COLS // LANES) * LANES
FFN_TF = 256
FFN_TP = 256
FFN_TM = 1024
GLA_TL = 1024
ATTN_TQ = 256
STREAM_SEQS = 4
CUMSUM_ROWS = 256
VMEM_LIMIT = 56 * 1024 * 1024


def _dot(a, b):
    return jnp.dot(a, b, preferred_element_type=F32)


def _dot_nt(a, b):
    return lax.dot_general(a, b, (((1,), (1,)), ((), ())), preferred_element_type=F32)


def _dot_tn(a, b):
    return lax.dot_general(a, b, (((0,), (0,)), ((), ())), preferred_element_type=F32)


def _rms(x, g):
    ms = jnp.mean(x * x, axis=-1, keepdims=True)
    return x * lax.rsqrt(ms + EPS) * g


def _modulate(x, g, sh, sc, nb):
    tm, d = x.shape
    y = _rms(x, g)
    if nb == 1:
        return y * (1.0 + sc[0]) + sh[0]
    y3 = y.reshape(nb, tm // nb, d)
    return (y3 * (1.0 + sc) + sh).reshape(tm, d)


def _gated_residual(x, y, coef, nb):
    tm, d = x.shape
    if nb == 1:
        return x + coef[0] * y
    return x + (coef * y.reshape(nb, tm // nb, d)).reshape(tm, d)


def _const_spec(shape):
    nd = len(shape)
    return pl.BlockSpec(shape, lambda *_: (0,) * nd, pipeline_mode=pl.Buffered(1))


def _layer_spec(shape, *lead):
    nd = len(shape)
    return pl.BlockSpec((None,) * len(lead) + tuple(shape), lambda *_: tuple(lead) + (0,) * nd,
                        pipeline_mode=pl.Buffered(1))


def _mod_spec(nmod, nb, tps, seq0, layer, blk):
    return pl.BlockSpec((None, nmod, nb, 1, D_MODEL), lambda i: (layer, blk, seq0 + i // tps, 0, 0))


def _params(sem):
    return pltpu.CompilerParams(dimension_semantics=sem, vmem_limit_bytes=VMEM_LIMIT)


ADA_GROUP = 3


def _ada_kernel(c_ref, w_ref, b_ref, o_ref):
    c = c_ref[...]
    sc = (c * jax.nn.sigmoid(c)).astype(BF16)
    for k in range(ADA_GROUP):
        w = w_ref[0, :, k * D_MODEL:(k + 1) * D_MODEL].astype(BF16)
        o_ref[0, k] = _dot(sc, w) + b_ref[0, k]


def _ada_call(c_all, w_ada, b_ada):
    nrow = c_all.shape[0]
    b4 = b_ada.reshape(DEPTH, N_MOD, 1, D_MODEL)
    return pl.pallas_call(
        _ada_kernel,
        grid=(DEPTH, N_MOD // ADA_GROUP),
        in_specs=[
            pl.BlockSpec((nrow, D_MODEL), lambda l, j: (0, 0)),
            pl.BlockSpec((1, D_MODEL, ADA_GROUP * D_MODEL), lambda l, j: (l, 0, j)),
            pl.BlockSpec((1, ADA_GROUP, 1, D_MODEL), lambda l, j: (l, j, 0, 0)),
        ],
        out_specs=pl.BlockSpec((1, ADA_GROUP, nrow, D_MODEL), lambda l, j: (l, j, 0, 0)),
        out_shape=jax.ShapeDtypeStruct((DEPTH, N_MOD, nrow, D_MODEL), F32),
        compiler_params=_params(("arbitrary", "arbitrary")),
        name="ada_mod",
    )(c_all, w_ada, b4)


def _ffn_body(x, sh, sc, gate, g, wa_ref, wb_ref, wo_ref, h_ref, nb):
    n = _modulate(x, g, sh, sc, nb).astype(BF16)
    for c in range(D_FF // FFN_TP):
        lo = c * FFN_TP
        a = _dot(n, wa_ref[:, lo:lo + FFN_TP])
        b = _dot(n, wb_ref[:, lo:lo + FFN_TP])
        h_ref[:, lo:lo + FFN_TP] = (a * jax.nn.sigmoid(a) * b).astype(BF16)
    y = _dot(h_ref[...], wo_ref[...])
    return _gated_residual(x, y, 0.5 * (1.0 + gate), nb)


def _mix_residual(x, og_ref, om_ref, gate, wmix_ref, nb):
    y = _dot(og_ref[...], wmix_ref[0:GVW, :]) + _dot(om_ref[...], wmix_ref[GVW:GVW + MVW, :])
    return _gated_residual(x, y, 1.0 + gate, nb)


def _ffn_kernel(x_ref, mod_ref, g_ref, wa_ref, wb_ref, wo_ref, o_ref, h_ref, *, nb):
    o_ref[...] = _ffn_body(x_ref[...], mod_ref[0], mod_ref[1], mod_ref[2], g_ref[...],
                           wa_ref, wb_ref, wo_ref, h_ref, nb)


def _mix_ffn_kernel(x_ref, og_ref, om_ref, gmix_ref, mod_ref, wmix_ref, g_ref, wa_ref, wb_ref, wo_ref,
                    fg_ref, o_ref, h_ref, *, nb, final_norm):
    x = _mix_residual(x_ref[...], og_ref, om_ref, gmix_ref[2], wmix_ref, nb)
    x = _ffn_body(x, mod_ref[0], mod_ref[1], mod_ref[2], g_ref[...], wa_ref, wb_ref, wo_ref, h_ref, nb)
    o_ref[...] = _rms(x, fg_ref[...]) if final_norm else x


def _ffn_weight_specs():
    return [_const_spec((D_MODEL, D_FF)), _const_spec((D_MODEL, D_FF)), _const_spec((D_FF, D_MODEL))]


def _ffn_call(x, mods, ln_g, w16, layer, *, tm, nb, tps, seq0):
    t = x.shape[0]
    return pl.pallas_call(
        functools.partial(_ffn_kernel, nb=nb),
        grid=(t // tm,),
        in_specs=[
            pl.BlockSpec((tm, D_MODEL), lambda i: (i, 0)),
            _mod_spec(3, nb, tps, seq0, layer, 0),
            _layer_spec((1, D_MODEL), layer, 0),
        ] + _ffn_weight_specs(),
        out_specs=pl.BlockSpec((tm, D_MODEL), lambda i: (i, 0)),
        out_shape=jax.ShapeDtypeStruct((t, D_MODEL), F32),
        scratch_shapes=[pltpu.VMEM((tm, D_FF), BF16)],
        compiler_params=_params(("arbitrary",)),
        name="ffn",
    )(x, mods, ln_g, *w16)


def _mix_ffn_call(x, og, om, mods, w_mix, ln_g, w16, fg, layer, *, tm, nb, tps, seq0, final_norm):
    t = x.shape[0]
    tok = lambda w: pl.BlockSpec((tm, w), lambda i: (i, 0))
    return pl.pallas_call(
        functools.partial(_mix_ffn_kernel, nb=nb, final_norm=final_norm),
        grid=(t // tm,),
        in_specs=[
            tok(D_MODEL), tok(GVW), tok(MVW),
            _mod_spec(3, nb, tps, seq0, layer, 1),
            _mod_spec(3, nb, tps, seq0, layer, 2),
            _layer_spec((GVW + MVW, D_MODEL), layer),
            _layer_spec((1, D_MODEL), layer, 2),
        ] + _ffn_weight_specs() + [_const_spec((1, D_MODEL))],
        out_specs=tok(D_MODEL),
        out_shape=jax.ShapeDtypeStruct((t, D_MODEL), F32),
        scratch_shapes=[pltpu.VMEM((tm, D_FF), BF16)],
        compiler_params=_params(("arbitrary",)),
        name="mix_ffn",
    )(x, og, om, mods, mods, w_mix, ln_g, *w16, fg)


def _ffn_stream_kernel(*refs, nb, has_mix, final_norm):
    if has_mix:
        x_ref, og_ref, om_ref, gmix_ref, mod_ref, wmix_ref, *refs = refs
    else:
        x_ref, mod_ref, *refs = refs
    (g_ref, wa_ref, wb_ref, wo_ref, fg_ref,
     y_ref, wa16_ref, wb16_ref, wo16_ref, xs_ref, n_ref, acc_ref) = refs
    c = pl.program_id(0)

    @pl.when(c == 0)
    def _():
        x = x_ref[...]
        if has_mix:
            x = _mix_residual(x, og_ref, om_ref, gmix_ref[2], wmix_ref, nb)
        xs_ref[...] = x
        n_ref[...] = _modulate(x, g_ref[...], mod_ref[0], mod_ref[1], nb).astype(BF16)
        acc_ref[...] = jnp.zeros_like(acc_ref)

    wa, wb, wo = wa_ref[...].astype(BF16), wb_ref[...].astype(BF16), wo_ref[...].astype(BF16)
    wa16_ref[...] = wa
    wb16_ref[...] = wb
    wo16_ref[...] = wo
    n = n_ref[...]
    a = _dot(n, wa)
    b = _dot(n, wb)
    acc_ref[...] += _dot((a * jax.nn.sigmoid(a) * b).astype(BF16), wo)

    @pl.when(c == pl.num_programs(0) - 1)
    def _():
        x = _gated_residual(xs_ref[...], acc_ref[...], 0.5 * (1.0 + mod_ref[2]), nb)
        y_ref[...] = _rms(x, fg_ref[...]) if final_norm else x


def _ffn_stream_call(x, mix, mods, ln_g, w_in, w_out, fg, layer, *, nb, ffn_idx, final_norm):
    tm = x.shape[0]
    nchunk = D_FF // FFN_TF
    fixed = lambda shape: pl.BlockSpec(shape, lambda c: (0,) * len(shape))
    mod = lambda blk: pl.BlockSpec((None, 3, nb, 1, D_MODEL), lambda c: (layer, blk, 0, 0, 0))
    if mix is None:
        args, specs = [x, mods], [fixed((tm, D_MODEL)), mod(ffn_idx)]
    else:
        og, om, w_mix = mix
        args = [x, og, om, mods, mods, w_mix]
        specs = [fixed((tm, D_MODEL)), fixed((tm, GVW)), fixed((tm, MVW)), mod(1), mod(ffn_idx),
                 _layer_spec((GVW + MVW, D_MODEL), layer)]
    args += [ln_g, w_in, w_in, w_out, fg]
    specs += [
        _layer_spec((1, D_MODEL), layer, ffn_idx),
        pl.BlockSpec((None, D_MODEL, FFN_TF), lambda c: (layer, 0, c)),
        pl.BlockSpec((None, D_MODEL, FFN_TF), lambda c: (layer, 0, nchunk + c)),
        pl.BlockSpec((None, FFN_TF, D_MODEL), lambda c: (layer, c, 0)),
        fixed((1, D_MODEL)),
    ]
    y, wa, wb, wo = pl.pallas_call(
        functools.partial(_ffn_stream_kernel, nb=nb, has_mix=mix is not None, final_norm=final_norm),
        grid=(nchunk,),
        in_specs=specs,
        out_specs=[fixed((tm, D_MODEL)),
                   pl.BlockSpec((D_MODEL, FFN_TF), lambda c: (0, c)),
                   pl.BlockSpec((D_MODEL, FFN_TF), lambda c: (0, c)),
                   pl.BlockSpec((FFN_TF, D_MODEL), lambda c: (c, 0))],
        out_shape=[jax.ShapeDtypeStruct((tm, D_MODEL), F32),
                   jax.ShapeDtypeStruct((D_MODEL, D_FF), BF16),
                   jax.ShapeDtypeStruct((D_MODEL, D_FF), BF16),
                   jax.ShapeDtypeStruct((D_FF, D_MODEL), BF16)],
        scratch_shapes=[pltpu.VMEM((tm, D_MODEL), F32), pltpu.VMEM((tm, D_MODEL), BF16),
                        pltpu.VMEM((tm, D_MODEL), F32)],
        compiler_params=_params(("arbitrary",)),
        name="ffn_stream",
    )(*args)
    return y, (wa, wb, wo)


def _swap_rope_halves(xb, lane):
    fwd = pltpu.roll(xb, LANES - MLA_ROPE // 2, axis=1)
    bwd = pltpu.roll(xb, MLA_ROPE // 2, axis=1)
    return jnp.where(lane < MLA_NOPE + MLA_ROPE // 2, fwd, bwd)


N_SHIFT = (GVW + MLA_Q_LORA + MLA_KV_LORA) // LANES


def _head_sq_norms(a, ones_ref):
    sq = _dot((a * a).astype(BF16), ones_ref[...])
    return jnp.max(sq, axis=0, keepdims=True)


def _mixin_kernel(x_ref, mod_ref, g_ref, win_ref, wgk_ref, bgk_ref, qg_ref, wuq_ref,
                  kvg_ref, wk_ref, wv_ref, ones_ref, cq_ref, sq_ref, ck_ref, sk_ref,
                  gq_ref, gk_ref, gv_ref, gog_ref, lg_ref, qh_ref, kh_ref, vh_ref,
                  ckv_ref, kpe_ref, nrm_ref, wsh_ref, *, nb):
    @pl.when(pl.program_id(0) == 0)
    def _():
        keep = LANES - GLA_GATE_RANK
        wl = lax.broadcasted_iota(jnp.int32, (D_MODEL, LANES), 1)
        prev = pltpu.roll(win_ref[:, O_GLR:O_GLR + LANES].astype(F32), keep, axis=1)
        for j in range(N_SHIFT):
            c0 = O_GLR + LANES * (j + 1)
            nxt = pltpu.roll(win_ref[:, c0:c0 + LANES].astype(F32), keep, axis=1)
            wsh_ref[:, LANES * j:LANES * (j + 1)] = jnp.where(wl < keep, prev, nxt).astype(BF16)
            prev = nxt

    x = x_ref[...]
    tm = x.shape[0]
    n = _modulate(x, g_ref[...], mod_ref[0], mod_ref[1], nb).astype(BF16)
    lane = lax.broadcasted_iota(jnp.int32, (tm, LANES), 1)
    h = _dot(n, win_ref[:, 0:O_GLR + LANES])
    gq_ref[...] = h[:, O_GQ:O_GQ + GQK] * (GLA_DK ** -0.5)
    gk_ref[...] = h[:, O_GK:O_GK + GQK]
    gv_ref[...] = h[:, O_GV:O_GV + GVW]
    z = _dot(h[:, O_GLR:O_GLR + LANES].astype(BF16), wgk_ref[...]) + bgk_ref[...]
    lsig = jnp.minimum(z, 0.0) - jnp.log1p(jnp.exp(-jnp.abs(z)))
    lg_ref[...] = lsig / GLA_GATE_NORM
    h2 = _dot(n, wsh_ref[...])
    gog_ref[...] = h2[:, 0:GVW]
    cqn = _rms(h2[:, GVW:GVW + MLA_Q_LORA], qg_ref[...]).astype(BF16)
    q = _dot(cqn, wuq_ref[...])
    cq, sq = cq_ref[...], sq_ref[...]
    qr = jnp.concatenate(
        [q[:, hd * HEAD_BLOCK:(hd + 1) * HEAD_BLOCK] * cq
         + _swap_rope_halves(q[:, hd * HEAD_BLOCK:(hd + 1) * HEAD_BLOCK], lane) * sq
         for hd in range(MLA_HEADS)], axis=1)
    qh_ref[...] = qr.astype(BF16)
    ckv = _rms(h2[:, GVW + MLA_Q_LORA:], kvg_ref[...])
    ckv_ref[...] = ckv
    kpe_lane0 = O_KPE_COL - O_KPE_BLK
    kb = pltpu.roll(_dot(n, win_ref[:, O_KPE_BLK:O_KPE_BLK + LANES]), MLA_NOPE - kpe_lane0, axis=1)
    kb = jnp.where((lane >= MLA_NOPE) & (lane < MLA_NOPE + MLA_ROPE), kb, 0.0)
    kpe = kb * ck_ref[...] + _swap_rope_halves(kb, lane) * sk_ref[...]
    kpe_ref[...] = pltpu.roll(kpe, LANES - MLA_NOPE, axis=1)[:, :MLA_ROPE]
    ckv16 = ckv.astype(BF16)
    kn = _dot(ckv16, wk_ref[...])
    kr = jnp.concatenate(
        [kn[:, hd * HEAD_BLOCK:(hd + 1) * HEAD_BLOCK] + kpe for hd in range(MLA_HEADS)], axis=1)
    kh_ref[...] = kr.astype(BF16)
    vh_ref[...] = _dot(ckv16, wv_ref[...]).astype(BF16)
    nrm_ref[0] = jnp.concatenate(
        [_head_sq_norms(qr, ones_ref), _head_sq_norms(kr, ones_ref),
         jnp.zeros((NRM_ROWS - 2, LANES), F32)], axis=0)


def _mixin_call(x, mods, ln_g, w_in, w_gk, b_gk, qg, w_uq, kvg, w_k, w_v, tabs, layer, *, tm, nb, tps, seq0):
    t = x.shape[0]
    tok = lambda w: pl.BlockSpec((tm, w), lambda i: (i, 0))
    tab = pl.BlockSpec((tm, LANES), lambda i: (i % tps, 0))
    out_w = [(GQK, F32), (GQK, F32), (GVW, F32), (GVW, F32), (GQK, F32),
             (MQK, BF16), (MQK, BF16), (MVW, BF16), (MLA_KV_LORA, F32), (MLA_ROPE, F32)]
    nrm_spec = pl.BlockSpec((1, NRM_ROWS, LANES), lambda i: (i, 0, 0))
    nrm_shape = jax.ShapeDtypeStruct((t // tm, NRM_ROWS, LANES), F32)
    head_ones = np.zeros((MQK, LANES), np.float32)
    head_ones[np.arange(MQK), np.arange(MQK) // HEAD_BLOCK] = 1.0
    return pl.pallas_call(
        functools.partial(_mixin_kernel, nb=nb),
        grid=(t // tm,),
        in_specs=[
            tok(D_MODEL),
            _mod_spec(3, nb, tps, seq0, layer, 1),
            _layer_spec((1, D_MODEL), layer, 1),
            _layer_spec((D_MODEL, IN_W), layer),
            _layer_spec((LANES, GQK), layer),
            _layer_spec((1, GQK), layer),
            _layer_spec((1, MLA_Q_LORA), layer),
            _layer_spec((MLA_Q_LORA, MQK), layer),
            _layer_spec((1, MLA_KV_LORA), layer),
            _layer_spec((MLA_KV_LORA, MQK), layer),
            _layer_spec((MLA_KV_LORA, MVW), layer),
            _const_spec((MQK, LANES)),
            tab, tab, tab, tab,
        ],
        out_specs=[tok(w) for w, _ in out_w] + [nrm_spec],
        out_shape=[jax.ShapeDtypeStruct((t, w), dt) for w, dt in out_w] + [nrm_shape],
        scratch_shapes=[pltpu.VMEM((D_MODEL, N_SHIFT * LANES), BF16)],
        compiler_params=_params(("arbitrary",)),
        name="mixer_in",
    )(x, mods, ln_g, w_in, w_gk, b_gk, qg, w_uq, kvg, w_k, w_v, jnp.asarray(head_ones, BF16), *tabs)


def _gla_kernel(q_ref, k_ref, v_ref, og_ref, lg_ref, s0_ref, ng_ref, o_ref, sout_ref, st_ref,
                *, chunk, nch):
    j = pl.program_id(1)
    nsub = chunk // GLA_SUB
    tl = chunk * nch
    lane_head = lax.broadcasted_iota(jnp.int32, (chunk, GQK), 1) // GLA_DK

    def stack_heads(a):
        return jnp.concatenate(
            [jnp.where(lane_head == hd, a, 0.0) for hd in range(GLA_HEADS)], axis=0).astype(BF16)

    @pl.when(j == 0)
    def _():
        for hd in range(GLA_HEADS):
            s0 = s0_ref[0, hd]
            rows = [jnp.zeros((GLA_DK, GLA_DV), F32)] * GLA_HEADS
            rows[hd] = s0
            st_ref[hd * GLA_DV:(hd + 1) * GLA_DV, :] = jnp.concatenate(rows, axis=0).T

    grp = min(tl, CUMSUM_ROWS)
    ri = lax.broadcasted_iota(jnp.int32, (grp, grp), 0)
    ci = lax.broadcasted_iota(jnp.int32, (grp, grp), 1)
    tri = jnp.where((ci <= ri) & (ci // chunk == ri // chunk), 1.0, 0.0).astype(BF16)
    b_parts = []
    for g in range(tl // grp):
        lg = lg_ref[g * grp:(g + 1) * grp, :]
        p0 = lg.astype(BF16)
        r1 = lg - p0.astype(F32)
        p1 = r1.astype(BF16)
        p2 = (r1 - p1.astype(F32)).astype(BF16)
        b_parts.append(_dot(tri, p0) + _dot(tri, p1) + _dot(tri, p2))
    b_all = b_parts[0] if len(b_parts) == 1 else jnp.concatenate(b_parts, axis=0)

    rt = lax.broadcasted_iota(jnp.int32, (GLA_HEADS * chunk, nsub * chunk), 0) % chunk
    cc = lax.broadcasted_iota(jnp.int32, (GLA_HEADS * chunk, nsub * chunk), 1)
    keep = (cc // chunk == rt // GLA_SUB) & (cc % chunk <= rt)
    ng = ng_ref[...]

    def v_of(c):
        return v_ref[c * chunk:(c + 1) * chunk, :].astype(BF16)

    start = []
    st = st_ref[...]
    for c in range(nch):
        b = b_all[c * chunk:(c + 1) * chunk]
        kl = k_ref[c * chunk:(c + 1) * chunk, :] * jnp.exp(b[chunk - 1:chunk] - b)
        g_end = jnp.exp(b[chunk - 8:chunk])[7:8]
        start.append(st)
        st = st * g_end + _dot_tn(v_of(c), kl.astype(BF16))
    st_ref[...] = st

    for c in range(nch):
        r0 = c * chunk
        b = b_all[r0:r0 + chunk]
        q = q_ref[r0:r0 + chunk, :]
        k = k_ref[r0:r0 + chunk, :]
        r_own = jnp.broadcast_to(b[0:1], (chunk, GQK)) if nsub == 1 else jnp.concatenate(
            [jnp.broadcast_to(b[i * GLA_SUB:i * GLA_SUB + 1], (GLA_SUB, GQK)) for i in range(nsub)],
            axis=0)
        qt = q * jnp.exp(b - r_own)
        qe = q * jnp.exp(b)
        k_rel = [(k * jnp.exp(jnp.minimum(b[i * GLA_SUB:i * GLA_SUB + 1] - b, EXP_CLAMP))).astype(BF16)
                 for i in range(nsub)]
        att = _dot_nt(stack_heads(qt), jnp.concatenate(k_rel, axis=0))
        att = jnp.where(keep, att, 0.0).astype(BF16)
        v = v_of(c)
        o_all = (_dot(att, jnp.concatenate([v] * nsub, axis=0))
                 + _dot_nt(stack_heads(qe), start[c].astype(BF16)))
        for hd in range(GLA_HEADS):
            cols = slice(hd * GLA_DV, (hd + 1) * GLA_DV)
            o = o_all[hd * chunk:(hd + 1) * chunk, cols]
            og = og_ref[r0:r0 + chunk, cols]
            o_ref[r0:r0 + chunk, cols] = (_rms(o, ng) * (og * jax.nn.sigmoid(og))).astype(BF16)

    @pl.when(j == pl.num_programs(1) - 1)
    def _():
        for hd in range(GLA_HEADS):
            sout_ref[0, hd] = st_ref[hd * GLA_DV:(hd + 1) * GLA_DV, :].T[hd * GLA_DK:(hd + 1) * GLA_DK, :]


def _gla_call(gq, gk, gv, gog, lg, s0, ng, layer, *, nseq, seq_len, tl, chunk):
    t = gq.shape[0]
    tps = seq_len // tl
    tok = lambda w: pl.BlockSpec((tl, w), lambda b, j: (b * tps + j, 0))
    st_spec = pl.BlockSpec((1, GLA_HEADS, GLA_DK, GLA_DV), lambda b, j: (b, 0, 0, 0))
    s0_spec = pl.BlockSpec((None, 1, GLA_HEADS, GLA_DK, GLA_DV), lambda b, j: (layer, b, 0, 0, 0))
    return pl.pallas_call(
        functools.partial(_gla_kernel, chunk=chunk, nch=tl // chunk),
        grid=(nseq, tps),
        in_specs=[tok(GQK), tok(GQK), tok(GVW), tok(GVW), tok(GQK), s0_spec,
                  _layer_spec((1, GLA_DV), layer)],
        out_specs=[tok(GVW), st_spec],
        out_shape=[jax.ShapeDtypeStruct((t, GVW), BF16),
                   jax.ShapeDtypeStruct((nseq, GLA_HEADS, GLA_DK, GLA_DV), F32)],
        scratch_shapes=[pltpu.VMEM((GVW, GQK), F32)],
        compiler_params=_params(("arbitrary", "arbitrary")),
        name="gla",
    )(gq, gk, gv, gog, lg, s0, ng)


def _softmax_pv(parts, vs, shift_by_max=True):
    if shift_by_max:
        m = functools.reduce(jnp.maximum, [jnp.max(s, axis=-1, keepdims=True) for s in parts])
    acc, den = None, None
    for s, v in zip(parts, vs):
        p = jnp.exp2(s - m) if shift_by_max else jnp.exp2(s)
        d = jnp.sum(p, axis=-1, keepdims=True)
        o = _dot(p.astype(BF16), v)
        acc = o if acc is None else acc + o
        den = d if den is None else den + d
    return acc / den


def _attn_prompt_kernel(q_ref, k_ref, v_ref, nrm_ref, o_ref, *, seq_len, tq):
    pair = pl.program_id(1)
    lane = lax.broadcasted_iota(jnp.int32, (tq, LANES), 1)
    qi = lax.broadcasted_iota(jnp.int32, (tq, tq), 0) // CHUNK
    ki = lax.broadcasted_iota(jnp.int32, (tq, tq), 1) // CHUNK
    diag_ok = ki <= qi

    def attend(shift_by_max):
        for i in range(seq_len // tq):
            r0 = i * tq
            outs = []
            for hh in range(2):
                cs = slice(hh * HEAD_BLOCK, (hh + 1) * HEAD_BLOCK)
                q = q_ref[r0:r0 + tq, cs]
                s_d = jnp.where(diag_ok, _dot_nt(q, k_ref[r0:r0 + tq, cs]), NEG_BIG)
                parts, vs = [s_d], [v_ref[r0:r0 + tq, :]]
                if i > 0:
                    parts.append(_dot_nt(q, k_ref[0:r0, cs]))
                    vs.append(v_ref[0:r0, :])
                outs.append(_softmax_pv(parts, vs, shift_by_max))
            o_ref[r0:r0 + tq, :] = jnp.where(lane < MLA_V, outs[0], outs[1]).astype(BF16)

    nrm = jnp.max(nrm_ref[...], axis=0)
    sq_bound = nrm[0:1] * nrm[1:2]
    head_lane = lax.broadcasted_iota(jnp.int32, (1, LANES), 1) // 2
    safe = jnp.max(jnp.where(head_lane == pair, sq_bound, 0.0)) <= SCORE_SAFE * SCORE_SAFE
    pl.when(safe)(functools.partial(attend, False))
    pl.when(jnp.logical_not(safe))(functools.partial(attend, True))


def _attn_prompt_call(qh, kh, vh, nrm, *, nseq, seq_len, tq):
    t = qh.shape[0]
    npair = MLA_HEADS // 2
    tps = nrm.shape[0] // nseq
    return pl.pallas_call(
        functools.partial(_attn_prompt_kernel, seq_len=seq_len, tq=tq),
        grid=(nseq, npair),
        in_specs=[
            pl.BlockSpec((seq_len, 2 * HEAD_BLOCK), lambda b, p: (b, p)),
            pl.BlockSpec((seq_len, 2 * HEAD_BLOCK), lambda b, p: (b, p)),
            pl.BlockSpec((seq_len, 2 * MLA_V), lambda b, p: (b, p)),
            pl.BlockSpec((tps, NRM_ROWS, LANES), lambda b, p: (b, 0, 0)),
        ],
        out_specs=pl.BlockSpec((seq_len, 2 * MLA_V), lambda b, p: (b, p)),
        out_shape=jax.ShapeDtypeStruct((t, MVW), BF16),
        compiler_params=_params(("arbitrary", "arbitrary")),
        name="mla_attn_prompt",
    )(qh, kh, vh, nrm)


def _attn_stream_kernel(q_ref, ckc_ref, kpc_ref, ckn_ref, kpn_ref, wk_ref, wv_ref, place_ref,
                        o_ref, *, tq, ns):
    place_t = place_ref[...]
    blocks = [slice(hd * HEAD_BLOCK, (hd + 1) * HEAD_BLOCK) for hd in range(MLA_HEADS)]
    lane = lax.broadcasted_iota(jnp.int32, (tq, LANES), 1)
    for s in range(ns):
        rows = slice(s * tq, (s + 1) * tq)
        q = q_ref[rows, :]
        ckc = ckc_ref[0, s].astype(BF16)
        kpc_t = _dot(place_t, kpc_ref[0, s].astype(BF16)).astype(BF16)
        ckn = ckn_ref[rows, :].astype(BF16)
        kpn = _dot_nt(kpn_ref[rows, :].astype(BF16), place_t).astype(BF16)
        q_abs = jnp.concatenate(
            [_dot_nt(q[:, bs], wk_ref[:, bs]) for bs in blocks], axis=0).astype(BF16)
        q_blk = jnp.concatenate([q[:, bs] for bs in blocks], axis=0)
        s_c = _dot_nt(q_abs, ckc) + _dot(q_blk, kpc_t)
        s_n = _dot_nt(q_abs, ckn) + _dot_nt(q_blk, kpn)
        lat = _softmax_pv([s_c, s_n], [ckc, ckn]).astype(BF16)
        for p in range(MLA_HEADS // 2):
            wv = wv_ref[:, p * LANES:(p + 1) * LANES]
            even = _dot(lat[(2 * p) * tq:(2 * p + 1) * tq], wv)
            odd = _dot(lat[(2 * p + 1) * tq:(2 * p + 2) * tq], wv)
            o_ref[rows, p * LANES:(p + 1) * LANES] = jnp.where(lane < MLA_V, even, odd).astype(BF16)


def _attn_stream_call(qh, cache_ckv, cache_kpe, ckv_new, kpe_new, w_k, w_v, place, layer,
                      *, nseq, tq, past):
    t = qh.shape[0]
    qpos = past + np.arange(tq)
    if past % CHUNK or not ((qpos[None, :] // CHUNK) <= (qpos[:, None] // CHUNK)).all():
        raise NotImplementedError("new frames spanning several chunks")
    ns = math.gcd(nseq, STREAM_SEQS)
    return pl.pallas_call(
        functools.partial(_attn_stream_kernel, tq=tq, ns=ns),
        grid=(nseq // ns,),
        in_specs=[
            pl.BlockSpec((ns * tq, MQK), lambda b: (b, 0)),
            pl.BlockSpec((1, ns, past, MLA_KV_LORA), lambda b: (layer, b, 0, 0)),
            pl.BlockSpec((1, ns, MLA_ROPE, past), lambda b: (layer, b, 0, 0)),
            pl.BlockSpec((ns * tq, MLA_KV_LORA), lambda b: (b, 0)),
            pl.BlockSpec((ns * tq, MLA_ROPE), lambda b: (b, 0)),
            _layer_spec((MLA_KV_LORA, MQK), layer),
            _layer_spec((MLA_KV_LORA, MVW), layer),
            _const_spec((HEAD_BLOCK, MLA_ROPE)),
        ],
        out_specs=pl.BlockSpec((ns * tq, MVW), lambda b: (b, 0)),
        out_shape=jax.ShapeDtypeStruct((t, MVW), BF16),
        compiler_params=_params(("arbitrary",)),
        name="mla_attn_stream",
    )(qh, cache_ckv, cache_kpe, ckv_new, kpe_new, w_k, w_v, place)


def _prep_weights(w_in, w_gk2, b_gk2, w_uq, w_ukv):
    w_in_p = jnp.pad(w_in.astype(BF16), ((0, 0), (0, 0), (0, IN_W - IN_COLS)))
    w_gk_p = jnp.pad(w_gk2.astype(BF16), ((0, 0), (0, LANES - GLA_GATE_RANK), (0, 0)))
    b_gk_p = b_gk2.reshape(DEPTH, 1, GQK)
    uq = w_uq.astype(BF16).reshape(DEPTH, MLA_Q_LORA, MLA_HEADS, MLA_NOPE + MLA_ROPE)
    uq = jnp.pad(uq, ((0, 0), (0, 0), (0, 0), (0, HEAD_BLOCK - MLA_NOPE - MLA_ROPE)))
    w_uq_p = uq.reshape(DEPTH, MLA_Q_LORA, MQK)
    ukv = w_ukv.astype(BF16).reshape(DEPTH, MLA_KV_LORA, MLA_HEADS, MLA_NOPE + MLA_V)
    uk = jnp.pad(ukv[..., :MLA_NOPE], ((0, 0), (0, 0), (0, 0), (0, HEAD_BLOCK - MLA_NOPE)))
    w_k_p = uk.reshape(DEPTH, MLA_KV_LORA, MQK)
    w_v_p = ukv[..., MLA_NOPE:].reshape(DEPTH, MLA_KV_LORA, MVW)
    return w_in_p, w_gk_p, b_gk_p, w_uq_p, w_k_p, w_v_p


def _rope_tables(pos, reps):
    half = MLA_ROPE // 2
    inv = ROPE_BASE ** (-jnp.arange(half, dtype=F32) / half)
    ang = pos.astype(F32)[:, None] * inv[None, :]
    cos, sin = jnp.cos(ang), jnp.sin(ang)
    n = pos.shape[0]
    one, zero = jnp.ones((n, MLA_NOPE), F32), jnp.zeros((n, MLA_NOPE), F32)
    pad = jnp.zeros((n, HEAD_BLOCK - MLA_NOPE - MLA_ROPE), F32)
    c = jnp.concatenate([one, cos, cos, pad], axis=1)
    s = jnp.concatenate([zero, -sin, sin, pad], axis=1)
    q_scale = MLA_SCALE * LOG2E
    tabs = (c * q_scale, s * q_scale, c, s)
    return tuple(jnp.tile(a, (reps, 1)) for a in tabs)


def _kpe_placement():
    p = np.zeros((HEAD_BLOCK, MLA_ROPE), np.float32)
    p[MLA_NOPE + np.arange(MLA_ROPE), np.arange(MLA_ROPE)] = 1.0
    return jnp.asarray(p, BF16)


def _trunk(x, mods, seq0, wts, tabs, *, nseq, seq_len, tm, tm_ffn, nb, gla_tl, gla_chunk, s0,
           stream=None, ffn16=None):
    tps = max(seq_len // tm, 1) if nb == 1 else 1
    tile = dict(tm=tm, nb=nb, tps=tps, seq0=seq0)
    tile_ffn = dict(tile, tm=tm_ffn, tps=max(seq_len // tm_ffn, 1)) if nb == 1 else tile
    ckvs, kpes, states, cast = [], [], [], []
    for l in range(DEPTH):
        if stream is None:
            x = _ffn_call(x, mods, wts["ln_g"], ffn16[l][0], l, **tile_ffn)
        else:
            x, w1 = _ffn_stream_call(x, None, mods, wts["ln_g"], wts["ffn1_in"], wts["ffn1_out"],
                                     wts["final_g"], l, nb=nb, ffn_idx=0, final_norm=False)
        gq, gk, gv, gog, lg, qh, kh, vh, ckv, kpe, nrm = _mixin_call(
            x, mods, wts["ln_g"], wts["w_in"], wts["w_gk"], wts["b_gk"], wts["qg"], wts["w_uq"],
            wts["kvg"], wts["w_k"], wts["w_v"], tabs, l, **tile)
        og, st = _gla_call(gq, gk, gv, gog, lg, s0, wts["ng"], l,
                           nseq=nseq, seq_len=seq_len, tl=gla_tl, chunk=gla_chunk)
        last = l == DEPTH - 1
        if stream is None:
            om = _attn_prompt_call(qh, kh, vh, nrm, nseq=nseq, seq_len=seq_len, tq=min(ATTN_TQ, seq_len))
            x = _mix_ffn_call(x, og, om, mods, wts["w_out"], wts["ln_g"], ffn16[l][1], wts["final_g"],
                              l, final_norm=last, **tile_ffn)
        else:
            cache_ckv, cache_kpe_t, place_t = stream
            om = _attn_stream_call(qh, cache_ckv, cache_kpe_t, ckv, kpe, wts["w_k"], wts["w_v"],
                                   place_t, l, nseq=nseq, tq=seq_len, past=cache_ckv.shape[2])
            x, w2 = _ffn_stream_call(x, (og, om, wts["w_out"]), mods, wts["ln_g"], wts["ffn2_in"],
                                     wts["ffn2_out"], wts["final_g"], l, nb=nb, ffn_idx=2,
                                     final_norm=last)
            cast.append((w1, w2))
        ckvs.append(ckv)
        kpes.append(kpe)
        states.append(st)
    return x, jnp.stack(ckvs), jnp.stack(kpes), jnp.stack(states), cast


def kernel(x_prompt, x_sample, cache_ckv, cache_kpe, state_gla, c_prompt, c_sample, ln_g, w_ada, b_ada, w_ffn1_in, w_ffn1_out, w_ffn2_in, w_ffn2_out, w_in, w_gk2, b_gk2, gla_norm_g, mla_qnorm_g, w_uq, mla_kvnorm_g, w_ukv, w_out, final_g):
    bsz, seq, _ = x_prompt.shape
    dbs, dseq, _ = x_sample.shape
    past = cache_ckv.shape[2]

    w_in_p, w_gk_p, b_gk_p, w_uq_p, w_k_p, w_v_p = _prep_weights(w_in, w_gk2, b_gk2, w_uq, w_ukv)
    wts = {
        "ln_g": ln_g.reshape(DEPTH, 3, 1, D_MODEL), "final_g": final_g.reshape(1, D_MODEL),
        "ffn1_in": w_ffn1_in, "ffn1_out": w_ffn1_out, "ffn2_in": w_ffn2_in, "ffn2_out": w_ffn2_out,
        "w_in": w_in_p, "w_gk": w_gk_p, "b_gk": b_gk_p, "w_uq": w_uq_p, "w_k": w_k_p, "w_v": w_v_p,
        "qg": mla_qnorm_g.reshape(DEPTH, 1, MLA_Q_LORA),
        "kvg": mla_kvnorm_g.reshape(DEPTH, 1, MLA_KV_LORA),
        "ng": gla_norm_g.reshape(DEPTH, 1, GLA_DV),
        "w_out": w_out.astype(BF16),
    }
    mods = _ada_call(jnp.concatenate([c_sample, c_prompt], axis=0), w_ada, b_ada)
    mods = mods.reshape(DEPTH, N_MOD, dbs + bsz, 1, D_MODEL)

    y_s, ckv_s, kpe_s, gla_s, ffn16 = _trunk(
        x_sample.reshape(dbs * dseq, D_MODEL), mods, 0, wts,
        _rope_tables(past + jnp.arange(dseq), dbs),
        nseq=dbs, seq_len=dseq, tm=dbs * dseq, tm_ffn=dbs * dseq, nb=dbs, gla_tl=dseq, gla_chunk=dseq,
        s0=state_gla.astype(F32),
        stream=(cache_ckv, jnp.swapaxes(cache_kpe, 2, 3), _kpe_placement()))

    tm_p = min(512, seq)
    y_p, ckv_p, kpe_p, gla_p, _ = _trunk(
        x_prompt.reshape(bsz * seq, D_MODEL), mods, dbs, wts, _rope_tables(jnp.arange(seq), 1),
        nseq=bsz, seq_len=seq, tm=tm_p, tm_ffn=min(FFN_TM, seq), nb=1, gla_tl=min(GLA_TL, seq), gla_chunk=CHUNK,
        s0=jnp.zeros((DEPTH, bsz, GLA_HEADS, GLA_DK, GLA_DV), F32), ffn16=ffn16)

    return (y_p.reshape(bsz, seq, D_MODEL), y_s.reshape(dbs, dseq, D_MODEL),
            ckv_p.reshape(DEPTH, bsz, seq, MLA_KV_LORA), kpe_p.reshape(DEPTH, bsz, seq, MLA_ROPE),
            gla_p,
            ckv_s.reshape(DEPTH, dbs, dseq, MLA_KV_LORA), kpe_s.reshape(DEPTH, dbs, dseq, MLA_ROPE),
            gla_s)
```

```python
import functools
import math

import numpy as np
import jax
import jax.numpy as jnp
from jax import lax
from jax.experimental import pallas as pl
from jax.experimental.pallas import tpu as pltpu

F32 = jnp.float32
BF16 = jnp.bfloat16

D_MODEL = 1024
DEPTH = 4
CHUNK = 64
EPS = 1e-6
GLA_HEADS = 4
GLA_DK = 64
GLA_DV = 128
GLA_GATE_RANK = 16
GLA_GATE_NORM = 16.0
GLA_SUB = 16
EXP_CLAMP = 80.0
MLA_HEADS = 8
MLA_Q_LORA = 384
MLA_KV_LORA = 256
MLA_NOPE = 64
MLA_ROPE = 32
MLA_V = 64
MLA_SCALE = 1.0 / math.sqrt(MLA_NOPE + MLA_ROPE)
LOG2E = math.log2(math.e)
ROPE_BASE = 10000.0
D_FF = 2816
N_MOD = 9
LANES = 128
HEAD_BLOCK = 128
NEG_BIG = -1e30
SCORE_SAFE = 60.0
NRM_ROWS = 8

GQK = GLA_HEADS * GLA_DK
GVW = GLA_HEADS * GLA_DV
MQK = MLA_HEADS * HEAD_BLOCK
MVW = MLA_HEADS * MLA_V
O_GQ, O_GK, O_GV = 0, GQK, 2 * GQK
O_GLR = O_GV + GVW
O_KPE_COL = O_GLR + GLA_GATE_RANK + GVW + MLA_Q_LORA + MLA_KV_LORA
O_KPE_BLK = (O_KPE_COL // LANES) * LANES
IN_COLS = O_KPE_COL + MLA_ROPE
IN_W = -(-IN_COLS // LANES) * LANES
FFN_TF = 256
FFN_TP = 256
FFN_TM = 1024
GLA_TL = 1024
ATTN_TQ = 256
STREAM_SEQS = 4
CUMSUM_ROWS = 256
VMEM_LIMIT = 56 * 1024 * 1024


def _dot(a, b):
    return jnp.dot(a, b, preferred_element_type=F32)


def _dot_nt(a, b):
    return lax.dot_general(a, b, (((1,), (1,)), ((), ())), preferred_element_type=F32)


def _dot_tn(a, b):
    return lax.dot_general(a, b, (((0,), (0,)), ((), ())), preferred_element_type=F32)


def _rms(x, g):
    ms = jnp.mean(x * x, axis=-1, keepdims=True)
    return x * lax.rsqrt(ms + EPS) * g


def _modulate(x, g, sh, sc, nb):
    tm, d = x.shape
    y = _rms(x, g)
    if nb == 1:
        return y * (1.0 + sc[0]) + sh[0]
    y3 = y.reshape(nb, tm // nb, d)
    return (y3 * (1.0 + sc) + sh).reshape(tm, d)


def _gated_residual(x, y, coef, nb):
    tm, d = x.shape
    if nb == 1:
        return x + coef[0] * y
    return x + (coef * y.reshape(nb, tm // nb, d)).reshape(tm, d)


def _const_spec(shape):
    nd = len(shape)
    return pl.BlockSpec(shape, lambda *_: (0,) * nd, pipeline_mode=pl.Buffered(1))


def _layer_spec(shape, *lead):
    nd = len(shape)
    return pl.BlockSpec((None,) * len(lead) + tuple(shape), lambda *_: tuple(lead) + (0,) * nd,
                        pipeline_mode=pl.Buffered(1))


def _mod_spec(nmod, nb, tps, seq0, layer, blk):
    return pl.BlockSpec((None, nmod, nb, 1, D_MODEL), lambda i: (layer, blk, seq0 + i // tps, 0, 0))


def _params(sem):
    return pltpu.CompilerParams(dimension_semantics=sem, vmem_limit_bytes=VMEM_LIMIT)


ADA_GROUP = 3


def _ada_kernel(c_ref, w_ref, b_ref, o_ref):
    c = c_ref[...]
    sc = (c * jax.nn.sigmoid(c)).astype(BF16)
    for k in range(ADA_GROUP):
        w = w_ref[0, :, k * D_MODEL:(k + 1) * D_MODEL].astype(BF16)
        o_ref[0, k] = _dot(sc, w) + b_ref[0, k]


def _ada_call(c_all, w_ada, b_ada):
    nrow = c_all.shape[0]
    b4 = b_ada.reshape(DEPTH, N_MOD, 1, D_MODEL)
    return pl.pallas_call(
        _ada_kernel,
        grid=(DEPTH, N_MOD // ADA_GROUP),
        in_specs=[
            pl.BlockSpec((nrow, D_MODEL), lambda l, j: (0, 0)),
            pl.BlockSpec((1, D_MODEL, ADA_GROUP * D_MODEL), lambda l, j: (l, 0, j)),
            pl.BlockSpec((1, ADA_GROUP, 1, D_MODEL), lambda l, j: (l, j, 0, 0)),
        ],
        out_specs=pl.BlockSpec((1, ADA_GROUP, nrow, D_MODEL), lambda l, j: (l, j, 0, 0)),
        out_shape=jax.ShapeDtypeStruct((DEPTH, N_MOD, nrow, D_MODEL), F32),
        compiler_params=_params(("arbitrary", "arbitrary")),
        name="ada_mod",
    )(c_all, w_ada, b4)


def _ffn_body(x, sh, sc, gate, g, wa_ref, wb_ref, wo_ref, h_ref, nb):
    n = _modulate(x, g, sh, sc, nb).astype(BF16)
    for c in range(D_FF // FFN_TP):
        lo = c * FFN_TP
        a = _dot(n, wa_ref[:, lo:lo + FFN_TP])
        b = _dot(n, wb_ref[:, lo:lo + FFN_TP])
        h_ref[:, lo:lo + FFN_TP] = (a * jax.nn.sigmoid(a) * b).astype(BF16)
    y = _dot(h_ref[...], wo_ref[...])
    return _gated_residual(x, y, 0.5 * (1.0 + gate), nb)


def _mix_residual(x, og_ref, om_ref, gate, wmix_ref, nb):
    y = _dot(og_ref[...], wmix_ref[0:GVW, :]) + _dot(om_ref[...], wmix_ref[GVW:GVW + MVW, :])
    return _gated_residual(x, y, 1.0 + gate, nb)


def _ffn_kernel(x_ref, mod_ref, g_ref, wa_ref, wb_ref, wo_ref, o_ref, h_ref, *, nb):
    o_ref[...] = _ffn_body(x_ref[...], mod_ref[0], mod_ref[1], mod_ref[2], g_ref[...],
                           wa_ref, wb_ref, wo_ref, h_ref, nb)


def _mix_ffn_kernel(x_ref, og_ref, om_ref, gmix_ref, mod_ref, wmix_ref, g_ref, wa_ref, wb_ref, wo_ref,
                    fg_ref, o_ref, h_ref, *, nb, final_norm):
    x = _mix_residual(x_ref[...], og_ref, om_ref, gmix_ref[2], wmix_ref, nb)
    x = _ffn_body(x, mod_ref[0], mod_ref[1], mod_ref[2], g_ref[...], wa_ref, wb_ref, wo_ref, h_ref, nb)
    o_ref[...] = _rms(x, fg_ref[...]) if final_norm else x


def _ffn_weight_specs():
    return [_const_spec((D_MODEL, D_FF)), _const_spec((D_MODEL, D_FF)), _const_spec((D_FF, D_MODEL))]


def _ffn_call(x, mods, ln_g, w16, layer, *, tm, nb, tps, seq0):
    t = x.shape[0]
    return pl.pallas_call(
        functools.partial(_ffn_kernel, nb=nb),
        grid=(t // tm,),
        in_specs=[
            pl.BlockSpec((tm, D_MODEL), lambda i: (i, 0)),
            _mod_spec(3, nb, tps, seq0, layer, 0),
            _layer_spec((1, D_MODEL), layer, 0),
        ] + _ffn_weight_specs(),
        out_specs=pl.BlockSpec((tm, D_MODEL), lambda i: (i, 0)),
        out_shape=jax.ShapeDtypeStruct((t, D_MODEL), F32),
        scratch_shapes=[pltpu.VMEM((tm, D_FF), BF16)],
        compiler_params=_params(("arbitrary",)),
        name="ffn",
    )(x, mods, ln_g, *w16)


def _mix_ffn_call(x, og, om, mods, w_mix, ln_g, w16, fg, layer, *, tm, nb, tps, seq0, final_norm):
    t = x.shape[0]
    tok = lambda w: pl.BlockSpec((tm, w), lambda i: (i, 0))
    return pl.pallas_call(
        functools.partial(_mix_ffn_kernel, nb=nb, final_norm=final_norm),
        grid=(t // tm,),
        in_specs=[
            tok(D_MODEL), tok(GVW), tok(MVW),
            _mod_spec(3, nb, tps, seq0, layer, 1),
            _mod_spec(3, nb, tps, seq0, layer, 2),
            _layer_spec((GVW + MVW, D_MODEL), layer),
            _layer_spec((1, D_MODEL), layer, 2),
        ] + _ffn_weight_specs() + [_const_spec((1, D_MODEL))],
        out_specs=tok(D_MODEL),
        out_shape=jax.ShapeDtypeStruct((t, D_MODEL), F32),
        scratch_shapes=[pltpu.VMEM((tm, D_FF), BF16)],
        compiler_params=_params(("arbitrary",)),
        name="mix_ffn",
    )(x, og, om, mods, mods, w_mix, ln_g, *w16, fg)


def _ffn_stream_kernel(*refs, nb, has_mix, final_norm):
    if has_mix:
        x_ref, og_ref, om_ref, gmix_ref, mod_ref, wmix_ref, *refs = refs
    else:
        x_ref, mod_ref, *refs = refs
    (g_ref, wa_ref, wb_ref, wo_ref, fg_ref,
     y_ref, wa16_ref, wb16_ref, wo16_ref, xs_ref, n_ref, acc_ref) = refs
    c = pl.program_id(0)

    @pl.when(c == 0)
    def _():
        x = x_ref[...]
        if has_mix:
            x = _mix_residual(x, og_ref, om_ref, gmix_ref[2], wmix_ref, nb)
        xs_ref[...] = x
        n_ref[...] = _modulate(x, g_ref[...], mod_ref[0], mod_ref[1], nb).astype(BF16)
        acc_ref[...] = jnp.zeros_like(acc_ref)

    wa, wb, wo = wa_ref[...].astype(BF16), wb_ref[...].astype(BF16), wo_ref[...].astype(BF16)
    wa16_ref[...] = wa
    wb16_ref[...] = wb
    wo16_ref[...] = wo
    n = n_ref[...]
    a = _dot(n, wa)
    b = _dot(n, wb)
    acc_ref[...] += _dot((a * jax.nn.sigmoid(a) * b).astype(BF16), wo)

    @pl.when(c == pl.num_programs(0) - 1)
    def _():
        x = _gated_residual(xs_ref[...], acc_ref[...], 0.5 * (1.0 + mod_ref[2]), nb)
        y_ref[...] = _rms(x, fg_ref[...]) if final_norm else x


def _ffn_stream_call(x, mix, mods, ln_g, w_in, w_out, fg, layer, *, nb, ffn_idx, final_norm):
    tm = x.shape[0]
    nchunk = D_FF // FFN_TF
    fixed = lambda shape: pl.BlockSpec(shape, lambda c: (0,) * len(shape))
    mod = lambda blk: pl.BlockSpec((None, 3, nb, 1, D_MODEL), lambda c: (layer, blk, 0, 0, 0))
    if mix is None:
        args, specs = [x, mods], [fixed((tm, D_MODEL)), mod(ffn_idx)]
    else:
        og, om, w_mix = mix
        args = [x, og, om, mods, mods, w_mix]
        specs = [fixed((tm, D_MODEL)), fixed((tm, GVW)), fixed((tm, MVW)), mod(1), mod(ffn_idx),
                 _layer_spec((GVW + MVW, D_MODEL), layer)]
    args += [ln_g, w_in, w_in, w_out, fg]
    specs += [
        _layer_spec((1, D_MODEL), layer, ffn_idx),
        pl.BlockSpec((None, D_MODEL, FFN_TF), lambda c: (layer, 0, c)),
        pl.BlockSpec((None, D_MODEL, FFN_TF), lambda c: (layer, 0, nchunk + c)),
        pl.BlockSpec((None, FFN_TF, D_MODEL), lambda c: (layer, c, 0)),
        fixed((1, D_MODEL)),
    ]
    y, wa, wb, wo = pl.pallas_call(
        functools.partial(_ffn_stream_kernel, nb=nb, has_mix=mix is not None, final_norm=final_norm),
        grid=(nchunk,),
        in_specs=specs,
        out_specs=[fixed((tm, D_MODEL)),
                   pl.BlockSpec((D_MODEL, FFN_TF), lambda c: (0, c)),
                   pl.BlockSpec((D_MODEL, FFN_TF), lambda c: (0, c)),
                   pl.BlockSpec((FFN_TF, D_MODEL), lambda c: (c, 0))],
        out_shape=[jax.ShapeDtypeStruct((tm, D_MODEL), F32),
                   jax.ShapeDtypeStruct((D_MODEL, D_FF), BF16),
                   jax.ShapeDtypeStruct((D_MODEL, D_FF), BF16),
                   jax.ShapeDtypeStruct((D_FF, D_MODEL), BF16)],
        scratch_shapes=[pltpu.VMEM((tm, D_MODEL), F32), pltpu.VMEM((tm, D_MODEL), BF16),
                        pltpu.VMEM((tm, D_MODEL), F32)],
        compiler_params=_params(("arbitrary",)),
        name="ffn_stream",
    )(*args)
    return y, (wa, wb, wo)


def _swap_rope_halves(xb, lane):
    fwd = pltpu.roll(xb, LANES - MLA_ROPE // 2, axis=1)
    bwd = pltpu.roll(xb, MLA_ROPE // 2, axis=1)
    return jnp.where(lane < MLA_NOPE + MLA_ROPE // 2, fwd, bwd)


N_SHIFT = (GVW + MLA_Q_LORA + MLA_KV_LORA) // LANES


R_GOG = O_GLR
R_CQ = R_GOG + GVW
R_CKV = R_CQ + MLA_Q_LORA
R_TAIL = R_CKV + MLA_KV_LORA
TAIL_GLR = MLA_ROPE
Q_UNSCALE = 1.0 / (MLA_SCALE * LOG2E)


def _mixin_kernel(x_ref, mod_ref, g_ref, win_ref, wgk_ref, bgk_ref, qg_ref, wuq_ref,
                  kvg_ref, wk_ref, wv_ref, ones_ref, cq_ref, sq_ref, ck_ref, sk_ref,
                  gq_ref, gk_ref, gv_ref, gog_ref, lg_ref, qh_ref, kh_ref, vh_ref,
                  ckv_ref, kpe_ref, nrm_ref, wre_ref, *, nb, kpe_rows):
    @pl.when(pl.program_id(0) == 0)
    def _():
        keep = LANES - GLA_GATE_RANK
        wl = lax.broadcasted_iota(jnp.int32, (D_MODEL, LANES), 1)
        wre_ref[:, 0:O_GLR] = win_ref[:, 0:O_GLR]
        gate_blk = win_ref[:, O_GLR:O_GLR + LANES].astype(F32)
        prev = pltpu.roll(gate_blk, keep, axis=1)
        for j in range(N_SHIFT):
            c0 = O_GLR + LANES * (j + 1)
            nxt = pltpu.roll(win_ref[:, c0:c0 + LANES].astype(F32), keep, axis=1)
            wre_ref[:, R_GOG + LANES * j:R_GOG + LANES * (j + 1)] = (
                jnp.where(wl < keep, prev, nxt).astype(BF16))
            prev = nxt
        tail = jnp.where(wl < keep, prev, 0.0)
        glr = pltpu.roll(gate_blk, TAIL_GLR, axis=1)
        tail = jnp.where((wl >= TAIL_GLR) & (wl < TAIL_GLR + GLA_GATE_RANK), glr, tail)
        wre_ref[:, R_TAIL:R_TAIL + LANES] = tail.astype(BF16)

    x = x_ref[...]
    tm = x.shape[0]
    n = _modulate(x, g_ref[...], mod_ref[0], mod_ref[1], nb).astype(BF16)
    lane = lax.broadcasted_iota(jnp.int32, (tm, LANES), 1)
    h = _dot(n, wre_ref[...])
    gq_ref[...] = h[:, O_GQ:O_GQ + GQK] * (GLA_DK ** -0.5)
    gk_ref[...] = h[:, O_GK:O_GK + GQK]
    gv_ref[...] = h[:, O_GV:O_GV + GVW]
    gog_ref[...] = h[:, R_GOG:R_GOG + GVW]
    tail = h[:, R_TAIL:R_TAIL + LANES]
    z = _dot(tail.astype(BF16), wgk_ref[...]) + bgk_ref[...]
    lsig = jnp.minimum(z, 0.0) - jnp.log1p(jnp.exp(-jnp.abs(z)))
    lg_ref[...] = lsig / GLA_GATE_NORM
    cqn = _rms(h[:, R_CQ:R_CQ + MLA_Q_LORA], qg_ref[...]).astype(BF16)
    q = _dot(cqn, wuq_ref[...])
    cq, sq = cq_ref[...], sq_ref[...]
    qr = jnp.concatenate(
        [q[:, hd * HEAD_BLOCK:(hd + 1) * HEAD_BLOCK] * cq
         + _swap_rope_halves(q[:, hd * HEAD_BLOCK:(hd + 1) * HEAD_BLOCK], lane) * sq
         for hd in range(MLA_HEADS)], axis=1)
    qh_ref[...] = qr.astype(BF16)
    ckv = _rms(h[:, R_CKV:R_CKV + MLA_KV_LORA], kvg_ref[...])
    ckv_ref[...] = ckv
    kb = pltpu.roll(tail, MLA_NOPE, axis=1)
    kb = jnp.where((lane >= MLA_NOPE) & (lane < MLA_NOPE + MLA_ROPE), kb, 0.0)
    kpe = kb * ck_ref[...] + _swap_rope_halves(kb, lane) * sk_ref[...]
    kpe0 = pltpu.roll(kpe, LANES - MLA_NOPE, axis=1)
    kpe_ref[...] = kpe0.T[:MLA_ROPE, :] if kpe_rows else kpe0[:, :MLA_ROPE]
    ckv16 = ckv.astype(BF16)
    kn = _dot(ckv16, wk_ref[...])
    kr = jnp.concatenate(
        [kn[:, hd * HEAD_BLOCK:(hd + 1) * HEAD_BLOCK] + kpe for hd in range(MLA_HEADS)], axis=1)
    kh_ref[...] = kr.astype(BF16)
    vh_ref[...] = _dot(ckv16, wv_ref[...]).astype(BF16)
    u = _dot((qr * qr * Q_UNSCALE + kr * kr * (1.0 / Q_UNSCALE)).astype(BF16), ones_ref[...])
    nrm_ref[0] = jnp.concatenate(
        [jnp.max(u, axis=0, keepdims=True), jnp.zeros((NRM_ROWS - 1, LANES), F32)], axis=0)


def _mixin_call(x, mods, ln_g, w_in, w_gk, b_gk, qg, w_uq, kvg, w_k, w_v, tabs, layer, *, tm, nb, tps, seq0):
    t = x.shape[0]
    tok = lambda w: pl.BlockSpec((tm, w), lambda i: (i, 0))
    tab = pl.BlockSpec((tm, LANES), lambda i: (i % tps, 0))
    out_w = [(GQK, F32), (GQK, F32), (GVW, F32), (GVW, F32), (GQK, F32),
             (MQK, BF16), (MQK, BF16), (MVW, BF16), (MLA_KV_LORA, F32)]
    kpe_rows = nb == 1
    if kpe_rows:
        kpe_spec = pl.BlockSpec((None, MLA_ROPE, tm), lambda i: (i // tps, 0, i % tps))
        kpe_shape = jax.ShapeDtypeStruct((t // (tm * tps), MLA_ROPE, tm * tps), F32)
    else:
        kpe_spec, kpe_shape = tok(MLA_ROPE), jax.ShapeDtypeStruct((t, MLA_ROPE), F32)
    nrm_spec = pl.BlockSpec((1, NRM_ROWS, LANES), lambda i: (i, 0, 0))
    nrm_shape = jax.ShapeDtypeStruct((t // tm, NRM_ROWS, LANES), F32)
    head_ones = np.zeros((MQK, LANES), np.float32)
    head_ones[np.arange(MQK), np.arange(MQK) // HEAD_BLOCK] = 1.0
    return pl.pallas_call(
        functools.partial(_mixin_kernel, nb=nb, kpe_rows=kpe_rows),
        grid=(t // tm,),
        in_specs=[
            tok(D_MODEL),
            _mod_spec(3, nb, tps, seq0, layer, 1),
            _layer_spec((1, D_MODEL), layer, 1),
            _layer_spec((D_MODEL, IN_W), layer),
            _layer_spec((LANES, GQK), layer),
            _layer_spec((1, GQK), layer),
            _layer_spec((1, MLA_Q_LORA), layer),
            _layer_spec((MLA_Q_LORA, MQK), layer),
            _layer_spec((1, MLA_KV_LORA), layer),
            _layer_spec((MLA_KV_LORA, MQK), layer),
            _layer_spec((MLA_KV_LORA, MVW), layer),
            _const_spec((MQK, LANES)),
            tab, tab, tab, tab,
        ],
        out_specs=[tok(w) for w, _ in out_w] + [kpe_spec, nrm_spec],
        out_shape=[jax.ShapeDtypeStruct((t, w), dt) for w, dt in out_w] + [kpe_shape, nrm_shape],
        scratch_shapes=[pltpu.VMEM((D_MODEL, IN_W), BF16)],
        compiler_params=_params(("arbitrary",)),
        name="mixer_in",
    )(x, mods, ln_g, w_in, w_gk, b_gk, qg, w_uq, kvg, w_k, w_v, jnp.asarray(head_ones, BF16), *tabs)


def _gla_kernel(q_ref, k_ref, v_ref, og_ref, lg_ref, s0_ref, ng_ref, o_ref, sout_ref, st_ref,
                *, chunk, nch, ns):
    j = pl.program_id(1)
    nsub = chunk // GLA_SUB
    tl = chunk * nch
    lane_head = lax.broadcasted_iota(jnp.int32, (chunk, GQK), 1) // GLA_DK

    def stack_heads(a):
        return jnp.concatenate(
            [jnp.where(lane_head == hd, a, 0.0) for hd in range(GLA_HEADS)], axis=0).astype(BF16)

    @pl.when(j == 0)
    def _():
        for s in range(ns):
            for hd in range(GLA_HEADS):
                rows = [jnp.zeros((GLA_DK, GLA_DV), F32)] * GLA_HEADS
                rows[hd] = s0_ref[s, hd]
                st_ref[s, hd * GLA_DV:(hd + 1) * GLA_DV, :] = jnp.concatenate(rows, axis=0).T

    grp = min(tl, CUMSUM_ROWS)
    ri = lax.broadcasted_iota(jnp.int32, (grp, grp), 0)
    ci = lax.broadcasted_iota(jnp.int32, (grp, grp), 1)
    tri = jnp.where((ci <= ri) & (ci // chunk == ri // chunk), 1.0, 0.0).astype(BF16)
    b_parts = []
    for g in range(ns * tl // grp):
        lg = lg_ref[g * grp:(g + 1) * grp, :]
        p0 = lg.astype(BF16)
        r1 = lg - p0.astype(F32)
        p1 = r1.astype(BF16)
        p2 = (r1 - p1.astype(F32)).astype(BF16)
        b_parts.append(_dot(tri, p0) + _dot(tri, p1) + _dot(tri, p2))
    b_all = b_parts[0] if len(b_parts) == 1 else jnp.concatenate(b_parts, axis=0)

    rt = lax.broadcasted_iota(jnp.int32, (GLA_HEADS * chunk, nsub * chunk), 0) % chunk
    cc = lax.broadcasted_iota(jnp.int32, (GLA_HEADS * chunk, nsub * chunk), 1)
    keep = (cc // chunk == rt // GLA_SUB) & (cc % chunk <= rt)
    ng = ng_ref[...]

    def v_of(r0):
        return v_ref[r0:r0 + chunk, :].astype(BF16)

    start = {}
    for s in range(ns):
        st = st_ref[s]
        for c in range(nch):
            r0 = s * tl + c * chunk
            b = b_all[r0:r0 + chunk]
            kl = k_ref[r0:r0 + chunk, :] * jnp.exp(b[chunk - 1:chunk] - b)
            g_end = jnp.exp(b[chunk - 8:chunk])[7:8]
            start[r0] = st
            st = st * g_end + _dot_tn(v_of(r0), kl.astype(BF16))
        st_ref[s] = st

    for r0 in range(0, ns * tl, chunk):
        b = b_all[r0:r0 + chunk]
        q = q_ref[r0:r0 + chunk, :]
        k = k_ref[r0:r0 + chunk, :]
        r_own = jnp.broadcast_to(b[0:1], (chunk, GQK)) if nsub == 1 else jnp.concatenate(
            [jnp.broadcast_to(b[i * GLA_SUB:i * GLA_SUB + 1], (GLA_SUB, GQK)) for i in range(nsub)],
            axis=0)
        qt = q * jnp.exp(b - r_own)
        qe = q * jnp.exp(b)
        k_rel = [(k * jnp.exp(jnp.minimum(b[i * GLA_SUB:i * GLA_SUB + 1] - b, EXP_CLAMP))).astype(BF16)
                 for i in range(nsub)]
        att = _dot_nt(stack_heads(qt), jnp.concatenate(k_rel, axis=0))
        att = jnp.where(keep, att, 0.0).astype(BF16)
        v = v_of(r0)
        o_all = (_dot(att, jnp.concatenate([v] * nsub, axis=0))
                 + _dot_nt(stack_heads(qe), start[r0].astype(BF16)))
        for hd in range(GLA_HEADS):
            cols = slice(hd * GLA_DV, (hd + 1) * GLA_DV)
            o = o_all[hd * chunk:(hd + 1) * chunk, cols]
            og = og_ref[r0:r0 + chunk, cols]
            o_ref[r0:r0 + chunk, cols] = (_rms(o, ng) * (og * jax.nn.sigmoid(og))).astype(BF16)

    @pl.when(j == pl.num_programs(1) - 1)
    def _():
        for s in range(ns):
            for hd in range(GLA_HEADS):
                st_h = st_ref[s, hd * GLA_DV:(hd + 1) * GLA_DV, :]
                sout_ref[s, hd] = st_h.T[hd * GLA_DK:(hd + 1) * GLA_DK, :]


def _gla_call(gq, gk, gv, gog, lg, s0, ng, layer, *, nseq, seq_len, tl, chunk):
    t = gq.shape[0]
    tps = seq_len // tl
    ns = math.gcd(nseq, STREAM_SEQS) if tps == 1 else 1
    tok = lambda w: pl.BlockSpec((ns * tl, w), lambda b, j: (b * tps + j, 0))
    st_spec = pl.BlockSpec((ns, GLA_HEADS, GLA_DK, GLA_DV), lambda b, j: (b, 0, 0, 0))
    s0_spec = pl.BlockSpec((None, ns, GLA_HEADS, GLA_DK, GLA_DV), lambda b, j: (layer, b, 0, 0, 0))
    return pl.pallas_call(
        functools.partial(_gla_kernel, chunk=chunk, nch=tl // chunk, ns=ns),
        grid=(nseq // ns, tps),
        in_specs=[tok(GQK), tok(GQK), tok(GVW), tok(GVW), tok(GQK), s0_spec,
                  _layer_spec((1, GLA_DV), layer)],
        out_specs=[tok(GVW), st_spec],
        out_shape=[jax.ShapeDtypeStruct((t, GVW), BF16),
                   jax.ShapeDtypeStruct((nseq, GLA_HEADS, GLA_DK, GLA_DV), F32)],
        scratch_shapes=[pltpu.VMEM((ns, GVW, GQK), F32)],
        compiler_params=_params(("arbitrary", "arbitrary")),
        name="gla",
    )(gq, gk, gv, gog, lg, s0, ng)


def _softmax_pv(parts, vs, shift_by_max=True):
    if shift_by_max:
        m = functools.reduce(jnp.maximum, [jnp.max(s, axis=-1, keepdims=True) for s in parts])
    acc, den = None, None
    for s, v in zip(parts, vs):
        p = jnp.exp2(s - m) if shift_by_max else jnp.exp2(s)
        d = jnp.sum(p, axis=-1, keepdims=True)
        o = _dot(p.astype(BF16), v)
        acc = o if acc is None else acc + o
        den = d if den is None else den + d
    return acc / den


def _attn_prompt_kernel(q_ref, k_ref, v_ref, nrm_ref, o_ref, *, seq_len, tq):
    pair = pl.program_id(1)
    lane = lax.broadcasted_iota(jnp.int32, (tq, LANES), 1)
    qi = lax.broadcasted_iota(jnp.int32, (tq, tq), 0) // CHUNK
    ki = lax.broadcasted_iota(jnp.int32, (tq, tq), 1) // CHUNK
    diag_ok = ki <= qi

    def attend(shift_by_max):
        for i in range(seq_len // tq):
            r0 = i * tq
            outs = []
            for hh in range(2):
                cs = slice(hh * HEAD_BLOCK, (hh + 1) * HEAD_BLOCK)
                q = q_ref[r0:r0 + tq, cs]
                s_d = jnp.where(diag_ok, _dot_nt(q, k_ref[r0:r0 + tq, cs]), NEG_BIG)
                parts, vs = [s_d], [v_ref[r0:r0 + tq, :]]
                if i > 0:
                    parts.append(_dot_nt(q, k_ref[0:r0, cs]))
                    vs.append(v_ref[0:r0, :])
                outs.append(_softmax_pv(parts, vs, shift_by_max))
            o_ref[r0:r0 + tq, :] = jnp.where(lane < MLA_V, outs[0], outs[1]).astype(BF16)

    bound = 0.5 * jnp.max(nrm_ref[...], axis=0)[0:1]
    head_lane = lax.broadcasted_iota(jnp.int32, (1, LANES), 1) // 2
    safe = jnp.max(jnp.where(head_lane == pair, bound, 0.0)) <= SCORE_SAFE
    pl.when(safe)(functools.partial(attend, False))
    pl.when(jnp.logical_not(safe))(functools.partial(attend, True))


def _attn_prompt_call(qh, kh, vh, nrm, *, nseq, seq_len, tq):
    t = qh.shape[0]
    npair = MLA_HEADS // 2
    tps = nrm.shape[0] // nseq
    return pl.pallas_call(
        functools.partial(_attn_prompt_kernel, seq_len=seq_len, tq=tq),
        grid=(nseq, npair),
        in_specs=[
            pl.BlockSpec((seq_len, 2 * HEAD_BLOCK), lambda b, p: (b, p)),
            pl.BlockSpec((seq_len, 2 * HEAD_BLOCK), lambda b, p: (b, p)),
            pl.BlockSpec((seq_len, 2 * MLA_V), lambda b, p: (b, p)),
            pl.BlockSpec((tps, NRM_ROWS, LANES), lambda b, p: (b, 0, 0)),
        ],
        out_specs=pl.BlockSpec((seq_len, 2 * MLA_V), lambda b, p: (b, p)),
        out_shape=jax.ShapeDtypeStruct((t, MVW), BF16),
        compiler_params=_params(("arbitrary", "arbitrary")),
        name="mla_attn_prompt",
    )(qh, kh, vh, nrm)


def _attn_stream_kernel(q_ref, ckc_ref, kpc_ref, ckn_ref, kpn_ref, wk_ref, wv_ref, place_ref,
                        o_ref, *, tq, ns):
    place_t = place_ref[...]
    blocks = [slice(hd * HEAD_BLOCK, (hd + 1) * HEAD_BLOCK) for hd in range(MLA_HEADS)]
    lane = lax.broadcasted_iota(jnp.int32, (tq, LANES), 1)
    for s in range(ns):
        rows = slice(s * tq, (s + 1) * tq)
        q = q_ref[rows, :]
        ckc = ckc_ref[0, s].astype(BF16)
        kpc_t = _dot(place_t, kpc_ref[0, s].astype(BF16)).astype(BF16)
        ckn = ckn_ref[rows, :].astype(BF16)
        kpn = _dot_nt(kpn_ref[rows, :].astype(BF16), place_t).astype(BF16)
        q_abs = jnp.concatenate(
            [_dot_nt(q[:, bs], wk_ref[:, bs]) for bs in blocks], axis=0).astype(BF16)
        q_blk = jnp.concatenate([q[:, bs] for bs in blocks], axis=0)
        s_c = _dot_nt(q_abs, ckc) + _dot(q_blk, kpc_t)
        s_n = _dot_nt(q_abs, ckn) + _dot_nt(q_blk, kpn)
        lat = _softmax_pv([s_c, s_n], [ckc, ckn]).astype(BF16)
        for p in range(MLA_HEADS // 2):
            wv = wv_ref[:, p * LANES:(p + 1) * LANES]
            even = _dot(lat[(2 * p) * tq:(2 * p + 1) * tq], wv)
            odd = _dot(lat[(2 * p + 1) * tq:(2 * p + 2) * tq], wv)
            o_ref[rows, p * LANES:(p + 1) * LANES] = jnp.where(lane < MLA_V, even, odd).astype(BF16)


def _attn_stream_call(qh, cache_ckv, cache_kpe, ckv_new, kpe_new, w_k, w_v, place, layer,
                      *, nseq, tq, past):
    t = qh.shape[0]
    qpos = past + np.arange(tq)
    if past % CHUNK or not ((qpos[None, :] // CHUNK) <= (qpos[:, None] // CHUNK)).all():
        raise NotImplementedError("new frames spanning several chunks")
    ns = math.gcd(nseq, STREAM_SEQS)
    return pl.pallas_call(
        functools.partial(_attn_stream_kernel, tq=tq, ns=ns),
        grid=(nseq // ns,),
        in_specs=[
            pl.BlockSpec((ns * tq, MQK), lambda b: (b, 0)),
            pl.BlockSpec((1, ns, past, MLA_KV_LORA), lambda b: (layer, b, 0, 0)),
            pl.BlockSpec((1, ns, MLA_ROPE, past), lambda b: (layer, b, 0, 0)),
            pl.BlockSpec((ns * tq, MLA_KV_LORA), lambda b: (b, 0)),
            pl.BlockSpec((ns * tq, MLA_ROPE), lambda b: (b, 0)),
            _layer_spec((MLA_KV_LORA, MQK), layer),
            _layer_spec((MLA_KV_LORA, MVW), layer),
            _const_spec((HEAD_BLOCK, MLA_ROPE)),
        ],
        out_specs=pl.BlockSpec((ns * tq, MVW), lambda b: (b, 0)),
        out_shape=jax.ShapeDtypeStruct((t, MVW), BF16),
        compiler_params=_params(("arbitrary",)),
        name="mla_attn_stream",
    )(qh, cache_ckv, cache_kpe, ckv_new, kpe_new, w_k, w_v, place)


def _prep_weights(w_in, w_gk2, b_gk2, w_uq, w_ukv):
    w_in_p = jnp.pad(w_in.astype(BF16), ((0, 0), (0, 0), (0, IN_W - IN_COLS)))
    w_gk_p = jnp.pad(w_gk2.astype(BF16),
                     ((0, 0), (TAIL_GLR, LANES - TAIL_GLR - GLA_GATE_RANK), (0, 0)))
    b_gk_p = b_gk2.reshape(DEPTH, 1, GQK)
    uq = w_uq.astype(BF16).reshape(DEPTH, MLA_Q_LORA, MLA_HEADS, MLA_NOPE + MLA_ROPE)
    uq = jnp.pad(uq, ((0, 0), (0, 0), (0, 0), (0, HEAD_BLOCK - MLA_NOPE - MLA_ROPE)))
    w_uq_p = uq.reshape(DEPTH, MLA_Q_LORA, MQK)
    ukv = w_ukv.astype(BF16).reshape(DEPTH, MLA_KV_LORA, MLA_HEADS, MLA_NOPE + MLA_V)
    uk = jnp.pad(ukv[..., :MLA_NOPE], ((0, 0), (0, 0), (0, 0), (0, HEAD_BLOCK - MLA_NOPE)))
    w_k_p = uk.reshape(DEPTH, MLA_KV_LORA, MQK)
    w_v_p = ukv[..., MLA_NOPE:].reshape(DEPTH, MLA_KV_LORA, MVW)
    return w_in_p, w_gk_p, b_gk_p, w_uq_p, w_k_p, w_v_p


def _rope_tables(pos, reps):
    half = MLA_ROPE // 2
    inv = ROPE_BASE ** (-jnp.arange(half, dtype=F32) / half)
    ang = pos.astype(F32)[:, None] * inv[None, :]
    cos, sin = jnp.cos(ang), jnp.sin(ang)
    n = pos.shape[0]
    one, zero = jnp.ones((n, MLA_NOPE), F32), jnp.zeros((n, MLA_NOPE), F32)
    pad = jnp.zeros((n, HEAD_BLOCK - MLA_NOPE - MLA_ROPE), F32)
    c = jnp.concatenate([one, cos, cos, pad], axis=1)
    s = jnp.concatenate([zero, -sin, sin, pad], axis=1)
    q_scale = MLA_SCALE * LOG2E
    tabs = (c * q_scale, s * q_scale, c, s)
    return tuple(jnp.tile(a, (reps, 1)) for a in tabs)


def _kpe_placement():
    p = np.zeros((HEAD_BLOCK, MLA_ROPE), np.float32)
    p[MLA_NOPE + np.arange(MLA_ROPE), np.arange(MLA_ROPE)] = 1.0
    return jnp.asarray(p, BF16)


def _trunk(x, mods, seq0, wts, tabs, *, nseq, seq_len, tm, tm_ffn, nb, gla_tl, gla_chunk, s0,
           stream=None, ffn16=None):
    tps = max(seq_len // tm, 1) if nb == 1 else 1
    tile = dict(tm=tm, nb=nb, tps=tps, seq0=seq0)
    tile_ffn = dict(tile, tm=tm_ffn, tps=max(seq_len // tm_ffn, 1)) if nb == 1 else tile
    ckvs, kpes, states, cast = [], [], [], []
    for l in range(DEPTH):
        if stream is None:
            x = _ffn_call(x, mods, wts["ln_g"], ffn16[l][0], l, **tile_ffn)
        else:
            x, w1 = _ffn_stream_call(x, None, mods, wts["ln_g"], wts["ffn1_in"], wts["ffn1_out"],
                                     wts["final_g"], l, nb=nb, ffn_idx=0, final_norm=False)
        gq, gk, gv, gog, lg, qh, kh, vh, ckv, kpe, nrm = _mixin_call(
            x, mods, wts["ln_g"], wts["w_in"], wts["w_gk"], wts["b_gk"], wts["qg"], wts["w_uq"],
            wts["kvg"], wts["w_k"], wts["w_v"], tabs, l, **tile)
        og, st = _gla_call(gq, gk, gv, gog, lg, s0, wts["ng"], l,
                           nseq=nseq, seq_len=seq_len, tl=gla_tl, chunk=gla_chunk)
        last = l == DEPTH - 1
        if stream is None:
            om = _attn_prompt_call(qh, kh, vh, nrm, nseq=nseq, seq_len=seq_len, tq=min(ATTN_TQ, seq_len))
            x = _mix_ffn_call(x, og, om, mods, wts["w_out"], wts["ln_g"], ffn16[l][1], wts["final_g"],
                              l, final_norm=last, **tile_ffn)
        else:
            cache_ckv, cache_kpe_t, place_t = stream
            om = _attn_stream_call(qh, cache_ckv, cache_kpe_t, ckv, kpe, wts["w_k"], wts["w_v"],
                                   place_t, l, nseq=nseq, tq=seq_len, past=cache_ckv.shape[2])
            x, w2 = _ffn_stream_call(x, (og, om, wts["w_out"]), mods, wts["ln_g"], wts["ffn2_in"],
                                     wts["ffn2_out"], wts["final_g"], l, nb=nb, ffn_idx=2,
                                     final_norm=last)
            cast.append((w1, w2))
        ckvs.append(ckv)
        kpes.append(kpe)
        states.append(st)
    return x, jnp.stack(ckvs), jnp.stack(kpes), jnp.stack(states), cast


def kernel(x_prompt, x_sample, cache_ckv, cache_kpe, state_gla, c_prompt, c_sample, ln_g, w_ada, b_ada, w_ffn1_in, w_ffn1_out, w_ffn2_in, w_ffn2_out, w_in, w_gk2, b_gk2, gla_norm_g, mla_qnorm_g, w_uq, mla_kvnorm_g, w_ukv, w_out, final_g):
    bsz, seq, _ = x_prompt.shape
    dbs, dseq, _ = x_sample.shape
    past = cache_ckv.shape[2]

    w_in_p, w_gk_p, b_gk_p, w_uq_p, w_k_p, w_v_p = _prep_weights(w_in, w_gk2, b_gk2, w_uq, w_ukv)
    wts = {
        "ln_g": ln_g.reshape(DEPTH, 3, 1, D_MODEL), "final_g": final_g.reshape(1, D_MODEL),
        "ffn1_in": w_ffn1_in, "ffn1_out": w_ffn1_out, "ffn2_in": w_ffn2_in, "ffn2_out": w_ffn2_out,
        "w_in": w_in_p, "w_gk": w_gk_p, "b_gk": b_gk_p, "w_uq": w_uq_p, "w_k": w_k_p, "w_v": w_v_p,
        "qg": mla_qnorm_g.reshape(DEPTH, 1, MLA_Q_LORA),
        "kvg": mla_kvnorm_g.reshape(DEPTH, 1, MLA_KV_LORA),
        "ng": gla_norm_g.reshape(DEPTH, 1, GLA_DV),
        "w_out": w_out.astype(BF16),
    }
    mods = _ada_call(jnp.concatenate([c_sample, c_prompt], axis=0), w_ada, b_ada)
    mods = mods.reshape(DEPTH, N_MOD, dbs + bsz, 1, D_MODEL)

    y_s, ckv_s, kpe_s, gla_s, ffn16 = _trunk(
        x_sample.reshape(dbs * dseq, D_MODEL), mods, 0, wts,
        _rope_tables(past + jnp.arange(dseq), dbs),
        nseq=dbs, seq_len=dseq, tm=dbs * dseq, tm_ffn=dbs * dseq, nb=dbs, gla_tl=dseq, gla_chunk=dseq,
        s0=state_gla.astype(F32),
        stream=(cache_ckv, jnp.swapaxes(cache_kpe, 2, 3), _kpe_placement()))

    tm_p = min(512, seq)
    y_p, ckv_p, kpe_p, gla_p, _ = _trunk(
        x_prompt.reshape(bsz * seq, D_MODEL), mods, dbs, wts, _rope_tables(jnp.arange(seq), 1),
        nseq=bsz, seq_len=seq, tm=tm_p, tm_ffn=min(FFN_TM, seq), nb=1, gla_tl=min(GLA_TL, seq), gla_chunk=CHUNK,
        s0=jnp.zeros((DEPTH, bsz, GLA_HEADS, GLA_DK, GLA_DV), F32), ffn16=ffn16)

    return (y_p.reshape(bsz, seq, D_MODEL), y_s.reshape(dbs, dseq, D_MODEL),
            ckv_p.reshape(DEPTH, bsz, seq, MLA_KV_LORA), jnp.swapaxes(kpe_p, 2, 3),
            gla_p,
            ckv_s.reshape(DEPTH, dbs, dseq, MLA_KV_LORA), kpe_s.reshape(DEPTH, dbs, dseq, MLA_ROPE),
            gla_s)
```

```python
import functools
import math

import numpy as np
import jax
import jax.numpy as jnp
from jax import lax
from jax.experimental import pallas as pl
from jax.experimental.pallas import tpu as pltpu

F32 = jnp.float32
BF16 = jnp.bfloat16

D_MODEL = 1024
DEPTH = 4
CHUNK = 64
EPS = 1e-6
GLA_HEADS = 4
GLA_DK = 64
GLA_DV = 128
GLA_GATE_RANK = 16
GLA_GATE_NORM = 16.0
GLA_SUB = 16
EXP_CLAMP = 80.0
MLA_HEADS = 8
MLA_Q_LORA = 384
MLA_KV_LORA = 256
MLA_NOPE = 64
MLA_ROPE = 32
MLA_V = 64
MLA_SCALE = 1.0 / math.sqrt(MLA_NOPE + MLA_ROPE)
LOG2E = math.log2(math.e)
ROPE_BASE = 10000.0
D_FF = 2816
N_MOD = 9
LANES = 128
HEAD_BLOCK = 128
NEG_BIG = -1e30
SCORE_SAFE = 60.0
NRM_ROWS = 8

GQK = GLA_HEADS * GLA_DK
GVW = GLA_HEADS * GLA_DV
MQK = MLA_HEADS * HEAD_BLOCK
MVW = MLA_HEADS * MLA_V
O_GQ, O_GK, O_GV = 0, GQK, 2 * GQK
O_GLR = O_GV + GVW
IN_COLS = O_GLR + GLA_GATE_RANK + GVW + MLA_Q_LORA + MLA_KV_LORA + MLA_ROPE
IN_W = -(-IN_COLS // LANES) * LANES
STREAM_K = 128
STREAM_O = 1408
FFN_TP = 256
FFN_TM = 1024
GLA_TL = 1024
ATTN_TQ = 256
STREAM_SEQS = 4
CUMSUM_ROWS = 256
VMEM_LIMIT = 56 * 1024 * 1024


def _dot(a, b):
    return jnp.dot(a, b, preferred_element_type=F32)


def _dot_nt(a, b):
    return lax.dot_general(a, b, (((1,), (1,)), ((), ())), preferred_element_type=F32)


def _dot_tn(a, b):
    return lax.dot_general(a, b, (((0,), (0,)), ((), ())), preferred_element_type=F32)


def _rms(x, g):
    ms = jnp.mean(x * x, axis=-1, keepdims=True)
    return x * lax.rsqrt(ms + EPS) * g


def _modulate(x, g, sh, sc, nb):
    tm, d = x.shape
    y = _rms(x, g)
    if nb == 1:
        return y * (1.0 + sc[0]) + sh[0]
    y3 = y.reshape(nb, tm // nb, d)
    return (y3 * (1.0 + sc) + sh).reshape(tm, d)


def _gated_residual(x, y, coef, nb):
    tm, d = x.shape
    if nb == 1:
        return x + coef[0] * y
    return x + (coef * y.reshape(nb, tm // nb, d)).reshape(tm, d)


def _const_spec(shape):
    nd = len(shape)
    return pl.BlockSpec(shape, lambda *_: (0,) * nd, pipeline_mode=pl.Buffered(1))


def _layer_spec(shape, *lead):
    nd = len(shape)
    return pl.BlockSpec((None,) * len(lead) + tuple(shape), lambda *_: tuple(lead) + (0,) * nd,
                        pipeline_mode=pl.Buffered(1))


def _mod_spec(nmod, nb, tps, seq0, layer, blk):
    return pl.BlockSpec((None, nmod, nb, 1, D_MODEL), lambda i: (layer, blk, seq0 + i // tps, 0, 0))


def _params(sem):
    return pltpu.CompilerParams(dimension_semantics=sem, vmem_limit_bytes=VMEM_LIMIT)


ADA_GROUP = 3


def _ada_kernel(c_ref, w_ref, b_ref, o_ref):
    c = c_ref[...]
    sc = (c * jax.nn.sigmoid(c)).astype(BF16)
    for k in range(ADA_GROUP):
        w = w_ref[0, :, k * D_MODEL:(k + 1) * D_MODEL].astype(BF16)
        o_ref[0, k] = _dot(sc, w) + b_ref[0, k]


def _ada_call(c_all, w_ada, b_ada):
    nrow = c_all.shape[0]
    b4 = b_ada.reshape(DEPTH, N_MOD, 1, D_MODEL)
    return pl.pallas_call(
        _ada_kernel,
        grid=(DEPTH, N_MOD // ADA_GROUP),
        in_specs=[
            pl.BlockSpec((nrow, D_MODEL), lambda l, j: (0, 0)),
            pl.BlockSpec((1, D_MODEL, ADA_GROUP * D_MODEL), lambda l, j: (l, 0, j)),
            pl.BlockSpec((1, ADA_GROUP, 1, D_MODEL), lambda l, j: (l, j, 0, 0)),
        ],
        out_specs=pl.BlockSpec((1, ADA_GROUP, nrow, D_MODEL), lambda l, j: (l, j, 0, 0)),
        out_shape=jax.ShapeDtypeStruct((DEPTH, N_MOD, nrow, D_MODEL), F32),
        compiler_params=_params(("arbitrary", "arbitrary")),
        name="ada_mod",
    )(c_all, w_ada, b4)


def _ffn_body(x, sh, sc, gate, g, wa_ref, wb_ref, wo_ref, h_ref, nb):
    n = _modulate(x, g, sh, sc, nb).astype(BF16)
    for c in range(D_FF // FFN_TP):
        lo = c * FFN_TP
        a = _dot(n, wa_ref[:, lo:lo + FFN_TP])
        b = _dot(n, wb_ref[:, lo:lo + FFN_TP])
        h_ref[:, lo:lo + FFN_TP] = (a * jax.nn.sigmoid(a) * b).astype(BF16)
    y = _dot(h_ref[...], wo_ref[...])
    return _gated_residual(x, y, 0.5 * (1.0 + gate), nb)


def _mix_residual(x, og_ref, om_ref, gate, wmix_ref, nb):
    y = _dot(og_ref[...], wmix_ref[0:GVW, :]) + _dot(om_ref[...], wmix_ref[GVW:GVW + MVW, :])
    return _gated_residual(x, y, 1.0 + gate, nb)


def _ffn_kernel(x_ref, mod_ref, g_ref, wa_ref, wb_ref, wo_ref, o_ref, h_ref, *, nb):
    o_ref[...] = _ffn_body(x_ref[...], mod_ref[0], mod_ref[1], mod_ref[2], g_ref[...],
                           wa_ref, wb_ref, wo_ref, h_ref, nb)


def _mix_ffn_kernel(x_ref, og_ref, om_ref, gmix_ref, mod_ref, wmix_ref, g_ref, wa_ref, wb_ref, wo_ref,
                    fg_ref, o_ref, h_ref, *, nb, final_norm):
    x = _mix_residual(x_ref[...], og_ref, om_ref, gmix_ref[2], wmix_ref, nb)
    x = _ffn_body(x, mod_ref[0], mod_ref[1], mod_ref[2], g_ref[...], wa_ref, wb_ref, wo_ref, h_ref, nb)
    o_ref[...] = _rms(x, fg_ref[...]) if final_norm else x


def _ffn_weight_specs():
    half = lambda blk: pl.BlockSpec((D_MODEL, D_FF), lambda *_: (0, blk), pipeline_mode=pl.Buffered(1))
    return [half(0), half(1), _const_spec((D_FF, D_MODEL))]


def _ffn_call(x, mods, ln_g, w16, layer, *, tm, nb, tps, seq0):
    t = x.shape[0]
    return pl.pallas_call(
        functools.partial(_ffn_kernel, nb=nb),
        grid=(t // tm,),
        in_specs=[
            pl.BlockSpec((tm, D_MODEL), lambda i: (i, 0)),
            _mod_spec(3, nb, tps, seq0, layer, 0),
            _layer_spec((1, D_MODEL), layer, 0),
        ] + _ffn_weight_specs(),
        out_specs=pl.BlockSpec((tm, D_MODEL), lambda i: (i, 0)),
        out_shape=jax.ShapeDtypeStruct((t, D_MODEL), F32),
        scratch_shapes=[pltpu.VMEM((tm, D_FF), BF16)],
        compiler_params=_params(("arbitrary",)),
        name="ffn",
    )(x, mods, ln_g, *w16)


def _mix_ffn_call(x, og, om, mods, w_mix, ln_g, w16, fg, layer, *, tm, nb, tps, seq0, final_norm):
    t = x.shape[0]
    tok = lambda w: pl.BlockSpec((tm, w), lambda i: (i, 0))
    return pl.pallas_call(
        functools.partial(_mix_ffn_kernel, nb=nb, final_norm=final_norm),
        grid=(t // tm,),
        in_specs=[
            tok(D_MODEL), tok(GVW), tok(MVW),
            _mod_spec(3, nb, tps, seq0, layer, 1),
            _mod_spec(3, nb, tps, seq0, layer, 2),
            _layer_spec((GVW + MVW, D_MODEL), layer),
            _layer_spec((1, D_MODEL), layer, 2),
        ] + _ffn_weight_specs() + [_const_spec((1, D_MODEL))],
        out_specs=tok(D_MODEL),
        out_shape=jax.ShapeDtypeStruct((t, D_MODEL), F32),
        scratch_shapes=[pltpu.VMEM((tm, D_FF), BF16)],
        compiler_params=_params(("arbitrary",)),
        name="mix_ffn",
    )(x, og, om, mods, mods, w_mix, ln_g, *w16, fg)


def _ffn_stream_kernel(*refs, nb, has_mix, final_norm):
    if has_mix:
        x_ref, og_ref, om_ref, gmix_ref, mod_ref, wmix_ref, *refs = refs
    else:
        x_ref, mod_ref, *refs = refs
    (g_ref, win_ref, wout_ref, fg_ref,
     y_ref, win16_ref, wout16_ref, xs_ref, n_ref, ab_ref, h_ref, acc_ref) = refs
    c = pl.program_id(0)
    nk, no = n_ref.shape[0], h_ref.shape[0]

    @pl.when(c == 0)
    def _():
        x = x_ref[...]
        if has_mix:
            x = _mix_residual(x, og_ref, om_ref, gmix_ref[2], wmix_ref, nb)
        xs_ref[...] = x
        n = _modulate(x, g_ref[...], mod_ref[0], mod_ref[1], nb).astype(BF16)
        for k in range(nk):
            n_ref[k] = n[:, k * STREAM_K:(k + 1) * STREAM_K]
        ab_ref[...] = jnp.zeros_like(ab_ref)

    @pl.when(c < nk)
    def _():
        w = win_ref[...].astype(BF16)
        win16_ref[...] = w
        ab_ref[...] += _dot(n_ref[c], w)

    @pl.when(c == nk)
    def _():
        a, b = ab_ref[:, :D_FF], ab_ref[:, D_FF:]
        h = (a * jax.nn.sigmoid(a) * b).astype(BF16)
        for j in range(no):
            h_ref[j] = h[:, j * STREAM_O:(j + 1) * STREAM_O]
        acc_ref[...] = jnp.zeros_like(acc_ref)

    @pl.when(c >= nk)
    def _():
        w = wout_ref[...].astype(BF16)
        wout16_ref[...] = w
        acc_ref[...] += _dot(h_ref[c - nk], w)

    @pl.when(c == pl.num_programs(0) - 1)
    def _():
        x = _gated_residual(xs_ref[...], acc_ref[...], 0.5 * (1.0 + mod_ref[2]), nb)
        y_ref[...] = _rms(x, fg_ref[...]) if final_norm else x


def _ffn_stream_call(x, mix, mods, ln_g, w_in, w_out, fg, layer, *, nb, ffn_idx, final_norm):
    tm = x.shape[0]
    nk, no = D_MODEL // STREAM_K, D_FF // STREAM_O
    k_blk = lambda c: jnp.minimum(c, nk - 1)
    o_blk = lambda c: jnp.maximum(c - nk, 0)
    fixed = lambda shape: pl.BlockSpec(shape, lambda c: (0,) * len(shape))
    mod = lambda blk: pl.BlockSpec((None, 3, nb, 1, D_MODEL), lambda c: (layer, blk, 0, 0, 0))
    if mix is None:
        args, specs = [x, mods], [fixed((tm, D_MODEL)), mod(ffn_idx)]
    else:
        og, om, w_mix = mix
        args = [x, og, om, mods, mods, w_mix]
        specs = [fixed((tm, D_MODEL)), fixed((tm, GVW)), fixed((tm, MVW)), mod(1), mod(ffn_idx),
                 _layer_spec((GVW + MVW, D_MODEL), layer)]
    args += [ln_g, w_in, w_out, fg]
    specs += [
        _layer_spec((1, D_MODEL), layer, ffn_idx),
        pl.BlockSpec((None, STREAM_K, 2 * D_FF), lambda c: (layer, k_blk(c), 0)),
        pl.BlockSpec((None, STREAM_O, D_MODEL), lambda c: (layer, o_blk(c), 0)),
        fixed((1, D_MODEL)),
    ]
    y, w_in16, w_out16 = pl.pallas_call(
        functools.partial(_ffn_stream_kernel, nb=nb, has_mix=mix is not None, final_norm=final_norm),
        grid=(nk + no,),
        in_specs=specs,
        out_specs=[fixed((tm, D_MODEL)),
                   pl.BlockSpec((STREAM_K, 2 * D_FF), lambda c: (k_blk(c), 0)),
                   pl.BlockSpec((STREAM_O, D_MODEL), lambda c: (o_blk(c), 0))],
        out_shape=[jax.ShapeDtypeStruct((tm, D_MODEL), F32),
                   jax.ShapeDtypeStruct((D_MODEL, 2 * D_FF), BF16),
                   jax.ShapeDtypeStruct((D_FF, D_MODEL), BF16)],
        scratch_shapes=[pltpu.VMEM((tm, D_MODEL), F32),
                        pltpu.VMEM((nk, tm, STREAM_K), BF16),
                        pltpu.VMEM((tm, 2 * D_FF), F32),
                        pltpu.VMEM((no, tm, STREAM_O), BF16),
                        pltpu.VMEM((tm, D_MODEL), F32)],
        compiler_params=_params(("arbitrary",)),
        name="ffn_stream",
    )(*args)
    return y, (w_in16, w_in16, w_out16)


def _swap_rope_halves(xb, lane):
    fwd = pltpu.roll(xb, LANES - MLA_ROPE // 2, axis=1)
    bwd = pltpu.roll(xb, MLA_ROPE // 2, axis=1)
    return jnp.where(lane < MLA_NOPE + MLA_ROPE // 2, fwd, bwd)


N_SHIFT = (GVW + MLA_Q_LORA + MLA_KV_LORA) // LANES


R_GOG = O_GLR
R_CQ = R_GOG + GVW
R_CKV = R_CQ + MLA_Q_LORA
R_TAIL = R_CKV + MLA_KV_LORA
TAIL_GLR = MLA_ROPE
Q_UNSCALE = 1.0 / (MLA_SCALE * LOG2E)


def _mixin_kernel(x_ref, mod_ref, g_ref, win_ref, wgk_ref, bgk_ref, qg_ref, wuq_ref,
                  kvg_ref, wk_ref, wv_ref, ones_ref, cq_ref, sq_ref, ck_ref, sk_ref,
                  gq_ref, gk_ref, gv_ref, gog_ref, lg_ref, qh_ref, kh_ref, vh_ref,
                  ckv_ref, kpe_ref, nrm_ref, wre_ref, *, nb, kpe_rows):
    @pl.when(pl.program_id(0) == 0)
    def _():
        keep = LANES - GLA_GATE_RANK
        wl = lax.broadcasted_iota(jnp.int32, (D_MODEL, LANES), 1)
        wre_ref[:, 0:O_GLR] = win_ref[:, 0:O_GLR]
        gate_blk = win_ref[:, O_GLR:O_GLR + LANES].astype(F32)
        prev = pltpu.roll(gate_blk, keep, axis=1)
        for j in range(N_SHIFT):
            c0 = O_GLR + LANES * (j + 1)
            nxt = pltpu.roll(win_ref[:, c0:c0 + LANES].astype(F32), keep, axis=1)
            wre_ref[:, R_GOG + LANES * j:R_GOG + LANES * (j + 1)] = (
                jnp.where(wl < keep, prev, nxt).astype(BF16))
            prev = nxt
        tail = jnp.where(wl < keep, prev, 0.0)
        glr = pltpu.roll(gate_blk, TAIL_GLR, axis=1)
        tail = jnp.where((wl >= TAIL_GLR) & (wl < TAIL_GLR + GLA_GATE_RANK), glr, tail)
        wre_ref[:, R_TAIL:R_TAIL + LANES] = tail.astype(BF16)

    x = x_ref[...]
    tm = x.shape[0]
    n = _modulate(x, g_ref[...], mod_ref[0], mod_ref[1], nb).astype(BF16)
    lane = lax.broadcasted_iota(jnp.int32, (tm, LANES), 1)
    h = _dot(n, wre_ref[...])
    gq_ref[...] = h[:, O_GQ:O_GQ + GQK] * (GLA_DK ** -0.5)
    gk_ref[...] = h[:, O_GK:O_GK + GQK]
    gv_ref[...] = h[:, O_GV:O_GV + GVW]
    gog_ref[...] = h[:, R_GOG:R_GOG + GVW]
    tail = h[:, R_TAIL:R_TAIL + LANES]
    z = _dot(tail.astype(BF16), wgk_ref[...]) + bgk_ref[...]
    lsig = jnp.minimum(z, 0.0) - jnp.log1p(jnp.exp(-jnp.abs(z)))
    lg_ref[...] = lsig / GLA_GATE_NORM
    cqn = _rms(h[:, R_CQ:R_CQ + MLA_Q_LORA], qg_ref[...]).astype(BF16)
    q = _dot(cqn, wuq_ref[...])
    cq, sq = cq_ref[...], sq_ref[...]
    qr = jnp.concatenate(
        [q[:, hd * HEAD_BLOCK:(hd + 1) * HEAD_BLOCK] * cq
         + _swap_rope_halves(q[:, hd * HEAD_BLOCK:(hd + 1) * HEAD_BLOCK], lane) * sq
         for hd in range(MLA_HEADS)], axis=1)
    qh_ref[...] = qr.astype(BF16)
    ckv = _rms(h[:, R_CKV:R_CKV + MLA_KV_LORA], kvg_ref[...])
    ckv_ref[...] = ckv
    kb = pltpu.roll(tail, MLA_NOPE, axis=1)
    kb = jnp.where((lane >= MLA_NOPE) & (lane < MLA_NOPE + MLA_ROPE), kb, 0.0)
    kpe = kb * ck_ref[...] + _swap_rope_halves(kb, lane) * sk_ref[...]
    kpe0 = pltpu.roll(kpe, LANES - MLA_NOPE, axis=1)
    kpe_ref[...] = kpe0.T[:MLA_ROPE, :] if kpe_rows else kpe0[:, :MLA_ROPE]
    ckv16 = ckv.astype(BF16)
    kn = _dot(ckv16, wk_ref[...])
    kr = jnp.concatenate(
        [kn[:, hd * HEAD_BLOCK:(hd + 1) * HEAD_BLOCK] + kpe for hd in range(MLA_HEADS)], axis=1)
    kh_ref[...] = kr.astype(BF16)
    vh_ref[...] = _dot(ckv16, wv_ref[...]).astype(BF16)
    u = _dot((qr * qr * Q_UNSCALE + kr * kr * (1.0 / Q_UNSCALE)).astype(BF16), ones_ref[...])
    nrm_ref[0] = jnp.concatenate(
        [jnp.max(u, axis=0, keepdims=True), jnp.zeros((NRM_ROWS - 1, LANES), F32)], axis=0)


def _mixin_call(x, mods, ln_g, w_in, w_gk, b_gk, qg, w_uq, kvg, w_k, w_v, tabs, layer, *, tm, nb, tps, seq0):
    t = x.shape[0]
    tok = lambda w: pl.BlockSpec((tm, w), lambda i: (i, 0))
    tab = pl.BlockSpec((tm, LANES), lambda i: (i % tps, 0))
    out_w = [(GQK, F32), (GQK, F32), (GVW, F32), (GVW, F32), (GQK, F32),
             (MQK, BF16), (MQK, BF16), (MVW, BF16), (MLA_KV_LORA, F32)]
    kpe_rows = nb == 1
    if kpe_rows:
        kpe_spec = pl.BlockSpec((None, MLA_ROPE, tm), lambda i: (i // tps, 0, i % tps))
        kpe_shape = jax.ShapeDtypeStruct((t // (tm * tps), MLA_ROPE, tm * tps), F32)
    else:
        kpe_spec, kpe_shape = tok(MLA_ROPE), jax.ShapeDtypeStruct((t, MLA_ROPE), F32)
    nrm_spec = pl.BlockSpec((1, NRM_ROWS, LANES), lambda i: (i, 0, 0))
    nrm_shape = jax.ShapeDtypeStruct((t // tm, NRM_ROWS, LANES), F32)
    head_ones = np.zeros((MQK, LANES), np.float32)
    head_ones[np.arange(MQK), np.arange(MQK) // HEAD_BLOCK] = 1.0
    return pl.pallas_call(
        functools.partial(_mixin_kernel, nb=nb, kpe_rows=kpe_rows),
        grid=(t // tm,),
        in_specs=[
            tok(D_MODEL),
            _mod_spec(3, nb, tps, seq0, layer, 1),
            _layer_spec((1, D_MODEL), layer, 1),
            _layer_spec((D_MODEL, IN_W), layer),
            _layer_spec((LANES, GQK), layer),
            _layer_spec((1, GQK), layer),
            _layer_spec((1, MLA_Q_LORA), layer),
            _layer_spec((MLA_Q_LORA, MQK), layer),
            _layer_spec((1, MLA_KV_LORA), layer),
            _layer_spec((MLA_KV_LORA, MQK), layer),
            _layer_spec((MLA_KV_LORA, MVW), layer),
            _const_spec((MQK, LANES)),
            tab, tab, tab, tab,
        ],
        out_specs=[tok(w) for w, _ in out_w] + [kpe_spec, nrm_spec],
        out_shape=[jax.ShapeDtypeStruct((t, w), dt) for w, dt in out_w] + [kpe_shape, nrm_shape],
        scratch_shapes=[pltpu.VMEM((D_MODEL, IN_W), BF16)],
        compiler_params=_params(("arbitrary",)),
        name="mixer_in",
    )(x, mods, ln_g, w_in, w_gk, b_gk, qg, w_uq, kvg, w_k, w_v, jnp.asarray(head_ones, BF16), *tabs)


def _gla_kernel(q_ref, k_ref, v_ref, og_ref, lg_ref, s0_ref, ng_ref, o_ref, sout_ref, st_ref,
                *, chunk, nch, ns):
    j = pl.program_id(1)
    nsub = chunk // GLA_SUB
    tl = chunk * nch
    lane_head = lax.broadcasted_iota(jnp.int32, (chunk, GQK), 1) // GLA_DK

    def stack_heads(a):
        return jnp.concatenate(
            [jnp.where(lane_head == hd, a, 0.0) for hd in range(GLA_HEADS)], axis=0).astype(BF16)

    @pl.when(j == 0)
    def _():
        for s in range(ns):
            for hd in range(GLA_HEADS):
                rows = [jnp.zeros((GLA_DK, GLA_DV), F32)] * GLA_HEADS
                rows[hd] = s0_ref[s, hd]
                st_ref[s, hd * GLA_DV:(hd + 1) * GLA_DV, :] = jnp.concatenate(rows, axis=0).T

    grp = min(tl, CUMSUM_ROWS)
    ri = lax.broadcasted_iota(jnp.int32, (grp, grp), 0)
    ci = lax.broadcasted_iota(jnp.int32, (grp, grp), 1)
    tri = jnp.where((ci <= ri) & (ci // chunk == ri // chunk), 1.0, 0.0).astype(BF16)
    b_parts = []
    for g in range(ns * tl // grp):
        lg = lg_ref[g * grp:(g + 1) * grp, :]
        p0 = lg.astype(BF16)
        r1 = lg - p0.astype(F32)
        p1 = r1.astype(BF16)
        p2 = (r1 - p1.astype(F32)).astype(BF16)
        b_parts.append(_dot(tri, p0) + _dot(tri, p1) + _dot(tri, p2))
    b_all = b_parts[0] if len(b_parts) == 1 else jnp.concatenate(b_parts, axis=0)

    rt = lax.broadcasted_iota(jnp.int32, (GLA_HEADS * chunk, nsub * chunk), 0) % chunk
    cc = lax.broadcasted_iota(jnp.int32, (GLA_HEADS * chunk, nsub * chunk), 1)
    keep = (cc // chunk == rt // GLA_SUB) & (cc % chunk <= rt)
    ng = ng_ref[...]

    def v_of(r0):
        return v_ref[r0:r0 + chunk, :].astype(BF16)

    start = {}
    for s in range(ns):
        st = st_ref[s]
        for c in range(nch):
            r0 = s * tl + c * chunk
            b = b_all[r0:r0 + chunk]
            kl = k_ref[r0:r0 + chunk, :] * jnp.exp(b[chunk - 1:chunk] - b)
            g_end = jnp.exp(b[chunk - 8:chunk])[7:8]
            start[r0] = st
            st = st * g_end + _dot_tn(v_of(r0), kl.astype(BF16))
        st_ref[s] = st

    for r0 in range(0, ns * tl, chunk):
        b = b_all[r0:r0 + chunk]
        q = q_ref[r0:r0 + chunk, :]
        k = k_ref[r0:r0 + chunk, :]
        r_own = jnp.broadcast_to(b[0:1], (chunk, GQK)) if nsub == 1 else jnp.concatenate(
            [jnp.broadcast_to(b[i * GLA_SUB:i * GLA_SUB + 1], (GLA_SUB, GQK)) for i in range(nsub)],
            axis=0)
        qt = q * jnp.exp(b - r_own)
        qe = q * jnp.exp(b)
        k_rel = [(k * jnp.exp(jnp.minimum(b[i * GLA_SUB:i * GLA_SUB + 1] - b, EXP_CLAMP))).astype(BF16)
                 for i in range(nsub)]
        att = _dot_nt(stack_heads(qt), jnp.concatenate(k_rel, axis=0))
        att = jnp.where(keep, att, 0.0).astype(BF16)
        v = v_of(r0)
        o_all = (_dot(att, jnp.concatenate([v] * nsub, axis=0))
                 + _dot_nt(stack_heads(qe), start[r0].astype(BF16)))
        for hd in range(GLA_HEADS):
            cols = slice(hd * GLA_DV, (hd + 1) * GLA_DV)
            o = o_all[hd * chunk:(hd + 1) * chunk, cols]
            og = og_ref[r0:r0 + chunk, cols]
            o_ref[r0:r0 + chunk, cols] = (_rms(o, ng) * (og * jax.nn.sigmoid(og))).astype(BF16)

    @pl.when(j == pl.num_programs(1) - 1)
    def _():
        for s in range(ns):
            for hd in range(GLA_HEADS):
                st_h = st_ref[s, hd * GLA_DV:(hd + 1) * GLA_DV, :]
                sout_ref[s, hd] = st_h.T[hd * GLA_DK:(hd + 1) * GLA_DK, :]


def _gla_call(gq, gk, gv, gog, lg, s0, ng, layer, *, nseq, seq_len, tl, chunk):
    t = gq.shape[0]
    tps = seq_len // tl
    ns = math.gcd(nseq, STREAM_SEQS) if tps == 1 else 1
    tok = lambda w: pl.BlockSpec((ns * tl, w), lambda b, j: (b * tps + j, 0))
    st_spec = pl.BlockSpec((ns, GLA_HEADS, GLA_DK, GLA_DV), lambda b, j: (b, 0, 0, 0))
    s0_spec = pl.BlockSpec((None, ns, GLA_HEADS, GLA_DK, GLA_DV), lambda b, j: (layer, b, 0, 0, 0))
    return pl.pallas_call(
        functools.partial(_gla_kernel, chunk=chunk, nch=tl // chunk, ns=ns),
        grid=(nseq // ns, tps),
        in_specs=[tok(GQK), tok(GQK), tok(GVW), tok(GVW), tok(GQK), s0_spec,
                  _layer_spec((1, GLA_DV), layer)],
        out_specs=[tok(GVW), st_spec],
        out_shape=[jax.ShapeDtypeStruct((t, GVW), BF16),
                   jax.ShapeDtypeStruct((nseq, GLA_HEADS, GLA_DK, GLA_DV), F32)],
        scratch_shapes=[pltpu.VMEM((ns, GVW, GQK), F32)],
        compiler_params=_params(("arbitrary", "arbitrary")),
        name="gla",
    )(gq, gk, gv, gog, lg, s0, ng)


def _softmax_pv(parts, vs, shift_by_max=True):
    if shift_by_max:
        m = functools.reduce(jnp.maximum, [jnp.max(s, axis=-1, keepdims=True) for s in parts])
    acc, den = None, None
    for s, v in zip(parts, vs):
        p = jnp.exp2(s - m) if shift_by_max else jnp.exp2(s)
        d = jnp.sum(p, axis=-1, keepdims=True)
        o = _dot(p.astype(BF16), v)
        acc = o if acc is None else acc + o
        den = d if den is None else den + d
    return acc / den


def _attn_prompt_kernel(q_ref, k_ref, v_ref, nrm_ref, o_ref, *, seq_len, tq):
    pair = pl.program_id(1)
    lane = lax.broadcasted_iota(jnp.int32, (tq, LANES), 1)
    qi = lax.broadcasted_iota(jnp.int32, (tq, tq), 0) // CHUNK
    ki = lax.broadcasted_iota(jnp.int32, (tq, tq), 1) // CHUNK
    diag_ok = ki <= qi

    def attend(shift_by_max):
        for i in range(seq_len // tq):
            r0 = i * tq
            outs = []
            for hh in range(2):
                cs = slice(hh * HEAD_BLOCK, (hh + 1) * HEAD_BLOCK)
                q = q_ref[r0:r0 + tq, cs]
                s_d = jnp.where(diag_ok, _dot_nt(q, k_ref[r0:r0 + tq, cs]), NEG_BIG)
                parts, vs = [s_d], [v_ref[r0:r0 + tq, :]]
                if i > 0:
                    parts.append(_dot_nt(q, k_ref[0:r0, cs]))
                    vs.append(v_ref[0:r0, :])
                outs.append(_softmax_pv(parts, vs, shift_by_max))
            o_ref[r0:r0 + tq, :] = jnp.where(lane < MLA_V, outs[0], outs[1]).astype(BF16)

    bound = 0.5 * jnp.max(nrm_ref[...], axis=0)[0:1]
    head_lane = lax.broadcasted_iota(jnp.int32, (1, LANES), 1) // 2
    safe = jnp.max(jnp.where(head_lane == pair, bound, 0.0)) <= SCORE_SAFE
    pl.when(safe)(functools.partial(attend, False))
    pl.when(jnp.logical_not(safe))(functools.partial(attend, True))


def _attn_prompt_call(qh, kh, vh, nrm, *, nseq, seq_len, tq):
    t = qh.shape[0]
    npair = MLA_HEADS // 2
    tps = nrm.shape[0] // nseq
    return pl.pallas_call(
        functools.partial(_attn_prompt_kernel, seq_len=seq_len, tq=tq),
        grid=(nseq, npair),
        in_specs=[
            pl.BlockSpec((seq_len, 2 * HEAD_BLOCK), lambda b, p: (b, p)),
            pl.BlockSpec((seq_len, 2 * HEAD_BLOCK), lambda b, p: (b, p)),
            pl.BlockSpec((seq_len, 2 * MLA_V), lambda b, p: (b, p)),
            pl.BlockSpec((tps, NRM_ROWS, LANES), lambda b, p: (b, 0, 0)),
        ],
        out_specs=pl.BlockSpec((seq_len, 2 * MLA_V), lambda b, p: (b, p)),
        out_shape=jax.ShapeDtypeStruct((t, MVW), BF16),
        compiler_params=_params(("arbitrary", "arbitrary")),
        name="mla_attn_prompt",
    )(qh, kh, vh, nrm)


def _attn_stream_kernel(q_ref, ckc_ref, kpc_ref, ckn_ref, kpn_ref, wk_ref, wv_ref, place_ref,
                        o_ref, *, tq, ns):
    place_t = place_ref[...]
    blocks = [slice(hd * HEAD_BLOCK, (hd + 1) * HEAD_BLOCK) for hd in range(MLA_HEADS)]
    lane = lax.broadcasted_iota(jnp.int32, (tq, LANES), 1)
    for s in range(ns):
        rows = slice(s * tq, (s + 1) * tq)
        q = q_ref[rows, :]
        ckc = ckc_ref[0, s].astype(BF16)
        kpc_t = _dot(place_t, kpc_ref[0, s].astype(BF16)).astype(BF16)
        ckn = ckn_ref[rows, :].astype(BF16)
        kpn = _dot_nt(kpn_ref[rows, :].astype(BF16), place_t).astype(BF16)
        q_abs = jnp.concatenate(
            [_dot_nt(q[:, bs], wk_ref[:, bs]) for bs in blocks], axis=0).astype(BF16)
        q_blk = jnp.concatenate([q[:, bs] for bs in blocks], axis=0)
        s_c = _dot_nt(q_abs, ckc) + _dot(q_blk, kpc_t)
        s_n = _dot_nt(q_abs, ckn) + _dot_nt(q_blk, kpn)
        lat = _softmax_pv([s_c, s_n], [ckc, ckn]).astype(BF16)
        for p in range(MLA_HEADS // 2):
            wv = wv_ref[:, p * LANES:(p + 1) * LANES]
            even = _dot(lat[(2 * p) * tq:(2 * p + 1) * tq], wv)
            odd = _dot(lat[(2 * p + 1) * tq:(2 * p + 2) * tq], wv)
            o_ref[rows, p * LANES:(p + 1) * LANES] = jnp.where(lane < MLA_V, even, odd).astype(BF16)


def _attn_stream_call(qh, cache_ckv, cache_kpe, ckv_new, kpe_new, w_k, w_v, place, layer,
                      *, nseq, tq, past):
    t = qh.shape[0]
    qpos = past + np.arange(tq)
    if past % CHUNK or not ((qpos[None, :] // CHUNK) <= (qpos[:, None] // CHUNK)).all():
        raise NotImplementedError("new frames spanning several chunks")
    ns = math.gcd(nseq, STREAM_SEQS)
    return pl.pallas_call(
        functools.partial(_attn_stream_kernel, tq=tq, ns=ns),
        grid=(nseq // ns,),
        in_specs=[
            pl.BlockSpec((ns * tq, MQK), lambda b: (b, 0)),
            pl.BlockSpec((1, ns, past, MLA_KV_LORA), lambda b: (layer, b, 0, 0)),
            pl.BlockSpec((1, ns, MLA_ROPE, past), lambda b: (layer, b, 0, 0)),
            pl.BlockSpec((ns * tq, MLA_KV_LORA), lambda b: (b, 0)),
            pl.BlockSpec((ns * tq, MLA_ROPE), lambda b: (b, 0)),
            _layer_spec((MLA_KV_LORA, MQK), layer),
            _layer_spec((MLA_KV_LORA, MVW), layer),
            _const_spec((HEAD_BLOCK, MLA_ROPE)),
        ],
        out_specs=pl.BlockSpec((ns * tq, MVW), lambda b: (b, 0)),
        out_shape=jax.ShapeDtypeStruct((t, MVW), BF16),
        compiler_params=_params(("arbitrary",)),
        name="mla_attn_stream",
    )(qh, cache_ckv, cache_kpe, ckv_new, kpe_new, w_k, w_v, place)


def _prep_weights(w_in, w_gk2, b_gk2, w_uq, w_ukv):
    w_in_p = jnp.pad(w_in.astype(BF16), ((0, 0), (0, 0), (0, IN_W - IN_COLS)))
    w_gk_p = jnp.pad(w_gk2.astype(BF16),
                     ((0, 0), (TAIL_GLR, LANES - TAIL_GLR - GLA_GATE_RANK), (0, 0)))
    b_gk_p = b_gk2.reshape(DEPTH, 1, GQK)
    uq = w_uq.astype(BF16).reshape(DEPTH, MLA_Q_LORA, MLA_HEADS, MLA_NOPE + MLA_ROPE)
    uq = jnp.pad(uq, ((0, 0), (0, 0), (0, 0), (0, HEAD_BLOCK - MLA_NOPE - MLA_ROPE)))
    w_uq_p = uq.reshape(DEPTH, MLA_Q_LORA, MQK)
    ukv = w_ukv.astype(BF16).reshape(DEPTH, MLA_KV_LORA, MLA_HEADS, MLA_NOPE + MLA_V)
    uk = jnp.pad(ukv[..., :MLA_NOPE], ((0, 0), (0, 0), (0, 0), (0, HEAD_BLOCK - MLA_NOPE)))
    w_k_p = uk.reshape(DEPTH, MLA_KV_LORA, MQK)
    w_v_p = ukv[..., MLA_NOPE:].reshape(DEPTH, MLA_KV_LORA, MVW)
    return w_in_p, w_gk_p, b_gk_p, w_uq_p, w_k_p, w_v_p


def _rope_tables(pos, reps):
    half = MLA_ROPE // 2
    inv = ROPE_BASE ** (-jnp.arange(half, dtype=F32) / half)
    ang = pos.astype(F32)[:, None] * inv[None, :]
    cos, sin = jnp.cos(ang), jnp.sin(ang)
    n = pos.shape[0]
    one, zero = jnp.ones((n, MLA_NOPE), F32), jnp.zeros((n, MLA_NOPE), F32)
    pad = jnp.zeros((n, HEAD_BLOCK - MLA_NOPE - MLA_ROPE), F32)
    c = jnp.concatenate([one, cos, cos, pad], axis=1)
    s = jnp.concatenate([zero, -sin, sin, pad], axis=1)
    q_scale = MLA_SCALE * LOG2E
    tabs = (c * q_scale, s * q_scale, c, s)
    return tuple(jnp.tile(a, (reps, 1)) for a in tabs)


def _kpe_placement():
    p = np.zeros((HEAD_BLOCK, MLA_ROPE), np.float32)
    p[MLA_NOPE + np.arange(MLA_ROPE), np.arange(MLA_ROPE)] = 1.0
    return jnp.asarray(p, BF16)


def _trunk(x, mods, seq0, wts, tabs, *, nseq, seq_len, tm, tm_ffn, nb, gla_tl, gla_chunk, s0,
           stream=None, ffn16=None):
    tps = max(seq_len // tm, 1) if nb == 1 else 1
    tile = dict(tm=tm, nb=nb, tps=tps, seq0=seq0)
    tile_ffn = dict(tile, tm=tm_ffn, tps=max(seq_len // tm_ffn, 1)) if nb == 1 else tile
    ckvs, kpes, states, cast = [], [], [], []
    for l in range(DEPTH):
        if stream is None:
            x = _ffn_call(x, mods, wts["ln_g"], ffn16[l][0], l, **tile_ffn)
        else:
            x, w1 = _ffn_stream_call(x, None, mods, wts["ln_g"], wts["ffn1_in"], wts["ffn1_out"],
                                     wts["final_g"], l, nb=nb, ffn_idx=0, final_norm=False)
        gq, gk, gv, gog, lg, qh, kh, vh, ckv, kpe, nrm = _mixin_call(
            x, mods, wts["ln_g"], wts["w_in"], wts["w_gk"], wts["b_gk"], wts["qg"], wts["w_uq"],
            wts["kvg"], wts["w_k"], wts["w_v"], tabs, l, **tile)
        og, st = _gla_call(gq, gk, gv, gog, lg, s0, wts["ng"], l,
                           nseq=nseq, seq_len=seq_len, tl=gla_tl, chunk=gla_chunk)
        last = l == DEPTH - 1
        if stream is None:
            om = _attn_prompt_call(qh, kh, vh, nrm, nseq=nseq, seq_len=seq_len, tq=min(ATTN_TQ, seq_len))
            x = _mix_ffn_call(x, og, om, mods, wts["w_out"], wts["ln_g"], ffn16[l][1], wts["final_g"],
                              l, final_norm=last, **tile_ffn)
        else:
            cache_ckv, cache_kpe_t, place_t = stream
            om = _attn_stream_call(qh, cache_ckv, cache_kpe_t, ckv, kpe, wts["w_k"], wts["w_v"],
                                   place_t, l, nseq=nseq, tq=seq_len, past=cache_ckv.shape[2])
            x, w2 = _ffn_stream_call(x, (og, om, wts["w_out"]), mods, wts["ln_g"], wts["ffn2_in"],
                                     wts["ffn2_out"], wts["final_g"], l, nb=nb, ffn_idx=2,
                                     final_norm=last)
            cast.append((w1, w2))
        ckvs.append(ckv)
        kpes.append(kpe)
        states.append(st)
    return x, jnp.stack(ckvs), jnp.stack(kpes), jnp.stack(states), cast


def kernel(x_prompt, x_sample, cache_ckv, cache_kpe, state_gla, c_prompt, c_sample, ln_g, w_ada, b_ada, w_ffn1_in, w_ffn1_out, w_ffn2_in, w_ffn2_out, w_in, w_gk2, b_gk2, gla_norm_g, mla_qnorm_g, w_uq, mla_kvnorm_g, w_ukv, w_out, final_g):
    bsz, seq, _ = x_prompt.shape
    dbs, dseq, _ = x_sample.shape
    past = cache_ckv.shape[2]

    w_in_p, w_gk_p, b_gk_p, w_uq_p, w_k_p, w_v_p = _prep_weights(w_in, w_gk2, b_gk2, w_uq, w_ukv)
    wts = {
        "ln_g": ln_g.reshape(DEPTH, 3, 1, D_MODEL), "final_g": final_g.reshape(1, D_MODEL),
        "ffn1_in": w_ffn1_in, "ffn1_out": w_ffn1_out, "ffn2_in": w_ffn2_in, "ffn2_out": w_ffn2_out,
        "w_in": w_in_p, "w_gk": w_gk_p, "b_gk": b_gk_p, "w_uq": w_uq_p, "w_k": w_k_p, "w_v": w_v_p,
        "qg": mla_qnorm_g.reshape(DEPTH, 1, MLA_Q_LORA),
        "kvg": mla_kvnorm_g.reshape(DEPTH, 1, MLA_KV_LORA),
        "ng": gla_norm_g.reshape(DEPTH, 1, GLA_DV),
        "w_out": w_out.astype(BF16),
    }
    mods = _ada_call(jnp.concatenate([c_sample, c_prompt], axis=0), w_ada, b_ada)
    mods = mods.reshape(DEPTH, N_MOD, dbs + bsz, 1, D_MODEL)

    y_s, ckv_s, kpe_s, gla_s, ffn16 = _trunk(
        x_sample.reshape(dbs * dseq, D_MODEL), mods, 0, wts,
        _rope_tables(past + jnp.arange(dseq), dbs),
        nseq=dbs, seq_len=dseq, tm=dbs * dseq, tm_ffn=dbs * dseq, nb=dbs, gla_tl=dseq, gla_chunk=dseq,
        s0=state_gla.astype(F32),
        stream=(cache_ckv, jnp.swapaxes(cache_kpe, 2, 3), _kpe_placement()))

    tm_p = min(512, seq)
    y_p, ckv_p, kpe_p, gla_p, _ = _trunk(
        x_prompt.reshape(bsz * seq, D_MODEL), mods, dbs, wts, _rope_tables(jnp.arange(seq), 1),
        nseq=bsz, seq_len=seq, tm=tm_p, tm_ffn=min(FFN_TM, seq), nb=1, gla_tl=min(GLA_TL, seq), gla_chunk=CHUNK,
        s0=jnp.zeros((DEPTH, bsz, GLA_HEADS, GLA_DK, GLA_DV), F32), ffn16=ffn16)

    return (y_p.reshape(bsz, seq, D_MODEL), y_s.reshape(dbs, dseq, D_MODEL),
            ckv_p.reshape(DEPTH, bsz, seq, MLA_KV_LORA), jnp.swapaxes(kpe_p, 2, 3),
            gla_p,
            ckv_s.reshape(DEPTH, dbs, dseq, MLA_KV_LORA), kpe_s.reshape(DEPTH, dbs, dseq, MLA_ROPE),
            gla_s)
```

```python
import functools
import math

import numpy as np
import jax
import jax.numpy as jnp
from jax import lax
from jax.experimental import pallas as pl
from jax.experimental.pallas import tpu as pltpu

F32 = jnp.float32
BF16 = jnp.bfloat16

D_MODEL = 1024
DEPTH = 4
CHUNK = 64
EPS = 1e-6
GLA_HEADS = 4
GLA_DK = 64
GLA_DV = 128
GLA_GATE_RANK = 16
GLA_GATE_NORM = 16.0
GLA_SUB = 16
EXP_CLAMP = 80.0
MLA_HEADS = 8
MLA_Q_LORA = 384
MLA_KV_LORA = 256
MLA_NOPE = 64
MLA_ROPE = 32
MLA_V = 64
MLA_SCALE = 1.0 / math.sqrt(MLA_NOPE + MLA_ROPE)
LOG2E = math.log2(math.e)
ROPE_BASE = 10000.0
D_FF = 2816
N_MOD = 9
LANES = 128
HEAD_BLOCK = 128
NEG_BIG = -1e30
SCORE_SAFE = 60.0
NRM_ROWS = 8

GQK = GLA_HEADS * GLA_DK
GVW = GLA_HEADS * GLA_DV
MQK = MLA_HEADS * HEAD_BLOCK
MVW = MLA_HEADS * MLA_V
O_GQ, O_GK, O_GV = 0, GQK, 2 * GQK
O_GLR = O_GV + GVW
IN_COLS = O_GLR + GLA_GATE_RANK + GVW + MLA_Q_LORA + MLA_KV_LORA + MLA_ROPE
IN_W = -(-IN_COLS // LANES) * LANES
FFN_TF = 256
FFN_TP = 256
FFN_TM = 1024
GLA_TL = 1024
ATTN_TQ = 256
STREAM_SEQS = 4
CUMSUM_ROWS = 256
VMEM_LIMIT = 56 * 1024 * 1024


def _dot(a, b):
    return jnp.dot(a, b, preferred_element_type=F32)


def _dot_nt(a, b):
    return lax.dot_general(a, b, (((1,), (1,)), ((), ())), preferred_element_type=F32)


def _dot_tn(a, b):
    return lax.dot_general(a, b, (((0,), (0,)), ((), ())), preferred_element_type=F32)


def _rms(x, g):
    ms = jnp.mean(x * x, axis=-1, keepdims=True)
    return x * lax.rsqrt(ms + EPS) * g


def _modulate(x, g, sh, sc, nb):
    tm, d = x.shape
    y = _rms(x, g)
    if nb == 1:
        return y * (1.0 + sc[0]) + sh[0]
    y3 = y.reshape(nb, tm // nb, d)
    return (y3 * (1.0 + sc) + sh).reshape(tm, d)


def _gated_residual(x, y, coef, nb):
    tm, d = x.shape
    if nb == 1:
        return x + coef[0] * y
    return x + (coef * y.reshape(nb, tm // nb, d)).reshape(tm, d)


def _const_spec(shape):
    nd = len(shape)
    return pl.BlockSpec(shape, lambda *_: (0,) * nd, pipeline_mode=pl.Buffered(1))


def _layer_spec(shape, *lead):
    nd = len(shape)
    return pl.BlockSpec((None,) * len(lead) + tuple(shape), lambda *_: tuple(lead) + (0,) * nd,
                        pipeline_mode=pl.Buffered(1))


def _mod_spec(nmod, nb, tps, seq0, layer, blk):
    return pl.BlockSpec((None, nmod, nb, 1, D_MODEL), lambda i: (layer, blk, seq0 + i // tps, 0, 0))


def _params(sem):
    return pltpu.CompilerParams(dimension_semantics=sem, vmem_limit_bytes=VMEM_LIMIT)


ADA_GROUP = 3


def _ada_kernel(c_ref, w_ref, b_ref, o_ref):
    c = c_ref[...]
    sc = (c * jax.nn.sigmoid(c)).astype(BF16)
    for k in range(ADA_GROUP):
        w = w_ref[0, :, k * D_MODEL:(k + 1) * D_MODEL].astype(BF16)
        o_ref[0, k] = _dot(sc, w) + b_ref[0, k]


def _ada_call(c_all, w_ada, b_ada):
    nrow = c_all.shape[0]
    b4 = b_ada.reshape(DEPTH, N_MOD, 1, D_MODEL)
    return pl.pallas_call(
        _ada_kernel,
        grid=(DEPTH, N_MOD // ADA_GROUP),
        in_specs=[
            pl.BlockSpec((nrow, D_MODEL), lambda l, j: (0, 0)),
            pl.BlockSpec((1, D_MODEL, ADA_GROUP * D_MODEL), lambda l, j: (l, 0, j)),
            pl.BlockSpec((1, ADA_GROUP, 1, D_MODEL), lambda l, j: (l, j, 0, 0)),
        ],
        out_specs=pl.BlockSpec((1, ADA_GROUP, nrow, D_MODEL), lambda l, j: (l, j, 0, 0)),
        out_shape=jax.ShapeDtypeStruct((DEPTH, N_MOD, nrow, D_MODEL), F32),
        compiler_params=_params(("arbitrary", "arbitrary")),
        name="ada_mod",
    )(c_all, w_ada, b4)


def _ffn_body(x, sh, sc, gate, g, wa_ref, wb_ref, wo_ref, h_ref, nb):
    n = _modulate(x, g, sh, sc, nb).astype(BF16)
    for c in range(D_FF // FFN_TP):
        lo = c * FFN_TP
        a = _dot(n, wa_ref[:, lo:lo + FFN_TP])
        b = _dot(n, wb_ref[:, lo:lo + FFN_TP])
        h_ref[:, lo:lo + FFN_TP] = (a * jax.nn.sigmoid(a) * b).astype(BF16)
    y = _dot(h_ref[...], wo_ref[...])
    return _gated_residual(x, y, 0.5 * (1.0 + gate), nb)


def _mix_residual(x, og_ref, om_ref, gate, wmix_ref, nb):
    y = _dot(og_ref[...], wmix_ref[0:GVW, :]) + _dot(om_ref[...], wmix_ref[GVW:GVW + MVW, :])
    return _gated_residual(x, y, 1.0 + gate, nb)


def _ffn_kernel(x_ref, mod_ref, g_ref, wa_ref, wb_ref, wo_ref, o_ref, h_ref, *, nb):
    o_ref[...] = _ffn_body(x_ref[...], mod_ref[0], mod_ref[1], mod_ref[2], g_ref[...],
                           wa_ref, wb_ref, wo_ref, h_ref, nb)


def _mix_ffn_kernel(x_ref, og_ref, om_ref, gmix_ref, mod_ref, wmix_ref, g_ref, wa_ref, wb_ref, wo_ref,
                    fg_ref, o_ref, h_ref, *, nb, final_norm):
    x = _mix_residual(x_ref[...], og_ref, om_ref, gmix_ref[2], wmix_ref, nb)
    x = _ffn_body(x, mod_ref[0], mod_ref[1], mod_ref[2], g_ref[...], wa_ref, wb_ref, wo_ref, h_ref, nb)
    o_ref[...] = _rms(x, fg_ref[...]) if final_norm else x


def _ffn_weight_specs():
    return [_const_spec((D_MODEL, D_FF)), _const_spec((D_MODEL, D_FF)), _const_spec((D_FF, D_MODEL))]


def _ffn_call(x, mods, ln_g, w16, layer, *, tm, nb, tps, seq0):
    t = x.shape[0]
    return pl.pallas_call(
        functools.partial(_ffn_kernel, nb=nb),
        grid=(t // tm,),
        in_specs=[
            pl.BlockSpec((tm, D_MODEL), lambda i: (i, 0)),
            _mod_spec(3, nb, tps, seq0, layer, 0),
            _layer_spec((1, D_MODEL), layer, 0),
        ] + _ffn_weight_specs(),
        out_specs=pl.BlockSpec((tm, D_MODEL), lambda i: (i, 0)),
        out_shape=jax.ShapeDtypeStruct((t, D_MODEL), F32),
        scratch_shapes=[pltpu.VMEM((tm, D_FF), BF16)],
        compiler_params=_params(("arbitrary",)),
        name="ffn",
    )(x, mods, ln_g, *w16)


def _mix_ffn_call(x, og, om, mods, w_mix, ln_g, w16, fg, layer, *, tm, nb, tps, seq0, final_norm):
    t = x.shape[0]
    tok = lambda w: pl.BlockSpec((tm, w), lambda i: (i, 0))
    return pl.pallas_call(
        functools.partial(_mix_ffn_kernel, nb=nb, final_norm=final_norm),
        grid=(t // tm,),
        in_specs=[
            tok(D_MODEL), tok(GVW), tok(MVW),
            _mod_spec(3, nb, tps, seq0, layer, 1),
            _mod_spec(3, nb, tps, seq0, layer, 2),
            _layer_spec((GVW + MVW, D_MODEL), layer),
            _layer_spec((1, D_MODEL), layer, 2),
        ] + _ffn_weight_specs() + [_const_spec((1, D_MODEL))],
        out_specs=tok(D_MODEL),
        out_shape=jax.ShapeDtypeStruct((t, D_MODEL), F32),
        scratch_shapes=[pltpu.VMEM((tm, D_FF), BF16)],
        compiler_params=_params(("arbitrary",)),
        name="mix_ffn",
    )(x, og, om, mods, mods, w_mix, ln_g, *w16, fg)


def _ffn_stream_kernel(*refs, nb, has_mix, final_norm):
    if has_mix:
        x_ref, og_ref, om_ref, gmix_ref, mod_ref, wmix_ref, *refs = refs
    else:
        x_ref, mod_ref, *refs = refs
    (g_ref, wa_ref, wb_ref, wo_ref, fg_ref,
     y_ref, wa16_ref, wb16_ref, wo16_ref, xs_ref, n_ref, acc_ref) = refs
    c = pl.program_id(0)

    @pl.when(c == 0)
    def _():
        x = x_ref[...]
        if has_mix:
            x = _mix_residual(x, og_ref, om_ref, gmix_ref[2], wmix_ref, nb)
        xs_ref[...] = x
        n_ref[...] = _modulate(x, g_ref[...], mod_ref[0], mod_ref[1], nb).astype(BF16)
        acc_ref[...] = jnp.zeros_like(acc_ref)

    wa, wb, wo = wa_ref[...].astype(BF16), wb_ref[...].astype(BF16), wo_ref[...].astype(BF16)
    wa16_ref[...] = wa
    wb16_ref[...] = wb
    wo16_ref[...] = wo
    n = n_ref[...]
    a = _dot(n, wa)
    b = _dot(n, wb)
    acc_ref[...] += _dot((a * jax.nn.sigmoid(a) * b).astype(BF16), wo)

    @pl.when(c == pl.num_programs(0) - 1)
    def _():
        x = _gated_residual(xs_ref[...], acc_ref[...], 0.5 * (1.0 + mod_ref[2]), nb)
        y_ref[...] = _rms(x, fg_ref[...]) if final_norm else x


def _ffn_stream_call(x, mix, mods, ln_g, w_in, w_out, fg, layer, *, nb, ffn_idx, final_norm):
    tm = x.shape[0]
    nchunk = D_FF // FFN_TF
    fixed = lambda shape: pl.BlockSpec(shape, lambda c: (0,) * len(shape))
    mod = lambda blk: pl.BlockSpec((None, 3, nb, 1, D_MODEL), lambda c: (layer, blk, 0, 0, 0))
    if mix is None:
        args, specs = [x, mods], [fixed((tm, D_MODEL)), mod(ffn_idx)]
    else:
        og, om, w_mix = mix
        args = [x, og, om, mods, mods, w_mix]
        specs = [fixed((tm, D_MODEL)), fixed((tm, GVW)), fixed((tm, MVW)), mod(1), mod(ffn_idx),
                 _layer_spec((GVW + MVW, D_MODEL), layer)]
    args += [ln_g, w_in, w_in, w_out, fg]
    specs += [
        _layer_spec((1, D_MODEL), layer, ffn_idx),
        pl.BlockSpec((None, D_MODEL, FFN_TF), lambda c: (layer, 0, c)),
        pl.BlockSpec((None, D_MODEL, FFN_TF), lambda c: (layer, 0, nchunk + c)),
        pl.BlockSpec((None, FFN_TF, D_MODEL), lambda c: (layer, c, 0)),
        fixed((1, D_MODEL)),
    ]
    y, wa, wb, wo = pl.pallas_call(
        functools.partial(_ffn_stream_kernel, nb=nb, has_mix=mix is not None, final_norm=final_norm),
        grid=(nchunk,),
        in_specs=specs,
        out_specs=[fixed((tm, D_MODEL)),
                   pl.BlockSpec((D_MODEL, FFN_TF), lambda c: (0, c)),
                   pl.BlockSpec((D_MODEL, FFN_TF), lambda c: (0, c)),
                   pl.BlockSpec((FFN_TF, D_MODEL), lambda c: (c, 0))],
        out_shape=[jax.ShapeDtypeStruct((tm, D_MODEL), F32),
                   jax.ShapeDtypeStruct((D_MODEL, D_FF), BF16),
                   jax.ShapeDtypeStruct((D_MODEL, D_FF), BF16),
                   jax.ShapeDtypeStruct((D_FF, D_MODEL), BF16)],
        scratch_shapes=[pltpu.VMEM((tm, D_MODEL), F32), pltpu.VMEM((tm, D_MODEL), BF16),
                        pltpu.VMEM((tm, D_MODEL), F32)],
        compiler_params=_params(("arbitrary",)),
        name="ffn_stream",
    )(*args)
    return y, (wa, wb, wo)


def _swap_rope_halves(xb, lane):
    fwd = pltpu.roll(xb, LANES - MLA_ROPE // 2, axis=1)
    bwd = pltpu.roll(xb, MLA_ROPE // 2, axis=1)
    return jnp.where(lane < MLA_NOPE + MLA_ROPE // 2, fwd, bwd)


N_SHIFT = (GVW + MLA_Q_LORA + MLA_KV_LORA) // LANES


R_GOG = O_GLR
R_CQ = R_GOG + GVW
R_CKV = R_CQ + MLA_Q_LORA
R_TAIL = R_CKV + MLA_KV_LORA
TAIL_GLR = MLA_ROPE
Q_UNSCALE = 1.0 / (MLA_SCALE * LOG2E)
N_MIXIN_INPUTS = 16


def _mixin_kernel(x_ref, mod_ref, g_ref, win_ref, wgk_ref, bgk_ref, qg_ref, wuq_ref,
                  kvg_ref, wk_ref, wv_ref, ones_ref, cq_ref, sq_ref, ck_ref, sk_ref, *rest,
                  nb, kpe_rows, has_prev):
    (gq_ref, gk_ref, gv_ref, gog_ref, lg_ref, qh_ref, kh_ref, vh_ref,
     ckv_ref, kpe_ref, nrm_ref, wre_ref) = rest[1:] if has_prev else rest
    @pl.when(pl.program_id(0) == 0)
    def _():
        keep = LANES - GLA_GATE_RANK
        wl = lax.broadcasted_iota(jnp.int32, (D_MODEL, LANES), 1)
        wre_ref[:, 0:O_GLR] = win_ref[:, 0:O_GLR]
        gate_blk = win_ref[:, O_GLR:O_GLR + LANES].astype(F32)
        prev = pltpu.roll(gate_blk, keep, axis=1)
        for j in range(N_SHIFT):
            c0 = O_GLR + LANES * (j + 1)
            nxt = pltpu.roll(win_ref[:, c0:c0 + LANES].astype(F32), keep, axis=1)
            wre_ref[:, R_GOG + LANES * j:R_GOG + LANES * (j + 1)] = (
                jnp.where(wl < keep, prev, nxt).astype(BF16))
            prev = nxt
        tail = jnp.where(wl < keep, prev, 0.0)
        glr = pltpu.roll(gate_blk, TAIL_GLR, axis=1)
        tail = jnp.where((wl >= TAIL_GLR) & (wl < TAIL_GLR + GLA_GATE_RANK), glr, tail)
        wre_ref[:, R_TAIL:R_TAIL + LANES] = tail.astype(BF16)

    x = x_ref[...]
    tm = x.shape[0]
    n = _modulate(x, g_ref[...], mod_ref[0], mod_ref[1], nb).astype(BF16)
    lane = lax.broadcasted_iota(jnp.int32, (tm, LANES), 1)
    h = _dot(n, wre_ref[...])
    gq_ref[...] = h[:, O_GQ:O_GQ + GQK] * (GLA_DK ** -0.5)
    gk_ref[...] = h[:, O_GK:O_GK + GQK]
    gv_ref[...] = h[:, O_GV:O_GV + GVW]
    gog_ref[...] = h[:, R_GOG:R_GOG + GVW]
    tail = h[:, R_TAIL:R_TAIL + LANES]
    z = _dot(tail.astype(BF16), wgk_ref[...]) + bgk_ref[...]
    lsig = jnp.minimum(z, 0.0) - jnp.log1p(jnp.exp(-jnp.abs(z)))
    lg_ref[...] = lsig / GLA_GATE_NORM
    cqn = _rms(h[:, R_CQ:R_CQ + MLA_Q_LORA], qg_ref[...]).astype(BF16)
    q = _dot(cqn, wuq_ref[...])
    cq, sq = cq_ref[...], sq_ref[...]
    qr = jnp.concatenate(
        [q[:, hd * HEAD_BLOCK:(hd + 1) * HEAD_BLOCK] * cq
         + _swap_rope_halves(q[:, hd * HEAD_BLOCK:(hd + 1) * HEAD_BLOCK], lane) * sq
         for hd in range(MLA_HEADS)], axis=1)
    qh_ref[...] = qr.astype(BF16)
    ckv = _rms(h[:, R_CKV:R_CKV + MLA_KV_LORA], kvg_ref[...])
    ckv_ref[...] = ckv
    kb = pltpu.roll(tail, MLA_NOPE, axis=1)
    kb = jnp.where((lane >= MLA_NOPE) & (lane < MLA_NOPE + MLA_ROPE), kb, 0.0)
    kpe = kb * ck_ref[...] + _swap_rope_halves(kb, lane) * sk_ref[...]
    kpe0 = pltpu.roll(kpe, LANES - MLA_NOPE, axis=1)
    kpe_ref[...] = kpe0.T[:MLA_ROPE, :] if kpe_rows else kpe0[:, :MLA_ROPE]
    ckv16 = ckv.astype(BF16)
    kn = _dot(ckv16, wk_ref[...])
    kr = jnp.concatenate(
        [kn[:, hd * HEAD_BLOCK:(hd + 1) * HEAD_BLOCK] + kpe for hd in range(MLA_HEADS)], axis=1)
    kh_ref[...] = kr.astype(BF16)
    vh_ref[...] = _dot(ckv16, wv_ref[...]).astype(BF16)
    u = _dot((qr * qr * Q_UNSCALE + kr * kr * (1.0 / Q_UNSCALE)).astype(BF16), ones_ref[...])
    nrm_ref[0] = jnp.concatenate(
        [jnp.max(u, axis=0, keepdims=True), jnp.zeros((NRM_ROWS - 1, LANES), F32)], axis=0)


def _mixin_call(x, mods, ln_g, w_in, w_gk, b_gk, qg, w_uq, kvg, w_k, w_v, tabs, layer, ckv_all=None,
                *, tm, nb, tps, seq0, stack_ckv):
    t = x.shape[0]
    tok = lambda w: pl.BlockSpec((tm, w), lambda i: (i, 0))
    tab = pl.BlockSpec((tm, LANES), lambda i: (i % tps, 0))
    out_w = [(GQK, F32), (GQK, F32), (GVW, F32), (GVW, F32), (GQK, F32),
             (MQK, BF16), (MQK, BF16), (MVW, BF16)]
    if stack_ckv:
        ckv_spec = pl.BlockSpec((None, tm, MLA_KV_LORA), lambda i: (layer, i, 0))
        ckv_shape = jax.ShapeDtypeStruct((DEPTH, t, MLA_KV_LORA), F32)
    else:
        ckv_spec, ckv_shape = tok(MLA_KV_LORA), jax.ShapeDtypeStruct((t, MLA_KV_LORA), F32)
    has_prev = ckv_all is not None
    prev_args = [ckv_all] if has_prev else []
    prev_specs = [pl.BlockSpec(memory_space=pl.ANY)] if has_prev else []
    kpe_rows = nb == 1
    if kpe_rows:
        kpe_spec = pl.BlockSpec((None, MLA_ROPE, tm), lambda i: (i // tps, 0, i % tps))
        kpe_shape = jax.ShapeDtypeStruct((t // (tm * tps), MLA_ROPE, tm * tps), F32)
    else:
        kpe_spec, kpe_shape = tok(MLA_ROPE), jax.ShapeDtypeStruct((t, MLA_ROPE), F32)
    nrm_spec = pl.BlockSpec((1, NRM_ROWS, LANES), lambda i: (i, 0, 0))
    nrm_shape = jax.ShapeDtypeStruct((t // tm, NRM_ROWS, LANES), F32)
    head_ones = np.zeros((MQK, LANES), np.float32)
    head_ones[np.arange(MQK), np.arange(MQK) // HEAD_BLOCK] = 1.0
    return pl.pallas_call(
        functools.partial(_mixin_kernel, nb=nb, kpe_rows=kpe_rows, has_prev=has_prev),
        grid=(t // tm,),
        in_specs=[
            tok(D_MODEL),
            _mod_spec(3, nb, tps, seq0, layer, 1),
            _layer_spec((1, D_MODEL), layer, 1),
            _layer_spec((D_MODEL, IN_W), layer),
            _layer_spec((LANES, GQK), layer),
            _layer_spec((1, GQK), layer),
            _layer_spec((1, MLA_Q_LORA), layer),
            _layer_spec((MLA_Q_LORA, MQK), layer),
            _layer_spec((1, MLA_KV_LORA), layer),
            _layer_spec((MLA_KV_LORA, MQK), layer),
            _layer_spec((MLA_KV_LORA, MVW), layer),
            _const_spec((MQK, LANES)),
            tab, tab, tab, tab,
        ] + prev_specs,
        out_specs=[tok(w) for w, _ in out_w] + [ckv_spec, kpe_spec, nrm_spec],
        out_shape=([jax.ShapeDtypeStruct((t, w), dt) for w, dt in out_w]
                   + [ckv_shape, kpe_shape, nrm_shape]),
        input_output_aliases={N_MIXIN_INPUTS: len(out_w)} if has_prev else {},
        scratch_shapes=[pltpu.VMEM((D_MODEL, IN_W), BF16)],
        compiler_params=_params(("arbitrary",)),
        name="mixer_in",
    )(x, mods, ln_g, w_in, w_gk, b_gk, qg, w_uq, kvg, w_k, w_v, jnp.asarray(head_ones, BF16), *tabs,
      *prev_args)


def _gla_kernel(q_ref, k_ref, v_ref, og_ref, lg_ref, s0_ref, ng_ref, o_ref, sout_ref, st_ref,
                *, chunk, nch, ns):
    j = pl.program_id(1)
    nsub = chunk // GLA_SUB
    tl = chunk * nch
    lane_head = lax.broadcasted_iota(jnp.int32, (chunk, GQK), 1) // GLA_DK

    def stack_heads(a):
        return jnp.concatenate(
            [jnp.where(lane_head == hd, a, 0.0) for hd in range(GLA_HEADS)], axis=0).astype(BF16)

    @pl.when(j == 0)
    def _():
        for s in range(ns):
            for hd in range(GLA_HEADS):
                rows = [jnp.zeros((GLA_DK, GLA_DV), F32)] * GLA_HEADS
                rows[hd] = s0_ref[s, hd]
                st_ref[s, hd * GLA_DV:(hd + 1) * GLA_DV, :] = jnp.concatenate(rows, axis=0).T

    grp = min(tl, CUMSUM_ROWS)
    ri = lax.broadcasted_iota(jnp.int32, (grp, grp), 0)
    ci = lax.broadcasted_iota(jnp.int32, (grp, grp), 1)
    tri = jnp.where((ci <= ri) & (ci // chunk == ri // chunk), 1.0, 0.0).astype(BF16)
    b_parts = []
    for g in range(ns * tl // grp):
        lg = lg_ref[g * grp:(g + 1) * grp, :]
        p0 = lg.astype(BF16)
        r1 = lg - p0.astype(F32)
        p1 = r1.astype(BF16)
        p2 = (r1 - p1.astype(F32)).astype(BF16)
        b_parts.append(_dot(tri, p0) + _dot(tri, p1) + _dot(tri, p2))
    b_all = b_parts[0] if len(b_parts) == 1 else jnp.concatenate(b_parts, axis=0)

    rt = lax.broadcasted_iota(jnp.int32, (GLA_HEADS * chunk, nsub * chunk), 0) % chunk
    cc = lax.broadcasted_iota(jnp.int32, (GLA_HEADS * chunk, nsub * chunk), 1)
    keep = (cc // chunk == rt // GLA_SUB) & (cc % chunk <= rt)
    ng = ng_ref[...]

    def v_of(r0):
        return v_ref[r0:r0 + chunk, :].astype(BF16)

    start = {}
    for s in range(ns):
        st = st_ref[s]
        for c in range(nch):
            r0 = s * tl + c * chunk
            b = b_all[r0:r0 + chunk]
            kl = k_ref[r0:r0 + chunk, :] * jnp.exp(b[chunk - 1:chunk] - b)
            g_end = jnp.exp(b[chunk - 8:chunk])[7:8]
            start[r0] = st
            st = st * g_end + _dot_tn(v_of(r0), kl.astype(BF16))
        st_ref[s] = st

    for r0 in range(0, ns * tl, chunk):
        b = b_all[r0:r0 + chunk]
        q = q_ref[r0:r0 + chunk, :]
        k = k_ref[r0:r0 + chunk, :]
        r_own = jnp.broadcast_to(b[0:1], (chunk, GQK)) if nsub == 1 else jnp.concatenate(
            [jnp.broadcast_to(b[i * GLA_SUB:i * GLA_SUB + 1], (GLA_SUB, GQK)) for i in range(nsub)],
            axis=0)
        qt = q * jnp.exp(b - r_own)
        qe = q * jnp.exp(b)
        k_rel = [(k * jnp.exp(jnp.minimum(b[i * GLA_SUB:i * GLA_SUB + 1] - b, EXP_CLAMP))).astype(BF16)
                 for i in range(nsub)]
        att = _dot_nt(stack_heads(qt), jnp.concatenate(k_rel, axis=0))
        att = jnp.where(keep, att, 0.0).astype(BF16)
        v = v_of(r0)
        o_all = (_dot(att, jnp.concatenate([v] * nsub, axis=0))
                 + _dot_nt(stack_heads(qe), start[r0].astype(BF16)))
        for hd in range(GLA_HEADS):
            cols = slice(hd * GLA_DV, (hd + 1) * GLA_DV)
            o = o_all[hd * chunk:(hd + 1) * chunk, cols]
            og = og_ref[r0:r0 + chunk, cols]
            o_ref[r0:r0 + chunk, cols] = (_rms(o, ng) * (og * jax.nn.sigmoid(og))).astype(BF16)

    @pl.when(j == pl.num_programs(1) - 1)
    def _():
        for s in range(ns):
            for hd in range(GLA_HEADS):
                st_h = st_ref[s, hd * GLA_DV:(hd + 1) * GLA_DV, :]
                sout_ref[s, hd] = st_h.T[hd * GLA_DK:(hd + 1) * GLA_DK, :]


def _gla_call(gq, gk, gv, gog, lg, s0, ng, layer, *, nseq, seq_len, tl, chunk):
    t = gq.shape[0]
    tps = seq_len // tl
    ns = math.gcd(nseq, STREAM_SEQS) if tps == 1 else 1
    tok = lambda w: pl.BlockSpec((ns * tl, w), lambda b, j: (b * tps + j, 0))
    st_spec = pl.BlockSpec((ns, GLA_HEADS, GLA_DK, GLA_DV), lambda b, j: (b, 0, 0, 0))
    s0_spec = pl.BlockSpec((None, ns, GLA_HEADS, GLA_DK, GLA_DV), lambda b, j: (layer, b, 0, 0, 0))
    return pl.pallas_call(
        functools.partial(_gla_kernel, chunk=chunk, nch=tl // chunk, ns=ns),
        grid=(nseq // ns, tps),
        in_specs=[tok(GQK), tok(GQK), tok(GVW), tok(GVW), tok(GQK), s0_spec,
                  _layer_spec((1, GLA_DV), layer)],
        out_specs=[tok(GVW), st_spec],
        out_shape=[jax.ShapeDtypeStruct((t, GVW), BF16),
                   jax.ShapeDtypeStruct((nseq, GLA_HEADS, GLA_DK, GLA_DV), F32)],
        scratch_shapes=[pltpu.VMEM((ns, GVW, GQK), F32)],
        compiler_params=_params(("arbitrary", "arbitrary")),
        name="gla",
    )(gq, gk, gv, gog, lg, s0, ng)


def _softmax_pv(parts, vs, shift_by_max=True):
    if shift_by_max:
        m = functools.reduce(jnp.maximum, [jnp.max(s, axis=-1, keepdims=True) for s in parts])
    acc, den = None, None
    for s, v in zip(parts, vs):
        p = jnp.exp2(s - m) if shift_by_max else jnp.exp2(s)
        d = jnp.sum(p, axis=-1, keepdims=True)
        o = _dot(p.astype(BF16), v)
        acc = o if acc is None else acc + o
        den = d if den is None else den + d
    return acc / den


def _attn_prompt_kernel(q_ref, k_ref, v_ref, nrm_ref, o_ref, *, seq_len, tq):
    pair = pl.program_id(1)
    lane = lax.broadcasted_iota(jnp.int32, (tq, LANES), 1)
    qi = lax.broadcasted_iota(jnp.int32, (tq, tq), 0) // CHUNK
    ki = lax.broadcasted_iota(jnp.int32, (tq, tq), 1) // CHUNK
    diag_ok = ki <= qi

    def attend(shift_by_max):
        for i in range(seq_len // tq):
            r0 = i * tq
            outs = []
            for hh in range(2):
                cs = slice(hh * HEAD_BLOCK, (hh + 1) * HEAD_BLOCK)
                q = q_ref[r0:r0 + tq, cs]
                s_d = jnp.where(diag_ok, _dot_nt(q, k_ref[r0:r0 + tq, cs]), NEG_BIG)
                parts, vs = [s_d], [v_ref[r0:r0 + tq, :]]
                if i > 0:
                    parts.append(_dot_nt(q, k_ref[0:r0, cs]))
                    vs.append(v_ref[0:r0, :])
                outs.append(_softmax_pv(parts, vs, shift_by_max))
            o_ref[r0:r0 + tq, :] = jnp.where(lane < MLA_V, outs[0], outs[1]).astype(BF16)

    bound = 0.5 * jnp.max(nrm_ref[...], axis=0)[0:1]
    head_lane = lax.broadcasted_iota(jnp.int32, (1, LANES), 1) // 2
    safe = jnp.max(jnp.where(head_lane == pair, bound, 0.0)) <= SCORE_SAFE
    pl.when(safe)(functools.partial(attend, False))
    pl.when(jnp.logical_not(safe))(functools.partial(attend, True))


def _attn_prompt_call(qh, kh, vh, nrm, *, nseq, seq_len, tq):
    t = qh.shape[0]
    npair = MLA_HEADS // 2
    tps = nrm.shape[0] // nseq
    return pl.pallas_call(
        functools.partial(_attn_prompt_kernel, seq_len=seq_len, tq=tq),
        grid=(nseq, npair),
        in_specs=[
            pl.BlockSpec((seq_len, 2 * HEAD_BLOCK), lambda b, p: (b, p)),
            pl.BlockSpec((seq_len, 2 * HEAD_BLOCK), lambda b, p: (b, p)),
            pl.BlockSpec((seq_len, 2 * MLA_V), lambda b, p: (b, p)),
            pl.BlockSpec((tps, NRM_ROWS, LANES), lambda b, p: (b, 0, 0)),
        ],
        out_specs=pl.BlockSpec((seq_len, 2 * MLA_V), lambda b, p: (b, p)),
        out_shape=jax.ShapeDtypeStruct((t, MVW), BF16),
        compiler_params=_params(("arbitrary", "arbitrary")),
        name="mla_attn_prompt",
    )(qh, kh, vh, nrm)


def _attn_stream_kernel(q_ref, ckc_ref, kpc_ref, ckn_ref, kpn_ref, wk_ref, wv_ref, place_ref,
                        o_ref, *, tq, ns):
    place_t = place_ref[...]
    blocks = [slice(hd * HEAD_BLOCK, (hd + 1) * HEAD_BLOCK) for hd in range(MLA_HEADS)]
    lane = lax.broadcasted_iota(jnp.int32, (tq, LANES), 1)
    for s in range(ns):
        rows = slice(s * tq, (s + 1) * tq)
        q = q_ref[rows, :]
        ckc = ckc_ref[0, s].astype(BF16)
        kpc_t = _dot(place_t, kpc_ref[0, s].astype(BF16)).astype(BF16)
        ckn = ckn_ref[rows, :].astype(BF16)
        kpn = _dot_nt(kpn_ref[rows, :].astype(BF16), place_t).astype(BF16)
        q_abs = jnp.concatenate(
            [_dot_nt(q[:, bs], wk_ref[:, bs]) for bs in blocks], axis=0).astype(BF16)
        q_blk = jnp.concatenate([q[:, bs] for bs in blocks], axis=0)
        s_c = _dot_nt(q_abs, ckc) + _dot(q_blk, kpc_t)
        s_n = _dot_nt(q_abs, ckn) + _dot_nt(q_blk, kpn)
        lat = _softmax_pv([s_c, s_n], [ckc, ckn]).astype(BF16)
        for p in range(MLA_HEADS // 2):
            wv = wv_ref[:, p * LANES:(p + 1) * LANES]
            even = _dot(lat[(2 * p) * tq:(2 * p + 1) * tq], wv)
            odd = _dot(lat[(2 * p + 1) * tq:(2 * p + 2) * tq], wv)
            o_ref[rows, p * LANES:(p + 1) * LANES] = jnp.where(lane < MLA_V, even, odd).astype(BF16)


def _attn_stream_call(qh, cache_ckv, cache_kpe, ckv_new, kpe_new, w_k, w_v, place, layer,
                      *, nseq, tq, past):
    t = qh.shape[0]
    qpos = past + np.arange(tq)
    if past % CHUNK or not ((qpos[None, :] // CHUNK) <= (qpos[:, None] // CHUNK)).all():
        raise NotImplementedError("new frames spanning several chunks")
    ns = math.gcd(nseq, STREAM_SEQS)
    return pl.pallas_call(
        functools.partial(_attn_stream_kernel, tq=tq, ns=ns),
        grid=(nseq // ns,),
        in_specs=[
            pl.BlockSpec((ns * tq, MQK), lambda b: (b, 0)),
            pl.BlockSpec((1, ns, past, MLA_KV_LORA), lambda b: (layer, b, 0, 0)),
            pl.BlockSpec((1, ns, MLA_ROPE, past), lambda b: (layer, b, 0, 0)),
            pl.BlockSpec((ns * tq, MLA_KV_LORA), lambda b: (b, 0)),
            pl.BlockSpec((ns * tq, MLA_ROPE), lambda b: (b, 0)),
            _layer_spec((MLA_KV_LORA, MQK), layer),
            _layer_spec((MLA_KV_LORA, MVW), layer),
            _const_spec((HEAD_BLOCK, MLA_ROPE)),
        ],
        out_specs=pl.BlockSpec((ns * tq, MVW), lambda b: (b, 0)),
        out_shape=jax.ShapeDtypeStruct((t, MVW), BF16),
        compiler_params=_params(("arbitrary",)),
        name="mla_attn_stream",
    )(qh, cache_ckv, cache_kpe, ckv_new, kpe_new, w_k, w_v, place)


def _prep_weights(w_in, w_gk2, b_gk2, w_uq, w_ukv):
    w_in_p = jnp.pad(w_in.astype(BF16), ((0, 0), (0, 0), (0, IN_W - IN_COLS)))
    w_gk_p = jnp.pad(w_gk2.astype(BF16),
                     ((0, 0), (TAIL_GLR, LANES - TAIL_GLR - GLA_GATE_RANK), (0, 0)))
    b_gk_p = b_gk2.reshape(DEPTH, 1, GQK)
    uq = w_uq.astype(BF16).reshape(DEPTH, MLA_Q_LORA, MLA_HEADS, MLA_NOPE + MLA_ROPE)
    uq = jnp.pad(uq, ((0, 0), (0, 0), (0, 0), (0, HEAD_BLOCK - MLA_NOPE - MLA_ROPE)))
    w_uq_p = uq.reshape(DEPTH, MLA_Q_LORA, MQK)
    ukv = w_ukv.astype(BF16).reshape(DEPTH, MLA_KV_LORA, MLA_HEADS, MLA_NOPE + MLA_V)
    uk = jnp.pad(ukv[..., :MLA_NOPE], ((0, 0), (0, 0), (0, 0), (0, HEAD_BLOCK - MLA_NOPE)))
    w_k_p = uk.reshape(DEPTH, MLA_KV_LORA, MQK)
    w_v_p = ukv[..., MLA_NOPE:].reshape(DEPTH, MLA_KV_LORA, MVW)
    return w_in_p, w_gk_p, b_gk_p, w_uq_p, w_k_p, w_v_p


def _rope_tables(pos, reps):
    half = MLA_ROPE // 2
    inv = ROPE_BASE ** (-jnp.arange(half, dtype=F32) / half)
    ang = pos.astype(F32)[:, None] * inv[None, :]
    cos, sin = jnp.cos(ang), jnp.sin(ang)
    n = pos.shape[0]
    one, zero = jnp.ones((n, MLA_NOPE), F32), jnp.zeros((n, MLA_NOPE), F32)
    pad = jnp.zeros((n, HEAD_BLOCK - MLA_NOPE - MLA_ROPE), F32)
    c = jnp.concatenate([one, cos, cos, pad], axis=1)
    s = jnp.concatenate([zero, -sin, sin, pad], axis=1)
    q_scale = MLA_SCALE * LOG2E
    tabs = (c * q_scale, s * q_scale, c, s)
    return tuple(jnp.tile(a, (reps, 1)) for a in tabs)


def _kpe_placement():
    p = np.zeros((HEAD_BLOCK, MLA_ROPE), np.float32)
    p[MLA_NOPE + np.arange(MLA_ROPE), np.arange(MLA_ROPE)] = 1.0
    return jnp.asarray(p, BF16)


def _trunk(x, mods, seq0, wts, tabs, *, nseq, seq_len, tm, tm_ffn, nb, gla_tl, gla_chunk, s0,
           stream=None, ffn16=None):
    tps = max(seq_len // tm, 1) if nb == 1 else 1
    tile = dict(tm=tm, nb=nb, tps=tps, seq0=seq0)
    tile_ffn = dict(tile, tm=tm_ffn, tps=max(seq_len // tm_ffn, 1)) if nb == 1 else tile
    ckvs, kpes, states, cast = [], [], [], []
    ckv = None
    for l in range(DEPTH):
        if stream is None:
            x = _ffn_call(x, mods, wts["ln_g"], ffn16[l][0], l, **tile_ffn)
        else:
            x, w1 = _ffn_stream_call(x, None, mods, wts["ln_g"], wts["ffn1_in"], wts["ffn1_out"],
                                     wts["final_g"], l, nb=nb, ffn_idx=0, final_norm=False)
        gq, gk, gv, gog, lg, qh, kh, vh, ckv, kpe, nrm = _mixin_call(
            x, mods, wts["ln_g"], wts["w_in"], wts["w_gk"], wts["b_gk"], wts["qg"], wts["w_uq"],
            wts["kvg"], wts["w_k"], wts["w_v"], tabs, l, ckv if (stream is None and l > 0) else None,
            stack_ckv=stream is None, **tile)
        og, st = _gla_call(gq, gk, gv, gog, lg, s0, wts["ng"], l,
                           nseq=nseq, seq_len=seq_len, tl=gla_tl, chunk=gla_chunk)
        last = l == DEPTH - 1
        if stream is None:
            om = _attn_prompt_call(qh, kh, vh, nrm, nseq=nseq, seq_len=seq_len, tq=min(ATTN_TQ, seq_len))
            x = _mix_ffn_call(x, og, om, mods, wts["w_out"], wts["ln_g"], ffn16[l][1], wts["final_g"],
                              l, final_norm=last, **tile_ffn)
        else:
            cache_ckv, cache_kpe_t, place_t = stream
            om = _attn_stream_call(qh, cache_ckv, cache_kpe_t, ckv, kpe, wts["w_k"], wts["w_v"],
                                   place_t, l, nseq=nseq, tq=seq_len, past=cache_ckv.shape[2])
            x, w2 = _ffn_stream_call(x, (og, om, wts["w_out"]), mods, wts["ln_g"], wts["ffn2_in"],
                                     wts["ffn2_out"], wts["final_g"], l, nb=nb, ffn_idx=2,
                                     final_norm=last)
            cast.append((w1, w2))
        ckvs.append(ckv)
        kpes.append(kpe)
        states.append(st)
    ckv_out = ckv if stream is None else jnp.stack(ckvs)
    return x, ckv_out, jnp.stack(kpes), jnp.stack(states), cast


def kernel(x_prompt, x_sample, cache_ckv, cache_kpe, state_gla, c_prompt, c_sample, ln_g, w_ada, b_ada, w_ffn1_in, w_ffn1_out, w_ffn2_in, w_ffn2_out, w_in, w_gk2, b_gk2, gla_norm_g, mla_qnorm_g, w_uq, mla_kvnorm_g, w_ukv, w_out, final_g):
    bsz, seq, _ = x_prompt.shape
    dbs, dseq, _ = x_sample.shape
    past = cache_ckv.shape[2]

    w_in_p, w_gk_p, b_gk_p, w_uq_p, w_k_p, w_v_p = _prep_weights(w_in, w_gk2, b_gk2, w_uq, w_ukv)
    wts = {
        "ln_g": ln_g.reshape(DEPTH, 3, 1, D_MODEL), "final_g": final_g.reshape(1, D_MODEL),
        "ffn1_in": w_ffn1_in, "ffn1_out": w_ffn1_out, "ffn2_in": w_ffn2_in, "ffn2_out": w_ffn2_out,
        "w_in": w_in_p, "w_gk": w_gk_p, "b_gk": b_gk_p, "w_uq": w_uq_p, "w_k": w_k_p, "w_v": w_v_p,
        "qg": mla_qnorm_g.reshape(DEPTH, 1, MLA_Q_LORA),
        "kvg": mla_kvnorm_g.reshape(DEPTH, 1, MLA_KV_LORA),
        "ng": gla_norm_g.reshape(DEPTH, 1, GLA_DV),
        "w_out": w_out.astype(BF16),
    }
    mods = _ada_call(jnp.concatenate([c_sample, c_prompt], axis=0), w_ada, b_ada)
    mods = mods.reshape(DEPTH, N_MOD, dbs + bsz, 1, D_MODEL)

    y_s, ckv_s, kpe_s, gla_s, ffn16 = _trunk(
        x_sample.reshape(dbs * dseq, D_MODEL), mods, 0, wts,
        _rope_tables(past + jnp.arange(dseq), dbs),
        nseq=dbs, seq_len=dseq, tm=dbs * dseq, tm_ffn=dbs * dseq, nb=dbs, gla_tl=dseq, gla_chunk=dseq,
        s0=state_gla.astype(F32),
        stream=(cache_ckv, jnp.swapaxes(cache_kpe, 2, 3), _kpe_placement()))

    tm_p = min(512, seq)
    y_p, ckv_p, kpe_p, gla_p, _ = _trunk(
        x_prompt.reshape(bsz * seq, D_MODEL), mods, dbs, wts, _rope_tables(jnp.arange(seq), 1),
        nseq=bsz, seq_len=seq, tm=tm_p, tm_ffn=min(FFN_TM, seq), nb=1, gla_tl=min(GLA_TL, seq), gla_chunk=CHUNK,
        s0=jnp.zeros((DEPTH, bsz, GLA_HEADS, GLA_DK, GLA_DV), F32), ffn16=ffn16)

    return (y_p.reshape(bsz, seq, D_MODEL), y_s.reshape(dbs, dseq, D_MODEL),
            ckv_p.reshape(DEPTH, bsz, seq, MLA_KV_LORA), jnp.swapaxes(kpe_p, 2, 3),
            gla_p,
            ckv_s.reshape(DEPTH, dbs, dseq, MLA_KV_LORA), kpe_s.reshape(DEPTH, dbs, dseq, MLA_ROPE),
            gla_s)
```

```python
import functools
import math

import numpy as np
import jax
import jax.numpy as jnp
from jax import lax
from jax.experimental import pallas as pl
from jax.experimental.pallas import tpu as pltpu

F32 = jnp.float32
BF16 = jnp.bfloat16

D_MODEL = 1024
DEPTH = 4
CHUNK = 64
EPS = 1e-6
GLA_HEADS = 4
GLA_DK = 64
GLA_DV = 128
GLA_GATE_RANK = 16
GLA_GATE_NORM = 16.0
GLA_SUB = 16
EXP_CLAMP = 80.0
MLA_HEADS = 8
MLA_Q_LORA = 384
MLA_KV_LORA = 256
MLA_NOPE = 64
MLA_ROPE = 32
MLA_V = 64
MLA_SCALE = 1.0 / math.sqrt(MLA_NOPE + MLA_ROPE)
LOG2E = math.log2(math.e)
ROPE_BASE = 10000.0
D_FF = 2816
N_MOD = 9
LANES = 128
HEAD_BLOCK = 128
NEG_BIG = -1e30
SCORE_SAFE = 60.0
NRM_ROWS = 8

GQK = GLA_HEADS * GLA_DK
GVW = GLA_HEADS * GLA_DV
MQK = MLA_HEADS * HEAD_BLOCK
MVW = MLA_HEADS * MLA_V
O_GQ, O_GK, O_GV = 0, GQK, 2 * GQK
O_GLR = O_GV + GVW
IN_COLS = O_GLR + GLA_GATE_RANK + GVW + MLA_Q_LORA + MLA_KV_LORA + MLA_ROPE
IN_W = -(-IN_COLS // LANES) * LANES
FFN_TF = 256
FFN_TP = 256
FFN_TM = 1024
GLA_TL = 1024
ATTN_TQ = 256
ATTN_PAIRS = 2
STREAM_SEQS = 4
CUMSUM_ROWS = 256
VMEM_LIMIT = 56 * 1024 * 1024


def _dot(a, b):
    return jnp.dot(a, b, preferred_element_type=F32)


def _dot_nt(a, b):
    return lax.dot_general(a, b, (((1,), (1,)), ((), ())), preferred_element_type=F32)


def _dot_tn(a, b):
    return lax.dot_general(a, b, (((0,), (0,)), ((), ())), preferred_element_type=F32)


def _rms(x, g):
    ms = jnp.mean(x * x, axis=-1, keepdims=True)
    return x * lax.rsqrt(ms + EPS) * g


def _modulate(x, g, sh, sc, nb):
    tm, d = x.shape
    y = _rms(x, g)
    if nb == 1:
        return y * (1.0 + sc[0]) + sh[0]
    y3 = y.reshape(nb, tm // nb, d)
    return (y3 * (1.0 + sc) + sh).reshape(tm, d)


def _gated_residual(x, y, coef, nb):
    tm, d = x.shape
    if nb == 1:
        return x + coef[0] * y
    return x + (coef * y.reshape(nb, tm // nb, d)).reshape(tm, d)


def _const_spec(shape):
    nd = len(shape)
    return pl.BlockSpec(shape, lambda *_: (0,) * nd, pipeline_mode=pl.Buffered(1))


def _layer_spec(shape, *lead):
    nd = len(shape)
    return pl.BlockSpec((None,) * len(lead) + tuple(shape), lambda *_: tuple(lead) + (0,) * nd,
                        pipeline_mode=pl.Buffered(1))


def _mod_spec(nmod, nb, tps, seq0, layer, blk):
    return pl.BlockSpec((None, nmod, nb, 1, D_MODEL), lambda i: (layer, blk, seq0 + i // tps, 0, 0))


def _params(sem):
    return pltpu.CompilerParams(dimension_semantics=sem, vmem_limit_bytes=VMEM_LIMIT)


ADA_GROUP = 3


def _ada_kernel(c_ref, w_ref, b_ref, o_ref):
    c = c_ref[...]
    sc = (c * jax.nn.sigmoid(c)).astype(BF16)
    for k in range(ADA_GROUP):
        w = w_ref[0, :, k * D_MODEL:(k + 1) * D_MODEL].astype(BF16)
        o_ref[0, k] = _dot(sc, w) + b_ref[0, k]


def _ada_call(c_all, w_ada, b_ada):
    nrow = c_all.shape[0]
    b4 = b_ada.reshape(DEPTH, N_MOD, 1, D_MODEL)
    return pl.pallas_call(
        _ada_kernel,
        grid=(DEPTH, N_MOD // ADA_GROUP),
        in_specs=[
            pl.BlockSpec((nrow, D_MODEL), lambda l, j: (0, 0)),
            pl.BlockSpec((1, D_MODEL, ADA_GROUP * D_MODEL), lambda l, j: (l, 0, j)),
            pl.BlockSpec((1, ADA_GROUP, 1, D_MODEL), lambda l, j: (l, j, 0, 0)),
        ],
        out_specs=pl.BlockSpec((1, ADA_GROUP, nrow, D_MODEL), lambda l, j: (l, j, 0, 0)),
        out_shape=jax.ShapeDtypeStruct((DEPTH, N_MOD, nrow, D_MODEL), F32),
        compiler_params=_params(("arbitrary", "arbitrary")),
        name="ada_mod",
    )(c_all, w_ada, b4)


def _ffn_body(x, sh, sc, gate, g, wa_ref, wb_ref, wo_ref, h_ref, nb):
    n = _modulate(x, g, sh, sc, nb).astype(BF16)
    for c in range(D_FF // FFN_TP):
        lo = c * FFN_TP
        a = _dot(n, wa_ref[:, lo:lo + FFN_TP])
        b = _dot(n, wb_ref[:, lo:lo + FFN_TP])
        h_ref[:, lo:lo + FFN_TP] = (a * jax.nn.sigmoid(a) * b).astype(BF16)
    y = _dot(h_ref[...], wo_ref[...])
    return _gated_residual(x, y, 0.5 * (1.0 + gate), nb)


def _mix_residual(x, og_ref, om_ref, gate, wmix_ref, nb):
    y = _dot(og_ref[...], wmix_ref[0:GVW, :]) + _dot(om_ref[...], wmix_ref[GVW:GVW + MVW, :])
    return _gated_residual(x, y, 1.0 + gate, nb)


def _ffn_kernel(x_ref, mod_ref, g_ref, wa_ref, wb_ref, wo_ref, o_ref, h_ref, *, nb):
    o_ref[...] = _ffn_body(x_ref[...], mod_ref[0], mod_ref[1], mod_ref[2], g_ref[...],
                           wa_ref, wb_ref, wo_ref, h_ref, nb)


def _mix_ffn_kernel(x_ref, og_ref, om_ref, gmix_ref, mod_ref, wmix_ref, g_ref, wa_ref, wb_ref, wo_ref,
                    fg_ref, o_ref, h_ref, *, nb, final_norm):
    x = _mix_residual(x_ref[...], og_ref, om_ref, gmix_ref[2], wmix_ref, nb)
    x = _ffn_body(x, mod_ref[0], mod_ref[1], mod_ref[2], g_ref[...], wa_ref, wb_ref, wo_ref, h_ref, nb)
    o_ref[...] = _rms(x, fg_ref[...]) if final_norm else x


def _ffn_weight_specs():
    return [_const_spec((D_MODEL, D_FF)), _const_spec((D_MODEL, D_FF)), _const_spec((D_FF, D_MODEL))]


def _ffn_call(x, mods, ln_g, w16, layer, *, tm, nb, tps, seq0):
    t = x.shape[0]
    return pl.pallas_call(
        functools.partial(_ffn_kernel, nb=nb),
        grid=(t // tm,),
        in_specs=[
            pl.BlockSpec((tm, D_MODEL), lambda i: (i, 0)),
            _mod_spec(3, nb, tps, seq0, layer, 0),
            _layer_spec((1, D_MODEL), layer, 0),
        ] + _ffn_weight_specs(),
        out_specs=pl.BlockSpec((tm, D_MODEL), lambda i: (i, 0)),
        out_shape=jax.ShapeDtypeStruct((t, D_MODEL), F32),
        scratch_shapes=[pltpu.VMEM((tm, D_FF), BF16)],
        compiler_params=_params(("arbitrary",)),
        name="ffn",
    )(x, mods, ln_g, *w16)


def _mix_ffn_call(x, og, om, mods, w_mix, ln_g, w16, fg, layer, *, tm, nb, tps, seq0, final_norm):
    t = x.shape[0]
    tok = lambda w: pl.BlockSpec((tm, w), lambda i: (i, 0))
    return pl.pallas_call(
        functools.partial(_mix_ffn_kernel, nb=nb, final_norm=final_norm),
        grid=(t // tm,),
        in_specs=[
            tok(D_MODEL), tok(GVW), tok(MVW),
            _mod_spec(3, nb, tps, seq0, layer, 1),
            _mod_spec(3, nb, tps, seq0, layer, 2),
            _layer_spec((GVW + MVW, D_MODEL), layer),
            _layer_spec((1, D_MODEL), layer, 2),
        ] + _ffn_weight_specs() + [_const_spec((1, D_MODEL))],
        out_specs=tok(D_MODEL),
        out_shape=jax.ShapeDtypeStruct((t, D_MODEL), F32),
        scratch_shapes=[pltpu.VMEM((tm, D_FF), BF16)],
        compiler_params=_params(("arbitrary",)),
        name="mix_ffn",
    )(x, og, om, mods, mods, w_mix, ln_g, *w16, fg)


def _ffn_stream_kernel(*refs, nb, has_mix, final_norm):
    if has_mix:
        x_ref, og_ref, om_ref, gmix_ref, mod_ref, wmix_ref, *refs = refs
    else:
        x_ref, mod_ref, *refs = refs
    (g_ref, wa_ref, wb_ref, wo_ref, fg_ref,
     y_ref, wa16_ref, wb16_ref, wo16_ref, xs_ref, n_ref, acc_ref) = refs
    c = pl.program_id(0)

    @pl.when(c == 0)
    def _():
        x = x_ref[...]
        if has_mix:
            x = _mix_residual(x, og_ref, om_ref, gmix_ref[2], wmix_ref, nb)
        xs_ref[...] = x
        n_ref[...] = _modulate(x, g_ref[...], mod_ref[0], mod_ref[1], nb).astype(BF16)
        acc_ref[...] = jnp.zeros_like(acc_ref)

    wa, wb, wo = wa_ref[...].astype(BF16), wb_ref[...].astype(BF16), wo_ref[...].astype(BF16)
    wa16_ref[...] = wa
    wb16_ref[...] = wb
    wo16_ref[...] = wo
    n = n_ref[...]
    a = _dot(n, wa)
    b = _dot(n, wb)
    acc_ref[...] += _dot((a * jax.nn.sigmoid(a) * b).astype(BF16), wo)

    @pl.when(c == pl.num_programs(0) - 1)
    def _():
        x = _gated_residual(xs_ref[...], acc_ref[...], 0.5 * (1.0 + mod_ref[2]), nb)
        y_ref[...] = _rms(x, fg_ref[...]) if final_norm else x


def _ffn_stream_call(x, mix, mods, ln_g, w_in, w_out, fg, layer, *, nb, ffn_idx, final_norm):
    tm = x.shape[0]
    nchunk = D_FF // FFN_TF
    fixed = lambda shape: pl.BlockSpec(shape, lambda c: (0,) * len(shape))
    mod = lambda blk: pl.BlockSpec((None, 3, nb, 1, D_MODEL), lambda c: (layer, blk, 0, 0, 0))
    if mix is None:
        args, specs = [x, mods], [fixed((tm, D_MODEL)), mod(ffn_idx)]
    else:
        og, om, w_mix = mix
        args = [x, og, om, mods, mods, w_mix]
        specs = [fixed((tm, D_MODEL)), fixed((tm, GVW)), fixed((tm, MVW)), mod(1), mod(ffn_idx),
                 _layer_spec((GVW + MVW, D_MODEL), layer)]
    args += [ln_g, w_in, w_in, w_out, fg]
    specs += [
        _layer_spec((1, D_MODEL), layer, ffn_idx),
        pl.BlockSpec((None, D_MODEL, FFN_TF), lambda c: (layer, 0, c)),
        pl.BlockSpec((None, D_MODEL, FFN_TF), lambda c: (layer, 0, nchunk + c)),
        pl.BlockSpec((None, FFN_TF, D_MODEL), lambda c: (layer, c, 0)),
        fixed((1, D_MODEL)),
    ]
    y, wa, wb, wo = pl.pallas_call(
        functools.partial(_ffn_stream_kernel, nb=nb, has_mix=mix is not None, final_norm=final_norm),
        grid=(nchunk,),
        in_specs=specs,
        out_specs=[fixed((tm, D_MODEL)),
                   pl.BlockSpec((D_MODEL, FFN_TF), lambda c: (0, c)),
                   pl.BlockSpec((D_MODEL, FFN_TF), lambda c: (0, c)),
                   pl.BlockSpec((FFN_TF, D_MODEL), lambda c: (c, 0))],
        out_shape=[jax.ShapeDtypeStruct((tm, D_MODEL), F32),
                   jax.ShapeDtypeStruct((D_MODEL, D_FF), BF16),
                   jax.ShapeDtypeStruct((D_MODEL, D_FF), BF16),
                   jax.ShapeDtypeStruct((D_FF, D_MODEL), BF16)],
        scratch_shapes=[pltpu.VMEM((tm, D_MODEL), F32), pltpu.VMEM((tm, D_MODEL), BF16),
                        pltpu.VMEM((tm, D_MODEL), F32)],
        compiler_params=_params(("arbitrary",)),
        name="ffn_stream",
    )(*args)
    return y, (wa, wb, wo)


def _swap_rope_halves(xb, lane):
    fwd = pltpu.roll(xb, LANES - MLA_ROPE // 2, axis=1)
    bwd = pltpu.roll(xb, MLA_ROPE // 2, axis=1)
    return jnp.where(lane < MLA_NOPE + MLA_ROPE // 2, fwd, bwd)


N_SHIFT = (GVW + MLA_Q_LORA + MLA_KV_LORA) // LANES


R_GOG = O_GLR
R_CQ = R_GOG + GVW
R_CKV = R_CQ + MLA_Q_LORA
R_TAIL = R_CKV + MLA_KV_LORA
TAIL_GLR = MLA_ROPE
Q_UNSCALE = 1.0 / (MLA_SCALE * LOG2E)
N_MIXIN_INPUTS = 16


def _mixin_kernel(x_ref, mod_ref, g_ref, win_ref, wgk_ref, bgk_ref, qg_ref, wuq_ref,
                  kvg_ref, wk_ref, wv_ref, ones_ref, cq_ref, sq_ref, ck_ref, sk_ref, *rest,
                  nb, kpe_rows, has_prev, ckv_first):
    (gq_ref, gk_ref, gv_ref, gog_ref, lg_ref, qh_ref, kh_ref, vh_ref,
     ckv_ref, kpe_ref, nrm_ref, wre_ref) = rest[1:] if has_prev else rest
    @pl.when(pl.program_id(0) == 0)
    def _():
        keep = LANES - GLA_GATE_RANK
        wl = lax.broadcasted_iota(jnp.int32, (D_MODEL, LANES), 1)
        wre_ref[:, 0:O_GLR] = win_ref[:, 0:O_GLR]
        gate_blk = win_ref[:, O_GLR:O_GLR + LANES].astype(F32)
        prev = pltpu.roll(gate_blk, keep, axis=1)
        for j in range(N_SHIFT):
            c0 = O_GLR + LANES * (j + 1)
            nxt = pltpu.roll(win_ref[:, c0:c0 + LANES].astype(F32), keep, axis=1)
            wre_ref[:, R_GOG + LANES * j:R_GOG + LANES * (j + 1)] = (
                jnp.where(wl < keep, prev, nxt).astype(BF16))
            prev = nxt
        tail = jnp.where(wl < keep, prev, 0.0)
        glr = pltpu.roll(gate_blk, TAIL_GLR, axis=1)
        tail = jnp.where((wl >= TAIL_GLR) & (wl < TAIL_GLR + GLA_GATE_RANK), glr, tail)
        wre_ref[:, R_TAIL:R_TAIL + LANES] = tail.astype(BF16)

    x = x_ref[...]
    tm = x.shape[0]
    n = _modulate(x, g_ref[...], mod_ref[0], mod_ref[1], nb).astype(BF16)
    lane = lax.broadcasted_iota(jnp.int32, (tm, LANES), 1)
    h = _dot(n, wre_ref[...])
    gq_ref[...] = h[:, O_GQ:O_GQ + GQK] * (GLA_DK ** -0.5)
    gk_ref[...] = h[:, O_GK:O_GK + GQK]
    gv_ref[...] = h[:, O_GV:O_GV + GVW]
    gog_ref[...] = h[:, R_GOG:R_GOG + GVW]
    tail = h[:, R_TAIL:R_TAIL + LANES]
    z = _dot(tail.astype(BF16), wgk_ref[...]) + bgk_ref[...]
    lsig = jnp.minimum(z, 0.0) - jnp.log1p(jnp.exp(-jnp.abs(z)))
    lg_ref[...] = lsig / GLA_GATE_NORM
    cqn = _rms(h[:, R_CQ:R_CQ + MLA_Q_LORA], qg_ref[...]).astype(BF16)
    q = _dot(cqn, wuq_ref[...])
    cq, sq = cq_ref[...], sq_ref[...]
    qr = jnp.concatenate(
        [q[:, hd * HEAD_BLOCK:(hd + 1) * HEAD_BLOCK] * cq
         + _swap_rope_halves(q[:, hd * HEAD_BLOCK:(hd + 1) * HEAD_BLOCK], lane) * sq
         for hd in range(MLA_HEADS)], axis=1)
    qh_ref[...] = qr.astype(BF16)
    ckv = _rms(h[:, R_CKV:R_CKV + MLA_KV_LORA], kvg_ref[...])
    if ckv_first:
        ckv_ref[0] = ckv
        ckv_ref[1:] = jnp.zeros((DEPTH - 1,) + ckv.shape, F32)
    else:
        ckv_ref[...] = ckv
    kb = pltpu.roll(tail, MLA_NOPE, axis=1)
    kb = jnp.where((lane >= MLA_NOPE) & (lane < MLA_NOPE + MLA_ROPE), kb, 0.0)
    kpe = kb * ck_ref[...] + _swap_rope_halves(kb, lane) * sk_ref[...]
    kpe0 = pltpu.roll(kpe, LANES - MLA_NOPE, axis=1)
    kpe_ref[...] = kpe0.T[:MLA_ROPE, :] if kpe_rows else kpe0[:, :MLA_ROPE]
    ckv16 = ckv.astype(BF16)
    kn = _dot(ckv16, wk_ref[...])
    kr = jnp.concatenate(
        [kn[:, hd * HEAD_BLOCK:(hd + 1) * HEAD_BLOCK] + kpe for hd in range(MLA_HEADS)], axis=1)
    kh_ref[...] = kr.astype(BF16)
    vh_ref[...] = _dot(ckv16, wv_ref[...]).astype(BF16)
    u = _dot((qr * qr * Q_UNSCALE + kr * kr * (1.0 / Q_UNSCALE)).astype(BF16), ones_ref[...])
    nrm_ref[0] = jnp.concatenate(
        [jnp.max(u, axis=0, keepdims=True), jnp.zeros((NRM_ROWS - 1, LANES), F32)], axis=0)


def _mixin_call(x, mods, ln_g, w_in, w_gk, b_gk, qg, w_uq, kvg, w_k, w_v, tabs, layer, ckv_all=None,
                *, tm, nb, tps, seq0, stack_ckv):
    t = x.shape[0]
    tok = lambda w: pl.BlockSpec((tm, w), lambda i: (i, 0))
    tab = pl.BlockSpec((tm, LANES), lambda i: (i % tps, 0))
    out_w = [(GQK, F32), (GQK, F32), (GVW, F32), (GVW, F32), (GQK, F32),
             (MQK, BF16), (MQK, BF16), (MVW, BF16)]
    has_prev = ckv_all is not None
    ckv_first = stack_ckv and not has_prev
    if ckv_first:
        ckv_spec = pl.BlockSpec((DEPTH, tm, MLA_KV_LORA), lambda i: (0, i, 0))
        ckv_shape = jax.ShapeDtypeStruct((DEPTH, t, MLA_KV_LORA), F32)
    elif stack_ckv:
        ckv_spec = pl.BlockSpec((None, tm, MLA_KV_LORA), lambda i: (layer, i, 0))
        ckv_shape = jax.ShapeDtypeStruct((DEPTH, t, MLA_KV_LORA), F32)
    else:
        ckv_spec, ckv_shape = tok(MLA_KV_LORA), jax.ShapeDtypeStruct((t, MLA_KV_LORA), F32)
    prev_args = [ckv_all] if has_prev else []
    prev_specs = [pl.BlockSpec(memory_space=pl.ANY)] if has_prev else []
    kpe_rows = nb == 1
    if kpe_rows:
        kpe_spec = pl.BlockSpec((None, MLA_ROPE, tm), lambda i: (i // tps, 0, i % tps))
        kpe_shape = jax.ShapeDtypeStruct((t // (tm * tps), MLA_ROPE, tm * tps), F32)
    else:
        kpe_spec, kpe_shape = tok(MLA_ROPE), jax.ShapeDtypeStruct((t, MLA_ROPE), F32)
    nrm_spec = pl.BlockSpec((1, NRM_ROWS, LANES), lambda i: (i, 0, 0))
    nrm_shape = jax.ShapeDtypeStruct((t // tm, NRM_ROWS, LANES), F32)
    head_ones = np.zeros((MQK, LANES), np.float32)
    head_ones[np.arange(MQK), np.arange(MQK) // HEAD_BLOCK] = 1.0
    return pl.pallas_call(
        functools.partial(_mixin_kernel, nb=nb, kpe_rows=kpe_rows, has_prev=has_prev, ckv_first=ckv_first),
        grid=(t // tm,),
        in_specs=[
            tok(D_MODEL),
            _mod_spec(3, nb, tps, seq0, layer, 1),
            _layer_spec((1, D_MODEL), layer, 1),
            _layer_spec((D_MODEL, IN_W), layer),
            _layer_spec((LANES, GQK), layer),
            _layer_spec((1, GQK), layer),
            _layer_spec((1, MLA_Q_LORA), layer),
            _layer_spec((MLA_Q_LORA, MQK), layer),
            _layer_spec((1, MLA_KV_LORA), layer),
            _layer_spec((MLA_KV_LORA, MQK), layer),
            _layer_spec((MLA_KV_LORA, MVW), layer),
            _const_spec((MQK, LANES)),
            tab, tab, tab, tab,
        ] + prev_specs,
        out_specs=[tok(w) for w, _ in out_w] + [ckv_spec, kpe_spec, nrm_spec],
        out_shape=([jax.ShapeDtypeStruct((t, w), dt) for w, dt in out_w]
                   + [ckv_shape, kpe_shape, nrm_shape]),
        input_output_aliases={N_MIXIN_INPUTS: len(out_w)} if has_prev else {},
        scratch_shapes=[pltpu.VMEM((D_MODEL, IN_W), BF16)],
        compiler_params=_params(("arbitrary",)),
        name="mixer_in",
    )(x, mods, ln_g, w_in, w_gk, b_gk, qg, w_uq, kvg, w_k, w_v, jnp.asarray(head_ones, BF16), *tabs,
      *prev_args)


def _gla_kernel(q_ref, k_ref, v_ref, og_ref, lg_ref, s0_ref, ng_ref, o_ref, sout_ref, st_ref,
                *, chunk, nch, ns):
    j = pl.program_id(1)
    nsub = chunk // GLA_SUB
    tl = chunk * nch
    lane_head = lax.broadcasted_iota(jnp.int32, (chunk, GQK), 1) // GLA_DK

    def stack_heads(a):
        return jnp.concatenate(
            [jnp.where(lane_head == hd, a, 0.0) for hd in range(GLA_HEADS)], axis=0).astype(BF16)

    @pl.when(j == 0)
    def _():
        for s in range(ns):
            for hd in range(GLA_HEADS):
                rows = [jnp.zeros((GLA_DK, GLA_DV), F32)] * GLA_HEADS
                rows[hd] = s0_ref[s, hd]
                st_ref[s, hd * GLA_DV:(hd + 1) * GLA_DV, :] = jnp.concatenate(rows, axis=0).T

    grp = min(tl, CUMSUM_ROWS)
    ri = lax.broadcasted_iota(jnp.int32, (grp, grp), 0)
    ci = lax.broadcasted_iota(jnp.int32, (grp, grp), 1)
    tri = jnp.where((ci <= ri) & (ci // chunk == ri // chunk), 1.0, 0.0).astype(BF16)
    b_parts = []
    for g in range(ns * tl // grp):
        lg = lg_ref[g * grp:(g + 1) * grp, :]
        p0 = lg.astype(BF16)
        r1 = lg - p0.astype(F32)
        p1 = r1.astype(BF16)
        p2 = (r1 - p1.astype(F32)).astype(BF16)
        b_parts.append(_dot(tri, p0) + _dot(tri, p1) + _dot(tri, p2))
    b_all = b_parts[0] if len(b_parts) == 1 else jnp.concatenate(b_parts, axis=0)

    rt = lax.broadcasted_iota(jnp.int32, (GLA_HEADS * chunk, nsub * chunk), 0) % chunk
    cc = lax.broadcasted_iota(jnp.int32, (GLA_HEADS * chunk, nsub * chunk), 1)
    keep = (cc // chunk == rt // GLA_SUB) & (cc % chunk <= rt)
    ng = ng_ref[...]

    def v_of(r0):
        return v_ref[r0:r0 + chunk, :].astype(BF16)

    start = {}
    for s in range(ns):
        st = st_ref[s]
        for c in range(nch):
            r0 = s * tl + c * chunk
            b = b_all[r0:r0 + chunk]
            kl = k_ref[r0:r0 + chunk, :] * jnp.exp(b[chunk - 1:chunk] - b)
            g_end = jnp.exp(b[chunk - 8:chunk])[7:8]
            start[r0] = st
            st = st * g_end + _dot_tn(v_of(r0), kl.astype(BF16))
        st_ref[s] = st

    for r0 in range(0, ns * tl, chunk):
        b = b_all[r0:r0 + chunk]
        q = q_ref[r0:r0 + chunk, :]
        k = k_ref[r0:r0 + chunk, :]
        r_own = jnp.broadcast_to(b[0:1], (chunk, GQK)) if nsub == 1 else jnp.concatenate(
            [jnp.broadcast_to(b[i * GLA_SUB:i * GLA_SUB + 1], (GLA_SUB, GQK)) for i in range(nsub)],
            axis=0)
        qt = q * jnp.exp(b - r_own)
        qe = q * jnp.exp(b)
        k_rel = [(k * jnp.exp(jnp.minimum(b[i * GLA_SUB:i * GLA_SUB + 1] - b, EXP_CLAMP))).astype(BF16)
                 for i in range(nsub)]
        att = _dot_nt(stack_heads(qt), jnp.concatenate(k_rel, axis=0))
        att = jnp.where(keep, att, 0.0).astype(BF16)
        v = v_of(r0)
        o_all = (_dot(att, jnp.concatenate([v] * nsub, axis=0))
                 + _dot_nt(stack_heads(qe), start[r0].astype(BF16)))
        for hd in range(GLA_HEADS):
            cols = slice(hd * GLA_DV, (hd + 1) * GLA_DV)
            o = o_all[hd * chunk:(hd + 1) * chunk, cols]
            og = og_ref[r0:r0 + chunk, cols]
            o_ref[r0:r0 + chunk, cols] = (_rms(o, ng) * (og * jax.nn.sigmoid(og))).astype(BF16)

    @pl.when(j == pl.num_programs(1) - 1)
    def _():
        for s in range(ns):
            for hd in range(GLA_HEADS):
                st_h = st_ref[s, hd * GLA_DV:(hd + 1) * GLA_DV, :]
                sout_ref[s, hd] = st_h.T[hd * GLA_DK:(hd + 1) * GLA_DK, :]


def _gla_call(gq, gk, gv, gog, lg, s0, ng, layer, *, nseq, seq_len, tl, chunk):
    t = gq.shape[0]
    tps = seq_len // tl
    ns = math.gcd(nseq, STREAM_SEQS) if tps == 1 else 1
    tok = lambda w: pl.BlockSpec((ns * tl, w), lambda b, j: (b * tps + j, 0))
    st_spec = pl.BlockSpec((ns, GLA_HEADS, GLA_DK, GLA_DV), lambda b, j: (b, 0, 0, 0))
    s0_spec = pl.BlockSpec((None, ns, GLA_HEADS, GLA_DK, GLA_DV), lambda b, j: (layer, b, 0, 0, 0))
    return pl.pallas_call(
        functools.partial(_gla_kernel, chunk=chunk, nch=tl // chunk, ns=ns),
        grid=(nseq // ns, tps),
        in_specs=[tok(GQK), tok(GQK), tok(GVW), tok(GVW), tok(GQK), s0_spec,
                  _layer_spec((1, GLA_DV), layer)],
        out_specs=[tok(GVW), st_spec],
        out_shape=[jax.ShapeDtypeStruct((t, GVW), BF16),
                   jax.ShapeDtypeStruct((nseq, GLA_HEADS, GLA_DK, GLA_DV), F32)],
        scratch_shapes=[pltpu.VMEM((ns, GVW, GQK), F32)],
        compiler_params=_params(("arbitrary", "arbitrary")),
        name="gla",
    )(gq, gk, gv, gog, lg, s0, ng)


def _softmax_pv(parts, vs, shift_by_max=True):
    if shift_by_max:
        m = functools.reduce(jnp.maximum, [jnp.max(s, axis=-1, keepdims=True) for s in parts])
    acc, den = None, None
    for s, v in zip(parts, vs):
        p = jnp.exp2(s - m) if shift_by_max else jnp.exp2(s)
        d = jnp.sum(p, axis=-1, keepdims=True)
        o = _dot(p.astype(BF16), v)
        acc = o if acc is None else acc + o
        den = d if den is None else den + d
    return acc / den


def _attn_prompt_kernel(q_ref, k_ref, v_ref, nrm_ref, o_ref, *, seq_len, tq):
    group = pl.program_id(1)
    lane = lax.broadcasted_iota(jnp.int32, (tq, LANES), 1)
    qi = lax.broadcasted_iota(jnp.int32, (tq, tq), 0) // CHUNK
    ki = lax.broadcasted_iota(jnp.int32, (tq, tq), 1) // CHUNK
    diag_ok = ki <= qi

    def attend(shift_by_max):
        for i in range(seq_len // tq):
            r0 = i * tq
            for pp in range(ATTN_PAIRS):
                vcols = slice(pp * 2 * MLA_V, (pp + 1) * 2 * MLA_V)
                outs = []
                for hh in range(2):
                    cs = slice((2 * pp + hh) * HEAD_BLOCK, (2 * pp + hh + 1) * HEAD_BLOCK)
                    q = q_ref[r0:r0 + tq, cs]
                    s_d = jnp.where(diag_ok, _dot_nt(q, k_ref[r0:r0 + tq, cs]), NEG_BIG)
                    parts, vs = [s_d], [v_ref[r0:r0 + tq, vcols]]
                    if i > 0:
                        parts.append(_dot_nt(q, k_ref[0:r0, cs]))
                        vs.append(v_ref[0:r0, vcols])
                    outs.append(_softmax_pv(parts, vs, shift_by_max))
                o_ref[r0:r0 + tq, vcols] = jnp.where(lane < MLA_V, outs[0], outs[1]).astype(BF16)

    bound = 0.5 * jnp.max(nrm_ref[...], axis=0)[0:1]
    head_lane = lax.broadcasted_iota(jnp.int32, (1, LANES), 1) // (2 * ATTN_PAIRS)
    safe = jnp.max(jnp.where(head_lane == group, bound, 0.0)) <= SCORE_SAFE
    pl.when(safe)(functools.partial(attend, False))
    pl.when(jnp.logical_not(safe))(functools.partial(attend, True))


def _attn_prompt_call(qh, kh, vh, nrm, *, nseq, seq_len, tq):
    t = qh.shape[0]
    ngroup = MLA_HEADS // (2 * ATTN_PAIRS)
    tps = nrm.shape[0] // nseq
    return pl.pallas_call(
        functools.partial(_attn_prompt_kernel, seq_len=seq_len, tq=tq),
        grid=(nseq, ngroup),
        in_specs=[
            pl.BlockSpec((seq_len, 2 * ATTN_PAIRS * HEAD_BLOCK), lambda b, p: (b, p)),
            pl.BlockSpec((seq_len, 2 * ATTN_PAIRS * HEAD_BLOCK), lambda b, p: (b, p)),
            pl.BlockSpec((seq_len, 2 * ATTN_PAIRS * MLA_V), lambda b, p: (b, p)),
            pl.BlockSpec((tps, NRM_ROWS, LANES), lambda b, p: (b, 0, 0)),
        ],
        out_specs=pl.BlockSpec((seq_len, 2 * ATTN_PAIRS * MLA_V), lambda b, p: (b, p)),
        out_shape=jax.ShapeDtypeStruct((t, MVW), BF16),
        compiler_params=_params(("arbitrary", "arbitrary")),
        name="mla_attn_prompt",
    )(qh, kh, vh, nrm)


def _attn_stream_kernel(q_ref, ckc_ref, kpc_ref, ckn_ref, kpn_ref, wk_ref, wv_ref, place_ref,
                        o_ref, *, tq, ns):
    place_t = place_ref[...]
    blocks = [slice(hd * HEAD_BLOCK, (hd + 1) * HEAD_BLOCK) for hd in range(MLA_HEADS)]
    lane = lax.broadcasted_iota(jnp.int32, (tq, LANES), 1)
    for s in range(ns):
        rows = slice(s * tq, (s + 1) * tq)
        q = q_ref[rows, :]
        ckc = ckc_ref[0, s].astype(BF16)
        kpc_t = _dot(place_t, kpc_ref[0, s].astype(BF16)).astype(BF16)
        ckn = ckn_ref[rows, :].astype(BF16)
        kpn = _dot_nt(kpn_ref[rows, :].astype(BF16), place_t).astype(BF16)
        q_abs = jnp.concatenate(
            [_dot_nt(q[:, bs], wk_ref[:, bs]) for bs in blocks], axis=0).astype(BF16)
        q_blk = jnp.concatenate([q[:, bs] for bs in blocks], axis=0)
        s_c = _dot_nt(q_abs, ckc) + _dot(q_blk, kpc_t)
        s_n = _dot_nt(q_abs, ckn) + _dot_nt(q_blk, kpn)
        lat = _softmax_pv([s_c, s_n], [ckc, ckn]).astype(BF16)
        for p in range(MLA_HEADS // 2):
            wv = wv_ref[:, p * LANES:(p + 1) * LANES]
            even = _dot(lat[(2 * p) * tq:(2 * p + 1) * tq], wv)
            odd = _dot(lat[(2 * p + 1) * tq:(2 * p + 2) * tq], wv)
            o_ref[rows, p * LANES:(p + 1) * LANES] = jnp.where(lane < MLA_V, even, odd).astype(BF16)


def _attn_stream_call(qh, cache_ckv, cache_kpe, ckv_new, kpe_new, w_k, w_v, place, layer,
                      *, nseq, tq, past):
    t = qh.shape[0]
    qpos = past + np.arange(tq)
    if past % CHUNK or not ((qpos[None, :] // CHUNK) <= (qpos[:, None] // CHUNK)).all():
        raise NotImplementedError("new frames spanning several chunks")
    ns = math.gcd(nseq, STREAM_SEQS)
    return pl.pallas_call(
        functools.partial(_attn_stream_kernel, tq=tq, ns=ns),
        grid=(nseq // ns,),
        in_specs=[
            pl.BlockSpec((ns * tq, MQK), lambda b: (b, 0)),
            pl.BlockSpec((1, ns, past, MLA_KV_LORA), lambda b: (layer, b, 0, 0)),
            pl.BlockSpec((1, ns, MLA_ROPE, past), lambda b: (layer, b, 0, 0)),
            pl.BlockSpec((ns * tq, MLA_KV_LORA), lambda b: (b, 0)),
            pl.BlockSpec((ns * tq, MLA_ROPE), lambda b: (b, 0)),
            _layer_spec((MLA_KV_LORA, MQK), layer),
            _layer_spec((MLA_KV_LORA, MVW), layer),
            _const_spec((HEAD_BLOCK, MLA_ROPE)),
        ],
        out_specs=pl.BlockSpec((ns * tq, MVW), lambda b: (b, 0)),
        out_shape=jax.ShapeDtypeStruct((t, MVW), BF16),
        compiler_params=_params(("arbitrary",)),
        name="mla_attn_stream",
    )(qh, cache_ckv, cache_kpe, ckv_new, kpe_new, w_k, w_v, place)


def _prep_weights(w_in, w_gk2, b_gk2, w_uq, w_ukv):
    w_in_p = jnp.pad(w_in.astype(BF16), ((0, 0), (0, 0), (0, IN_W - IN_COLS)))
    w_gk_p = jnp.pad(w_gk2.astype(BF16),
                     ((0, 0), (TAIL_GLR, LANES - TAIL_GLR - GLA_GATE_RANK), (0, 0)))
    b_gk_p = b_gk2.reshape(DEPTH, 1, GQK)
    uq = w_uq.astype(BF16).reshape(DEPTH, MLA_Q_LORA, MLA_HEADS, MLA_NOPE + MLA_ROPE)
    uq = jnp.pad(uq, ((0, 0), (0, 0), (0, 0), (0, HEAD_BLOCK - MLA_NOPE - MLA_ROPE)))
    w_uq_p = uq.reshape(DEPTH, MLA_Q_LORA, MQK)
    ukv = w_ukv.astype(BF16).reshape(DEPTH, MLA_KV_LORA, MLA_HEADS, MLA_NOPE + MLA_V)
    uk = jnp.pad(ukv[..., :MLA_NOPE], ((0, 0), (0, 0), (0, 0), (0, HEAD_BLOCK - MLA_NOPE)))
    w_k_p = uk.reshape(DEPTH, MLA_KV_LORA, MQK)
    w_v_p = ukv[..., MLA_NOPE:].reshape(DEPTH, MLA_KV_LORA, MVW)
    return w_in_p, w_gk_p, b_gk_p, w_uq_p, w_k_p, w_v_p


def _rope_tables(pos, reps):
    half = MLA_ROPE // 2
    inv = ROPE_BASE ** (-jnp.arange(half, dtype=F32) / half)
    ang = pos.astype(F32)[:, None] * inv[None, :]
    cos, sin = jnp.cos(ang), jnp.sin(ang)
    n = pos.shape[0]
    one, zero = jnp.ones((n, MLA_NOPE), F32), jnp.zeros((n, MLA_NOPE), F32)
    pad = jnp.zeros((n, HEAD_BLOCK - MLA_NOPE - MLA_ROPE), F32)
    c = jnp.concatenate([one, cos, cos, pad], axis=1)
    s = jnp.concatenate([zero, -sin, sin, pad], axis=1)
    q_scale = MLA_SCALE * LOG2E
    tabs = (c * q_scale, s * q_scale, c, s)
    return tuple(jnp.tile(a, (reps, 1)) for a in tabs)


def _kpe_placement():
    p = np.zeros((HEAD_BLOCK, MLA_ROPE), np.float32)
    p[MLA_NOPE + np.arange(MLA_ROPE), np.arange(MLA_ROPE)] = 1.0
    return jnp.asarray(p, BF16)


def _trunk(x, mods, seq0, wts, tabs, *, nseq, seq_len, tm, tm_ffn, nb, gla_tl, gla_chunk, s0,
           stream=None, ffn16=None):
    tps = max(seq_len // tm, 1) if nb == 1 else 1
    tile = dict(tm=tm, nb=nb, tps=tps, seq0=seq0)
    tile_ffn = dict(tile, tm=tm_ffn, tps=max(seq_len // tm_ffn, 1)) if nb == 1 else tile
    ckvs, kpes, states, cast = [], [], [], []
    ckv = None
    for l in range(DEPTH):
        if stream is None:
            x = _ffn_call(x, mods, wts["ln_g"], ffn16[l][0], l, **tile_ffn)
        else:
            x, w1 = _ffn_stream_call(x, None, mods, wts["ln_g"], wts["ffn1_in"], wts["ffn1_out"],
                                     wts["final_g"], l, nb=nb, ffn_idx=0, final_norm=False)
        gq, gk, gv, gog, lg, qh, kh, vh, ckv, kpe, nrm = _mixin_call(
            x, mods, wts["ln_g"], wts["w_in"], wts["w_gk"], wts["b_gk"], wts["qg"], wts["w_uq"],
            wts["kvg"], wts["w_k"], wts["w_v"], tabs, l, ckv if (stream is None and l > 0) else None,
            stack_ckv=stream is None, **tile)
        og, st = _gla_call(gq, gk, gv, gog, lg, s0, wts["ng"], l,
                           nseq=nseq, seq_len=seq_len, tl=gla_tl, chunk=gla_chunk)
        last = l == DEPTH - 1
        if stream is None:
            om = _attn_prompt_call(qh, kh, vh, nrm, nseq=nseq, seq_len=seq_len, tq=min(ATTN_TQ, seq_len))
            x = _mix_ffn_call(x, og, om, mods, wts["w_out"], wts["ln_g"], ffn16[l][1], wts["final_g"],
                              l, final_norm=last, **tile_ffn)
        else:
            cache_ckv, cache_kpe_t, place_t = stream
            om = _attn_stream_call(qh, cache_ckv, cache_kpe_t, ckv, kpe, wts["w_k"], wts["w_v"],
                                   place_t, l, nseq=nseq, tq=seq_len, past=cache_ckv.shape[2])
            x, w2 = _ffn_stream_call(x, (og, om, wts["w_out"]), mods, wts["ln_g"], wts["ffn2_in"],
                                     wts["ffn2_out"], wts["final_g"], l, nb=nb, ffn_idx=2,
                                     final_norm=last)
            cast.append((w1, w2))
        ckvs.append(ckv)
        kpes.append(kpe)
        states.append(st)
    ckv_out = ckv if stream is None else jnp.stack(ckvs)
    return x, ckv_out, jnp.stack(kpes), jnp.stack(states), cast


def kernel(x_prompt, x_sample, cache_ckv, cache_kpe, state_gla, c_prompt, c_sample, ln_g, w_ada, b_ada, w_ffn1_in, w_ffn1_out, w_ffn2_in, w_ffn2_out, w_in, w_gk2, b_gk2, gla_norm_g, mla_qnorm_g, w_uq, mla_kvnorm_g, w_ukv, w_out, final_g):
    bsz, seq, _ = x_prompt.shape
    dbs, dseq, _ = x_sample.shape
    past = cache_ckv.shape[2]

    w_in_p, w_gk_p, b_gk_p, w_uq_p, w_k_p, w_v_p = _prep_weights(w_in, w_gk2, b_gk2, w_uq, w_ukv)
    wts = {
        "ln_g": ln_g.reshape(DEPTH, 3, 1, D_MODEL), "final_g": final_g.reshape(1, D_MODEL),
        "ffn1_in": w_ffn1_in, "ffn1_out": w_ffn1_out, "ffn2_in": w_ffn2_in, "ffn2_out": w_ffn2_out,
        "w_in": w_in_p, "w_gk": w_gk_p, "b_gk": b_gk_p, "w_uq": w_uq_p, "w_k": w_k_p, "w_v": w_v_p,
        "qg": mla_qnorm_g.reshape(DEPTH, 1, MLA_Q_LORA),
        "kvg": mla_kvnorm_g.reshape(DEPTH, 1, MLA_KV_LORA),
        "ng": gla_norm_g.reshape(DEPTH, 1, GLA_DV),
        "w_out": w_out.astype(BF16),
    }
    mods = _ada_call(jnp.concatenate([c_sample, c_prompt], axis=0), w_ada, b_ada)
    mods = mods.reshape(DEPTH, N_MOD, dbs + bsz, 1, D_MODEL)

    y_s, ckv_s, kpe_s, gla_s, ffn16 = _trunk(
        x_sample.reshape(dbs * dseq, D_MODEL), mods, 0, wts,
        _rope_tables(past + jnp.arange(dseq), dbs),
        nseq=dbs, seq_len=dseq, tm=dbs * dseq, tm_ffn=dbs * dseq, nb=dbs, gla_tl=dseq, gla_chunk=dseq,
        s0=state_gla.astype(F32),
        stream=(cache_ckv, jnp.swapaxes(cache_kpe, 2, 3), _kpe_placement()))

    tm_p = min(512, seq)
    y_p, ckv_p, kpe_p, gla_p, _ = _trunk(
        x_prompt.reshape(bsz * seq, D_MODEL), mods, dbs, wts, _rope_tables(jnp.arange(seq), 1),
        nseq=bsz, seq_len=seq, tm=tm_p, tm_ffn=min(FFN_TM, seq), nb=1, gla_tl=min(GLA_TL, seq), gla_chunk=CHUNK,
        s0=jnp.zeros((DEPTH, bsz, GLA_HEADS, GLA_DK, GLA_DV), F32), ffn16=ffn16)

    return (y_p.reshape(bsz, seq, D_MODEL), y_s.reshape(dbs, dseq, D_MODEL),
            ckv_p.reshape(DEPTH, bsz, seq, MLA_KV_LORA), jnp.swapaxes(kpe_p, 2, 3),
            gla_p,
            ckv_s.reshape(DEPTH, dbs, dseq, MLA_KV_LORA), kpe_s.reshape(DEPTH, dbs, dseq, MLA_ROPE),
            gla_s)
```
